```python
import jax
import jax.numpy as jnp
from jax import lax
import numpy as np

D_MODEL = 1024
BATCH = 32
SEQ = 256
DEPTH = 4
DEC_BATCH = 8
DEC_SEQ = 2048
PAST_LEN = 256

GRID_W = 64
HEAD_DIM = 64
ATT_HEADS = 8
ATT_KV = 2
WIN_HEADS = 8
WIN_KV = 2
WINDOW = 128
HG_HEADS = 4
HG_DK = 128
HG_DV = 128
BRANCH_W = 512
N_BRANCH = 3
D_FF = 2816
Q_BLOCK = 128
CHUNK = 32
ROPE_THETA = 10000.0
EPS = 1e-6
NEG_INF = -1e30
TINY = 1e-30
N_MOD = 9
ATT_Q_W = ATT_HEADS * HEAD_DIM
ATT_KV_W = ATT_KV * HEAD_DIM
WIN_Q_W = WIN_HEADS * HEAD_DIM
WIN_KV_W = WIN_KV * HEAD_DIM
HG_W = HG_HEADS * HG_DK
IN_SECTIONS = (ATT_Q_W, ATT_KV_W, ATT_KV_W, WIN_Q_W, WIN_KV_W, WIN_KV_W,
               HG_W, HG_W, HG_W, HG_W, HG_W, D_MODEL, D_MODEL, D_MODEL)
IN_W = ATT_Q_W + 2 * ATT_KV_W + WIN_Q_W + 2 * WIN_KV_W + 5 * HG_W + 3 * D_MODEL

kernel_name = 'hybrid_diffusion_prefix_trunk_step'


def _rmsnorm(x, g):
    xf = x.astype(jnp.float32)
    y = xf * lax.rsqrt(jnp.mean(xf * xf, axis=-1, keepdims=True) + EPS)
    return (y * g.astype(jnp.float32)).astype(x.dtype)


def _swiglu(h, w_gu, w_down):
    gu = h @ w_gu
    return (jax.nn.silu(gu[..., :D_FF]) * gu[..., D_FF:]) @ w_down


def _rot(x, cos, sin):
    n = cos.shape[-1]
    x1, x2 = x[..., :n], x[..., n:]
    return jnp.concatenate([x1 * cos - x2 * sin, x1 * sin + x2 * cos], axis=-1)


def _rope_2d(x, rope):
    cos_r, sin_r, cos_c, sin_c = rope
    shape = (x.shape[1],) + (1,) * (x.ndim - 3) + (cos_r.shape[-1],)
    cast = lambda t: t.reshape(shape).astype(x.dtype)
    half = HEAD_DIM // 2
    return jnp.concatenate([_rot(x[..., :half], cast(cos_r), cast(sin_r)),
                            _rot(x[..., half:], cast(cos_c), cast(sin_c))], axis=-1)


def _attend(q, k, v, mask, sink):
    s = jnp.einsum('bqgrd,bkgd->bgrqk', q, k).astype(jnp.float32) * (HEAD_DIM ** -0.5)
    if mask is not None:
        s = jnp.where(mask, s, NEG_INF)
    if sink is None:
        p = jax.nn.softmax(s, axis=-1)
    else:
        sk = sink.astype(jnp.float32)[None, :, :, None, None]
        m = jnp.maximum(jnp.max(s, axis=-1, keepdims=True), sk)
        e = jnp.exp(s - m)
        p = e / (jnp.sum(e, axis=-1, keepdims=True) + jnp.exp(sk - m))
    return jnp.einsum('bgrqk,bkgd->bqgrd', p.astype(v.dtype), v)


def _dense_blocked(q, k, v, sink):
    b, s, g, r, d = q.shape
    nb = s // Q_BLOCK
    qb = jnp.moveaxis(q.reshape(b, nb, Q_BLOCK, g, r, d), 1, 0)
    ob = lax.map(lambda qi: _attend(qi, k, v, None, sink), qb)
    return jnp.moveaxis(ob, 0, 1).reshape(b, s, g, r, d)


def _window_blocked(q, k, v, k_ctx, v_ctx, sink):
    b, s, g, r, d = q.shape
    nb = s // Q_BLOCK
    n_ctx = k_ctx.shape[1]

    def bands(t):
        tp = jnp.pad(t, ((0, 0), (Q_BLOCK, Q_BLOCK), (0, 0), (0, 0))).reshape(b, nb + 2, Q_BLOCK, g, d)
        band = jnp.concatenate([tp[:, :-2], tp[:, 1:-1], tp[:, 2:]], axis=2)
        return jnp.moveaxis(band, 1, 0)

    qb = jnp.moveaxis(q.reshape(b, nb, Q_BLOCK, g, r, d), 1, 0)
    q_off = jnp.arange(Q_BLOCK)
    k_off = jnp.arange(3 * Q_BLOCK) - Q_BLOCK
    ctx_ok = jnp.ones((Q_BLOCK, n_ctx), dtype=bool)

    def step(args):
        qi, ki, vi, i = args
        t = i * Q_BLOCK + q_off
        src = i * Q_BLOCK + k_off
        band_ok = ((jnp.abs(t[:, None] - src[None, :]) <= WINDOW)
                   & (src >= 0)[None, :] & (src < s)[None, :])
        mask = jnp.concatenate([ctx_ok, band_ok], axis=1)
        return _attend(qi, jnp.concatenate([k_ctx, ki], axis=1),
                       jnp.concatenate([v_ctx, vi], axis=1), mask, sink)

    ob = lax.map(step, (qb, bands(k), bands(v), jnp.arange(nb)))
    return jnp.moveaxis(ob, 0, 1).reshape(b, s, g, r, d)


def _gla_scan(q, k, v, logf, s0):
    b, s, h, _ = q.shape
    n = s // CHUNK

    def chunks(t):
        return t.reshape(b, n, CHUNK, h, t.shape[-1]).transpose(1, 0, 3, 2, 4)

    causal = jnp.tril(jnp.ones((CHUNK, CHUNK), dtype=bool))[:, :, None]

    def step(state, xs):
        qc, kc, vc, gc = xs
        bcum = lax.cumsum(gc, axis=2)
        decay = jnp.exp(jnp.where(causal, bcum[:, :, :, None, :] - bcum[:, :, None, :, :], NEG_INF))
        scores = jnp.einsum('bhtd,bhsd,bhtsd->bhts', qc, kc, decay)
        out = (jnp.einsum('bhts,bhsv->bhtv', scores, vc)
               + jnp.einsum('bhtd,bhdv->bhtv', qc * jnp.exp(bcum), state))
        blast = bcum[:, :, -1:, :]
        new_state = (jnp.exp(blast)[:, :, 0, :, None] * state
                     + jnp.einsum('bhsd,bhsv->bhdv', kc * jnp.exp(blast - bcum), vc))
        return new_state, out

    s_fin, oc = lax.scan(step, s0.astype(jnp.float32), (chunks(q), chunks(k), chunks(v), chunks(logf)))
    return oc.transpose(1, 0, 3, 2, 4).reshape(b, s, h, v.shape[-1]), s_fin


def _hgrn(q, z_fwd, z_bwd, i, g, lb, norm_g, s0):
    b, s, _ = q.shape
    heads = lambda t, d: t.astype(jnp.float32).reshape(b, s, HG_HEADS, d)
    qh = heads(jax.nn.silu(q), HG_DK)
    vh = heads(i, HG_DV)

    def gates(z, lbd):
        zf = z.astype(jnp.float32)
        f = lbd + (1.0 - lbd) * jax.nn.sigmoid(zf)
        logf = jnp.log(jnp.maximum(f, TINY))
        key = (1.0 - lbd) * jax.nn.sigmoid(-zf)
        return heads(key, HG_DK), heads(logf, HG_DK)

    k_f, g_f = gates(z_fwd, lb[0])
    k_b, g_b = gates(z_bwd, lb[1])
    rev = lambda t: jnp.flip(t, axis=1)
    o_f, s_f = _gla_scan(qh, k_f, vh, g_f, s0[:, 0])
    o_b, s_b = _gla_scan(rev(qh), rev(k_b), rev(vh), rev(g_b), s0[:, 1])
    o = _rmsnorm(o_f + rev(o_b), norm_g) * jax.nn.silu(heads(g, HG_DV))
    return o.reshape(b, s, HG_HEADS * HG_DV).astype(q.dtype), jnp.stack([s_f, s_b], axis=1)


def _split_cols(t):
    out, start = [], 0
    for w in IN_SECTIONS:
        out.append(t[..., start:start + w])
        start += w
    return out


def _mixer(h, lw, rope, ctx):
    b, s, _ = h.shape
    (aq, ak, av, wq, wk, wv, hq, hf_fwd, hf_bwd, hi, hg, ga, gb, gc) = _split_cols(h @ lw['w_in'])
    aq = _rmsnorm(aq.reshape(b, s, ATT_KV, ATT_HEADS // ATT_KV, HEAD_DIM), lw['q_norm_g'])
    ak = _rmsnorm(ak.reshape(b, s, ATT_KV, HEAD_DIM), lw['k_norm_g'])
    av = av.reshape(b, s, ATT_KV, HEAD_DIM)
    wq = wq.reshape(b, s, WIN_KV, WIN_HEADS // WIN_KV, HEAD_DIM)
    wk = wk.reshape(b, s, WIN_KV, HEAD_DIM)
    wv = wv.reshape(b, s, WIN_KV, HEAD_DIM)
    sink = lw['sink'].reshape(WIN_KV, WIN_HEADS // WIN_KV)
    if ctx is None:
        o_att = _dense_blocked(aq, ak, av, None)
        o_win = _dense_blocked(wq, wk, wv, sink)
        s0 = jnp.zeros((b, 2, HG_HEADS, HG_DK, HG_DV), jnp.float32)
    else:
        ck_att, cv_att, ck_win, cv_win, s0 = ctx
        aq, ak = _rope_2d(aq, rope), _rope_2d(ak, rope)
        wq, wk = _rope_2d(wq, rope), _rope_2d(wk, rope)
        o_att = _dense_blocked(aq, jnp.concatenate([ck_att, ak], axis=1),
                               jnp.concatenate([cv_att, av], axis=1), None)
        o_win = _window_blocked(wq, wk, wv, ck_win, cv_win, sink)
    o_hg, s_fin = _hgrn(hq, hf_fwd, hf_bwd, hi, hg, lw['lb'], lw['hg_norm_g'], s0)
    wb = lw['w_branch']
    merged = (jax.nn.sigmoid(ga) * (o_att.reshape(b, s, BRANCH_W) @ wb[0])
              + jax.nn.sigmoid(gb) * (o_hg @ wb[1])
              + jax.nn.sigmoid(gc) * (o_win.reshape(b, s, BRANCH_W) @ wb[2]))
    y = merged @ lw['w_out']
    new_ctx = (ak, av, wk, wv, s_fin) if ctx is None else None
    return y, new_ctx


def _layer(x, cond, lw, rope, ctx):
    mod = (jax.nn.silu(cond) @ lw['w_mod'] + lw['b_mod']).reshape(cond.shape[0], 1, N_MOD, D_MODEL)

    def pre(j, t):
        return _rmsnorm(t, lw['norm_g'][j]) * (1.0 + mod[:, :, 3 * j + 1]) + mod[:, :, 3 * j]

    x = x + 0.5 * mod[:, :, 2] * _swiglu(pre(0, x), lw['ffn_in'][0], lw['ffn_out'][0])
    y, new_ctx = _mixer(pre(1, x), lw, rope, ctx)
    x = x + mod[:, :, 5] * y
    x = x + 0.5 * mod[:, :, 8] * _swiglu(pre(2, x), lw['ffn_in'][1], lw['ffn_out'][1])
    return x, new_ctx


def setup_inputs(seed: int = 0) -> dict:
    key = jax.random.key(seed)
    ks = jax.random.split(key, 22)
    nrm = lambda k, shape, scale: jax.random.normal(k, shape, jnp.float32) * scale
    return {
        'x_prompt': nrm(ks[0], (BATCH, SEQ, D_MODEL), 1.0),
        'x_sample': nrm(ks[1], (DEC_BATCH, DEC_SEQ, D_MODEL), 1.0),
        'cache_k_attn': nrm(ks[2], (DEC_BATCH, DEPTH, PAST_LEN, ATT_KV, HEAD_DIM), 1.0),
        'cache_v_attn': nrm(ks[3], (DEC_BATCH, DEPTH, PAST_LEN, ATT_KV, HEAD_DIM), 1.0),
        'cache_k_win': nrm(ks[4], (DEC_BATCH, DEPTH, PAST_LEN, WIN_KV, HEAD_DIM), 1.0),
        'cache_v_win': nrm(ks[5], (DEC_BATCH, DEPTH, PAST_LEN, WIN_KV, HEAD_DIM), 1.0),
        'state_hgrn': nrm(ks[6], (DEC_BATCH, DEPTH, 2, HG_HEADS, HG_DK, HG_DV), 0.5),
        'c': nrm(ks[7], (DEC_BATCH, D_MODEL), 1.0),
        'c_ctx': nrm(ks[8], (D_MODEL,), 1.0),
        'w_mod': nrm(ks[9], (DEPTH, D_MODEL, N_MOD * D_MODEL), D_MODEL ** -0.5),
        'b_mod': nrm(ks[10], (DEPTH, N_MOD * D_MODEL), 0.01),
        'norm_g': 1.0 + nrm(ks[11], (DEPTH, 3, D_MODEL), 0.01),
        'w_ffn_in': nrm(ks[12], (DEPTH, 2, D_MODEL, 2 * D_FF), D_MODEL ** -0.5),
        'w_ffn_out': nrm(ks[13], (DEPTH, 2, D_FF, D_MODEL), D_FF ** -0.5),
        'w_in': nrm(ks[14], (DEPTH, D_MODEL, IN_W), D_MODEL ** -0.5),
        'qk_norm_g': 1.0 + nrm(ks[15], (DEPTH, 2, HEAD_DIM), 0.01),
        'lower_bounds': nrm(ks[16], (DEPTH, 2, HG_W), 0.1),
        'hg_norm_g': 1.0 + nrm(ks[17], (DEPTH, HG_DV), 0.01),
        'sink_logit': nrm(ks[18], (DEPTH, WIN_HEADS), 0.5),
        'w_branch': nrm(ks[19], (DEPTH, N_BRANCH, BRANCH_W, D_MODEL), BRANCH_W ** -0.5),
        'w_out': nrm(ks[20], (DEPTH, D_MODEL, D_MODEL), D_MODEL ** -0.5),
        'final_norm_g': 1.0 + nrm(ks[21], (D_MODEL,), 0.01),
    }


def reference(x_prompt, x_sample, cache_k_attn, cache_v_attn, cache_k_win, cache_v_win, state_hgrn,
              c, c_ctx, w_mod, b_mod, norm_g, w_ffn_in, w_ffn_out, w_in, qk_norm_g, lower_bounds,
              hg_norm_g, sink_logit, w_branch, w_out, final_norm_g):
    lb_soft = jax.nn.softmax(lower_bounds.astype(jnp.float32), axis=0)
    lb_all = jnp.cumsum(lb_soft, axis=0) - lb_soft[0]
    layers = [dict(w_mod=w_mod[l], b_mod=b_mod[l], norm_g=norm_g[l], ffn_in=w_ffn_in[l],
                   ffn_out=w_ffn_out[l], w_in=w_in[l], q_norm_g=qk_norm_g[l, 0],
                   k_norm_g=qk_norm_g[l, 1], sink=sink_logit[l], lb=lb_all[l],
                   hg_norm_g=hg_norm_g[l], w_branch=w_branch[l], w_out=w_out[l])
              for l in range(DEPTH)]

    xp = x_prompt
    cond_ctx = c_ctx[None, :]
    ctx_out = []
    for l in range(DEPTH):
        xp, cx = _layer(xp, cond_ctx, layers[l], None, None)
        ctx_out.append(cx)
    y_prompt = _rmsnorm(xp, final_norm_g)
    new_k_attn = jnp.stack([cx[0] for cx in ctx_out], axis=1)
    new_v_attn = jnp.stack([cx[1] for cx in ctx_out], axis=1)
    new_k_win = jnp.stack([cx[2] for cx in ctx_out], axis=1)
    new_v_win = jnp.stack([cx[3] for cx in ctx_out], axis=1)
    new_state_hgrn = jnp.stack([cx[4] for cx in ctx_out], axis=1)

    n_lat = x_sample.shape[1]
    rows = n_lat // GRID_W
    row = jnp.repeat(jnp.arange(rows, dtype=jnp.float32), GRID_W)
    col = jnp.tile(jnp.arange(GRID_W, dtype=jnp.float32), rows)
    axis_dim = HEAD_DIM // 2
    inv = ROPE_THETA ** (-jnp.arange(0, axis_dim, 2, dtype=jnp.float32) / axis_dim)
    ang_r = row[:, None] * inv
    ang_c = col[:, None] * inv
    rope = (jnp.cos(ang_r), jnp.sin(ang_r), jnp.cos(ang_c), jnp.sin(ang_c))
    xs = x_sample
    for l in range(DEPTH):
        ctx = (cache_k_attn[:, l], cache_v_attn[:, l], cache_k_win[:, l], cache_v_win[:, l], state_hgrn[:, l])
        xs, _ = _layer(xs, c, layers[l], rope, ctx)
    y_sample = _rmsnorm(xs, final_norm_g)
    return (y_prompt, y_sample, new_k_attn, new_v_attn, new_k_win, new_v_win, new_state_hgrn)
```

```python
import functools

import jax
import jax.numpy as jnp
from jax import lax
from jax.experimental import pallas as pl
from jax.experimental.pallas import tpu as pltpu

F32 = jnp.float32
BF16 = jnp.bfloat16

D_MODEL = 1024
DEPTH = 4
GRID_W = 64
HEAD_DIM = 64
ATT_KV = 2
WINDOW = 128
HG_HEADS = 4
HG_DK = 128
BRANCH_W = 512
D_FF = 2816
ROPE_THETA = 10000.0
EPS = 1e-6
NEG_INF = -1e30
TINY = 1e-30
N_MOD = 9
IN_W = 7168

LANES = 128
SUBLANES = 8

COND_ROWS = 16
FFN_TM = 1024
FFN_TF = 256
INPROJ_TN = 256
MERGE_TM = 256
ATT_TQ = 256
HG_CHUNK = 128
HG_SUB = SUBLANES
VMEM_LIMIT = 48 * 1024 * 1024

COL_AQ, COL_AK, COL_AV = 0, 512, 640
COL_WQ, COL_WK, COL_WV = 768, 1280, 1408
COL_HQ, COL_HF, COL_HI, COL_HG = 1536, 2048, 3072, 3584
COL_GATES = 4096


def _params(sem):
    return pltpu.CompilerParams(dimension_semantics=sem, vmem_limit_bytes=VMEM_LIMIT)


def _sigmoid(x):
    return 1.0 / (1.0 + jnp.exp(-x))


def _dot(a, b):
    return jnp.dot(a, b, preferred_element_type=F32)


def _dot_nt(a, b):
    return lax.dot_general(a, b, (((1,), (1,)), ((), ())), preferred_element_type=F32)


def _rms_rows(x, gain):
    return x * lax.rsqrt(jnp.mean(x * x, axis=-1, keepdims=True) + EPS) * gain


def _mod_kernel(c_ref, w_ref, b_ref, o_ref):
    c = c_ref[...]
    h = (c * _sigmoid(c)).astype(BF16)
    o_ref[...] = _dot(h, w_ref[...].astype(BF16)) + b_ref[...]


def _mod_call(cond, w_mod, b_mod):
    depth, d, nm = w_mod.shape
    tn = nm // 8
    return pl.pallas_call(
        _mod_kernel,
        grid=(depth, nm // tn),
        in_specs=[pl.BlockSpec((COND_ROWS, d), lambda l, j: (0, 0)),
                  pl.BlockSpec((None, d, tn), lambda l, j: (l, 0, j)),
                  pl.BlockSpec((None, 1, tn), lambda l, j: (l, 0, j))],
        out_specs=pl.BlockSpec((None, COND_ROWS, tn), lambda l, j: (l, 0, j)),
        out_shape=jax.ShapeDtypeStruct((depth, COND_ROWS, nm), F32),
        compiler_params=_params(("parallel", "parallel")),
        name="mod",
    )(cond, w_mod, b_mod.reshape(depth, 1, nm))


def _lb_kernel(x_ref, o_ref):
    x = x_ref[...]
    m = jnp.max(x, axis=0, keepdims=True)
    e = jnp.exp(x - m)
    s = e / jnp.sum(e, axis=0, keepdims=True)
    acc = jnp.zeros_like(s[0:1])
    for l in range(x.shape[0]):
        acc = acc + s[l:l + 1]
        o_ref[l:l + 1, :] = acc - s[0:1]


def _lb_call(lower_bounds):
    depth = lower_bounds.shape[0]
    x = lower_bounds.reshape(depth, -1).astype(F32)
    return pl.pallas_call(_lb_kernel, out_shape=jax.ShapeDtypeStruct(x.shape, F32), name="lb")(x)


def _ffn_kernel(*refs, nj, final):
    if final:
        x_ref, shift_ref, scale_ref, gate_ref, g_ref, wg_ref, wu_ref, wd_ref, fg_ref, o_ref, h_scr = refs
    else:
        x_ref, shift_ref, scale_ref, gate_ref, g_ref, wg_ref, wu_ref, wd_ref, o_ref, h_scr = refs
    j = pl.program_id(1)

    @pl.when(j == 0)
    def _():
        h = _rms_rows(x_ref[...], g_ref[...]) * (1.0 + scale_ref[...]) + shift_ref[...]
        h_scr[...] = h.astype(BF16)
        o_ref[...] = jnp.zeros_like(o_ref)

    h = h_scr[...]
    a = _dot(h, wg_ref[...])
    u = _dot(h, wu_ref[...])
    act = (a * _sigmoid(a) * u).astype(BF16)
    o_ref[...] += _dot(act, wd_ref[...])

    @pl.when(j == nj - 1)
    def _():
        out = x_ref[...] + 0.5 * gate_ref[...] * o_ref[...]
        if final:
            out = _rms_rows(out, fg_ref[...])
        o_ref[...] = out


def _mod_spec(layer, k, row_fn):
    return pl.BlockSpec((None, 1, D_MODEL),
                        lambda i, j=0: ((layer * COND_ROWS + row_fn(i)) * N_MOD + k, 0, 0))


def _ffn_call(x, modv, normv, w_in_b, w_out_b, layer, which, row_fn, final_g=None):
    n = x.shape[0]
    tm = min(FFN_TM, n)
    nj = D_FF // FFN_TF
    k0 = 0 if which == 0 else 6
    nslot = 0 if which == 0 else 2
    final = final_g is not None
    in_specs = [
        pl.BlockSpec((tm, D_MODEL), lambda i, j: (i, 0)),
        _mod_spec(layer, k0, row_fn), _mod_spec(layer, k0 + 1, row_fn), _mod_spec(layer, k0 + 2, row_fn),
        pl.BlockSpec((None, 1, D_MODEL), lambda i, j: (layer * 3 + nslot, 0, 0)),
        pl.BlockSpec((None, None, D_MODEL, FFN_TF), lambda i, j: (layer, which, 0, j)),
        pl.BlockSpec((None, None, D_MODEL, FFN_TF), lambda i, j: (layer, which, 0, nj + j)),
        pl.BlockSpec((None, None, FFN_TF, D_MODEL), lambda i, j: (layer, which, j, 0)),
    ]
    args = [x, modv, modv, modv, normv, w_in_b, w_in_b, w_out_b]
    if final:
        in_specs.append(pl.BlockSpec((1, D_MODEL), lambda i, j: (0, 0)))
        args.append(final_g.reshape(1, D_MODEL))
    return pl.pallas_call(
        functools.partial(_ffn_kernel, nj=nj, final=final),
        grid=(n // tm, nj),
        in_specs=in_specs,
        out_specs=pl.BlockSpec((tm, D_MODEL), lambda i, j: (i, 0)),
        out_shape=jax.ShapeDtypeStruct((n, D_MODEL), F32),
        scratch_shapes=[pltpu.VMEM((tm, D_MODEL), BF16)],
        compiler_params=_params(("parallel", "arbitrary")),
        name="ffn",
    )(*args)


def _head_rms(y, gain):
    lane = lax.broadcasted_iota(jnp.int32, (1, LANES), 1)
    lo = lane < HEAD_DIM
    outs = []
    for c in range(y.shape[1] // LANES):
        blk = y[:, c * LANES:(c + 1) * LANES]
        sq = blk * blk
        s_lo = jnp.sum(jnp.where(lo, sq, 0.0), axis=-1, keepdims=True)
        s_hi = jnp.sum(jnp.where(lo, 0.0, sq), axis=-1, keepdims=True)
        ms = jnp.where(lo, s_lo, s_hi) * (1.0 / HEAD_DIM)
        outs.append(blk * lax.rsqrt(ms + EPS) * gain[:, c * LANES:(c + 1) * LANES])
    return outs[0] if len(outs) == 1 else jnp.concatenate(outs, axis=-1)


def _inproj_kernel(x_ref, shift_ref, scale_ref, g_ref, w_ref, qg_ref, kg_ref, y_ref, h_scr):
    j = pl.program_id(1)

    @pl.when(j == 0)
    def _():
        h = _rms_rows(x_ref[...], g_ref[...]) * (1.0 + scale_ref[...]) + shift_ref[...]
        h_scr[...] = h.astype(BF16)

    y = _dot(h_scr[...], w_ref[...])
    n_q_tiles = COL_AK // INPROJ_TN

    @pl.when(j < n_q_tiles)
    def _():
        y_ref[...] = _head_rms(y, qg_ref[...])

    @pl.when(j == n_q_tiles)
    def _():
        y_ref[:, :LANES] = _head_rms(y[:, :LANES], kg_ref[...])
        y_ref[:, LANES:] = y[:, LANES:]

    @pl.when(j > n_q_tiles)
    def _():
        y_ref[...] = y


def _inproj_call(x, modv, normv, w_in_b, qk_norm_g, layer, row_fn):
    n = x.shape[0]
    tm = min(FFN_TM, n)
    tn = INPROJ_TN
    qg = jnp.tile(qk_norm_g[layer, 0].astype(F32), tn // HEAD_DIM).reshape(1, tn)
    kg = jnp.tile(qk_norm_g[layer, 1].astype(F32), LANES // HEAD_DIM).reshape(1, LANES)
    return pl.pallas_call(
        _inproj_kernel,
        grid=(n // tm, IN_W // tn),
        in_specs=[
            pl.BlockSpec((tm, D_MODEL), lambda i, j: (i, 0)),
            _mod_spec(layer, 3, row_fn), _mod_spec(layer, 4, row_fn),
            pl.BlockSpec((None, 1, D_MODEL), lambda i, j: (layer * 3 + 1, 0, 0)),
            pl.BlockSpec((None, D_MODEL, tn), lambda i, j: (layer, 0, j)),
            pl.BlockSpec((1, tn), lambda i, j: (0, 0)),
            pl.BlockSpec((1, LANES), lambda i, j: (0, 0)),
        ],
        out_specs=pl.BlockSpec((tm, tn), lambda i, j: (i, j)),
        out_shape=jax.ShapeDtypeStruct((n, IN_W), F32),
        scratch_shapes=[pltpu.VMEM((tm, D_MODEL), BF16)],
        compiler_params=_params(("parallel", "arbitrary")),
        name="inproj",
    )(x, modv, modv, normv, w_in_b, qg, kg)


def _rope(x, cos, sin_signed):
    lane = lax.broadcasted_iota(jnp.int32, (1, LANES), 1)
    first = (lane & 31) < 16
    swapped = jnp.where(first, pltpu.roll(x, LANES - 16, 1), pltpu.roll(x, 16, 1))
    return x * cos + swapped * sin_signed


def _attn_kernel(*refs, tq, seq, past, use_rope, use_sink, window, sink_off):
    it = iter(refs)
    sink_ref = next(it) if use_sink else None
    q_refs = (next(it), next(it))
    k_ref, v_ref = next(it), next(it)
    ck_ref = cv_ref = cos_ref = sin_ref = None
    if past:
        ck_ref, cv_ref = next(it), next(it)
    if use_rope:
        cos_ref, sin_ref = next(it), next(it)
    o_ref = next(it)
    kscr, vscr = next(it), next(it)

    qi = pl.program_id(1)
    lane = lax.broadcasted_iota(jnp.int32, (1, LANES), 1)
    lo = lane < HEAD_DIM
    rows = min(seq, 256)

    def put(scr, x, r0):
        xr = pltpu.roll(x, HEAD_DIM, 1)
        sl = pl.ds(r0, x.shape[0])
        scr[0, sl, :] = jnp.where(lo, x, 0.0).astype(BF16)
        scr[1, sl, :] = jnp.where(lo, 0.0, xr).astype(BF16)
        scr[2, sl, :] = jnp.where(lo, xr, 0.0).astype(BF16)
        scr[3, sl, :] = jnp.where(lo, 0.0, x).astype(BF16)

    @pl.when(qi == 0)
    def _build():
        if past:
            put(kscr, ck_ref[...], 0)
            put(vscr, cv_ref[...], 0)

        def body(c, carry):
            r = pl.multiple_of(c * rows, rows)
            k = k_ref[pl.ds(r, rows), :]
            if use_rope:
                k = _rope(k, cos_ref[pl.ds(r, rows), :], sin_ref[pl.ds(r, rows), :])
            put(kscr, k, past + r)
            put(vscr, v_ref[pl.ds(r, rows), :], past + r)
            return carry

        lax.fori_loop(0, seq // rows, body, 0)

    q0 = pl.multiple_of(qi * tq, tq)
    if use_rope:
        cosq = cos_ref[pl.ds(q0, tq), :]
        sinq = sin_ref[pl.ds(q0, tq), :]
    if window:
        span = tq + 2 * WINDOW
        start = pl.multiple_of(jnp.clip(q0 - WINDOW, 0, seq - span), WINDOW)
        t_pos = q0 + lax.broadcasted_iota(jnp.int32, (tq, 1), 0)
        s_pos = start + lax.broadcasted_iota(jnp.int32, (1, span), 1)
        band_ok = jnp.abs(t_pos - s_pos) <= WINDOW

    for g in range(ATT_KV):
        for p in range(2):
            c0 = p * LANES
            q = q_refs[g][:, c0:c0 + LANES]
            if use_rope:
                q = _rope(q, cosq, sinq)
            qb = (q * (HEAD_DIM ** -0.5)).astype(BF16)
            o_pair = None
            for half in range(2):
                idx = g * 2 + half
                if window:
                    segs = [(_dot_nt(qb, kscr[idx, 0:past, :]), vscr[idx, 0:past, :])]
                    s_band = _dot_nt(qb, kscr[idx, pl.ds(past + start, span), :])
                    segs.append((jnp.where(band_ok, s_band, NEG_INF), vscr[idx, pl.ds(past + start, span), :]))
                else:
                    segs = [(_dot_nt(qb, kscr[idx]), vscr[idx])]
                m = None
                for s, _ in segs:
                    ms = jnp.max(s, axis=-1, keepdims=True)
                    m = ms if m is None else jnp.maximum(m, ms)
                if use_sink:
                    sk = sink_ref[sink_off + g * 4 + p * 2 + half]
                    m = jnp.maximum(m, sk)
                denom = jnp.exp(sk - m) if use_sink else None
                acc = None
                for s, v in segs:
                    e = jnp.exp(s - m)
                    es = jnp.sum(e, axis=-1, keepdims=True)
                    denom = es if denom is None else denom + es
                    pv = _dot(e.astype(BF16), v)
                    acc = pv if acc is None else acc + pv
                acc = acc * (1.0 / denom)
                o_pair = acc if o_pair is None else o_pair + acc
            o_ref[:, g * 256 + c0:g * 256 + c0 + LANES] = o_pair


def _attn_call(y, nb, seq, qcol, kcol, vcol, *, cache_k=None, cache_v=None, layer=0, rope=None,
               sink=None, window=False):
    tq = min(ATT_TQ, seq)
    nq = seq // tq
    past = 0 if cache_k is None else cache_k.shape[2]
    use_rope = rope is not None
    use_sink = sink is not None
    in_specs, args = [], []
    if use_sink:
        in_specs.append(pl.BlockSpec(memory_space=pltpu.SMEM))
        args.append(sink.reshape(-1).astype(F32))
    for g in range(ATT_KV):
        in_specs.append(pl.BlockSpec((tq, 256), lambda b, qi, g=g: (b * nq + qi, qcol // 256 + g)))
        args.append(y)
    in_specs.append(pl.BlockSpec((seq, LANES), lambda b, qi: (b, kcol // LANES)))
    in_specs.append(pl.BlockSpec((seq, LANES), lambda b, qi: (b, vcol // LANES)))
    args += [y, y]
    if past:
        cshape = cache_k.shape[:3] + (LANES,)
        in_specs += [pl.BlockSpec((None, None, past, LANES), lambda b, qi: (b, layer, 0, 0))] * 2
        args += [cache_k.reshape(cshape), cache_v.reshape(cshape)]
    if use_rope:
        in_specs += [pl.BlockSpec((seq, LANES), lambda b, qi: (0, 0))] * 2
        args += list(rope)
    kern = functools.partial(_attn_kernel, tq=tq, seq=seq, past=past, use_rope=use_rope,
                             use_sink=use_sink, window=window, sink_off=layer * 8)
    return pl.pallas_call(
        kern,
        grid=(nb, nq),
        in_specs=in_specs,
        out_specs=pl.BlockSpec((tq, BRANCH_W), lambda b, qi: (b * nq + qi, 0)),
        out_shape=jax.ShapeDtypeStruct((nb * seq, BRANCH_W), F32),
        scratch_shapes=[pltpu.VMEM((4, past + seq, LANES), BF16)] * 2,
        compiler_params=_params(("parallel", "arbitrary")),
        name="win" if window or use_sink else "att",
    )(*args)


def _hgrn_kernel(*refs, nc, has_init, emit_state):
    it = iter(refs)
    q_ref, z_ref, v_ref, lb_ref = next(it), next(it), next(it), next(it)
    s0_ref = next(it) if has_init else None
    o_ref = next(it)
    sfin_ref = next(it) if emit_state else None
    st_scr = next(it)

    d = pl.program_id(2)
    j = pl.program_id(3)
    C = HG_CHUNK
    fwd = d == 0

    @pl.when(j == 0)
    def _():
        if has_init:
            st_scr[...] = s0_ref[...].T
        else:
            st_scr[...] = jnp.zeros_like(st_scr)

    q = q_ref[...]
    q = q * _sigmoid(q)
    z = z_ref[...]
    lb = lb_ref[...]
    ez = jnp.exp(-jnp.abs(z))
    rz = 1.0 / (1.0 + ez)
    sig_pos = jnp.where(z >= 0, rz, ez * rz)
    sig_neg = jnp.where(z >= 0, ez * rz, rz)
    logf = jnp.log(jnp.maximum(lb + (1.0 - lb) * sig_pos, TINY))
    k = (1.0 - lb) * sig_neg
    v = v_ref[...]
    vb = v.astype(BF16)

    ti = lax.broadcasted_iota(jnp.int32, (C, C), 0)
    si = lax.broadcasted_iota(jnp.int32, (C, C), 1)
    rel = jnp.where(fwd, ti - si, si - ti)
    tri = jnp.where(rel >= 0, 1.0, 0.0).astype(F32)
    b = jnp.dot(tri, logf, precision=lax.Precision.HIGHEST, preferred_element_type=F32)

    G = C // HG_SUB
    b3 = b.reshape(G, HG_SUB, HG_DK)
    q3 = q.reshape(G, HG_SUB, HG_DK)
    k3 = k.reshape(G, HG_SUB, HG_DK)
    v3 = v.reshape(G, HG_SUB, HG_DK)
    r8 = lax.broadcasted_iota(jnp.int32, (1, HG_SUB, 1), 1)
    pos8 = jnp.where(fwd, r8, HG_SUB - 1 - r8)
    out3 = jnp.zeros((G, HG_SUB, HG_DK), F32)
    for jj in range(HG_SUB):
        jpos = jnp.where(fwd, jj, HG_SUB - 1 - jj)
        ok = pos8 >= jpos
        dec = jnp.exp(jnp.where(ok, b3 - b3[:, jj:jj + 1, :], NEG_INF))
        score = jnp.sum(q3 * (k3[:, jj:jj + 1, :] * dec), axis=-1, keepdims=True)
        out3 = out3 + score * v3[:, jj:jj + 1, :]
    out = out3.reshape(C, HG_DK)

    tcol = lax.broadcasted_iota(jnp.int32, (C, 1), 0)
    pcol = jnp.where(fwd, tcol, C - 1 - tcol)
    a_mat = jnp.zeros((C, C), F32)
    hh = HG_SUB
    while hh < C:
        grp = 2 * hh
        is_q = (pcol & (grp - 1)) >= hh
        bg = b.reshape(C // grp, grp, HG_DK)
        bnd = jnp.where(fwd, bg[:, hh - 1:hh, :], bg[:, hh:hh + 1, :])
        bnd = jnp.broadcast_to(bnd, (C // grp, grp, HG_DK)).reshape(C, HG_DK)
        ex = jnp.exp(jnp.where(is_q, b - bnd, bnd - b))
        qt = jnp.where(is_q, q * ex, 0.0).astype(BF16)
        kt = jnp.where(is_q, 0.0, k * ex).astype(BF16)
        sc = _dot_nt(qt, kt)
        shift = grp.bit_length() - 1
        same = (ti >> shift) == (si >> shift)
        a_mat = a_mat + jnp.where(same, sc, 0.0)
        hh = grp
    out = out + _dot(a_mat.astype(BF16), vb)

    st = st_scr[...]
    out = out + _dot_nt((q * jnp.exp(b)).astype(BF16), st.astype(BF16))
    o_ref[...] = out
    blast = jnp.where(fwd, b[C - 1:C, :], b[0:1, :])
    kh = (k * jnp.exp(blast - b)).astype(BF16)
    st_new = st * jnp.exp(blast) + _dot(v.T.astype(BF16), kh)
    st_scr[...] = st_new
    if emit_state:
        @pl.when(j == nc - 1)
        def _():
            sfin_ref[...] = st_new.T


def _hgrn_call(y, lbv, nb, seq, layer, state0=None, emit_state=False):
    nc = seq // HG_CHUNK
    has_init = state0 is not None

    def chunk(d, j):
        return j + d * (nc - 1 - 2 * j)

    def yspec(col_fn):
        return pl.BlockSpec((HG_CHUNK, LANES), lambda n, h, d, j: (n * nc + chunk(d, j), col_fn(h, d)))

    in_specs = [
        yspec(lambda h, d: COL_HQ // LANES + h),
        yspec(lambda h, d: COL_HF // LANES + d * HG_HEADS + h),
        yspec(lambda h, d: COL_HI // LANES + h),
        pl.BlockSpec((None, 1, LANES), lambda n, h, d, j: ((layer * 2 + d) * HG_HEADS + h, 0, 0)),
    ]
    args = [y, y, y, lbv]
    if has_init:
        in_specs.append(pl.BlockSpec((None, None, None, None, HG_DK, HG_DK),
                                     lambda n, h, d, j: (n, layer, d, h, 0, 0)))
        args.append(state0)
    out_specs = [pl.BlockSpec((None, HG_CHUNK, LANES), lambda n, h, d, j: (d, n * nc + chunk(d, j), h))]
    out_shape = [jax.ShapeDtypeStruct((2, nb * seq, HG_HEADS * HG_DK), F32)]
    if emit_state:
        out_specs.append(pl.BlockSpec((None, None, None, HG_DK, HG_DK), lambda n, h, d, j: (n, d, h, 0, 0)))
        out_shape.append(jax.ShapeDtypeStruct((nb, 2, HG_HEADS, HG_DK, HG_DK), F32))
    res = pl.pallas_call(
        functools.partial(_hgrn_kernel, nc=nc, has_init=has_init, emit_state=emit_state),
        grid=(nb, HG_HEADS, 2, nc),
        in_specs=in_specs,
        out_specs=out_specs,
        out_shape=out_shape,
        scratch_shapes=[pltpu.VMEM((HG_DK, HG_DK), F32)],
        compiler_params=_params(("parallel", "parallel", "parallel", "arbitrary")),
        name="hgrn",
    )(*args)
    return (res[0], res[1]) if emit_state else (res[0], None)


def _merge_kernel(oa_ref, ow_ref, of_ref, ob_ref, hg_ref, ga_ref, gb_ref, gc_ref, x_ref, gate_ref,
                  hgn_ref, wa_ref, wh_ref, ww_ref, wo_ref, o_ref):
    o = of_ref[...] + ob_ref[...]
    hg = hg_ref[...]
    hgn = hgn_ref[...]
    parts = []
    for h in range(HG_HEADS):
        sl = slice(h * HG_DK, (h + 1) * HG_DK)
        g = hg[:, sl]
        parts.append(_rms_rows(o[:, sl], hgn) * (g * _sigmoid(g)))
    o_hg = jnp.concatenate(parts, axis=-1).astype(BF16)
    merged = (_sigmoid(ga_ref[...]) * _dot(oa_ref[...].astype(BF16), wa_ref[...])
              + _sigmoid(gb_ref[...]) * _dot(o_hg, wh_ref[...])
              + _sigmoid(gc_ref[...]) * _dot(ow_ref[...].astype(BF16), ww_ref[...]))
    yv = _dot(merged.astype(BF16), wo_ref[...])
    o_ref[...] = x_ref[...] + gate_ref[...] * yv


def _merge_call(x, y, o_att, o_win, o_hg, modv, hg_norm_g, w_branch_b, w_out_b, layer, row_fn):
    n = x.shape[0]
    tm = min(MERGE_TM, n)
    row = lambda i: (i, 0)
    in_specs = [
        pl.BlockSpec((tm, BRANCH_W), row),
        pl.BlockSpec((tm, BRANCH_W), row),
        pl.BlockSpec((None, tm, BRANCH_W), lambda i: (0, i, 0)),
        pl.BlockSpec((None, tm, BRANCH_W), lambda i: (1, i, 0)),
        pl.BlockSpec((tm, BRANCH_W), lambda i: (i, COL_HG // BRANCH_W)),
        pl.BlockSpec((tm, D_MODEL), lambda i: (i, COL_GATES // D_MODEL)),
        pl.BlockSpec((tm, D_MODEL), lambda i: (i, COL_GATES // D_MODEL + 1)),
        pl.BlockSpec((tm, D_MODEL), lambda i: (i, COL_GATES // D_MODEL + 2)),
        pl.BlockSpec((tm, D_MODEL), row),
        _mod_spec(layer, 5, lambda i: row_fn(i)),
        pl.BlockSpec((1, HG_DK), lambda i: (0, 0)),
        pl.BlockSpec((None, None, BRANCH_W, D_MODEL), lambda i: (layer, 0, 0, 0)),
        pl.BlockSpec((None, None, BRANCH_W, D_MODEL), lambda i: (layer, 1, 0, 0)),
        pl.BlockSpec((None, None, BRANCH_W, D_MODEL), lambda i: (layer, 2, 0, 0)),
        pl.BlockSpec((None, D_MODEL, D_MODEL), lambda i: (layer, 0, 0)),
    ]
    return pl.pallas_call(
        _merge_kernel,
        grid=(n // tm,),
        in_specs=in_specs,
        out_specs=pl.BlockSpec((tm, D_MODEL), row),
        out_shape=jax.ShapeDtypeStruct((n, D_MODEL), F32),
        compiler_params=_params(("parallel",)),
        name="merge",
    )(o_att, o_win, o_hg, o_hg, y, y, y, y, x, modv, hg_norm_g[layer].reshape(1, HG_DK).astype(F32),
      w_branch_b, w_branch_b, w_branch_b, w_out_b)


def _rope_tables(n_lat):
    t = jnp.arange(n_lat, dtype=jnp.int32)
    row = (t // GRID_W).astype(F32)
    col = (t % GRID_W).astype(F32)
    axis_dim = HEAD_DIM // 2
    inv = ROPE_THETA ** (-jnp.arange(0, axis_dim, 2, dtype=F32) / axis_dim)
    ang_r = row[:, None] * inv
    ang_c = col[:, None] * inv
    cos64 = jnp.concatenate([jnp.cos(ang_r), jnp.cos(ang_r), jnp.cos(ang_c), jnp.cos(ang_c)], axis=-1)
    sin64 = jnp.concatenate([-jnp.sin(ang_r), jnp.sin(ang_r), -jnp.sin(ang_c), jnp.sin(ang_c)], axis=-1)
    return jnp.tile(cos64, (1, 2)), jnp.tile(sin64, (1, 2))


def kernel(x_prompt, x_sample, cache_k_attn, cache_v_attn, cache_k_win, cache_v_win, state_hgrn, c, c_ctx,
           w_mod, b_mod, norm_g, w_ffn_in, w_ffn_out, w_in, qk_norm_g, lower_bounds, hg_norm_g, sink_logit,
           w_branch, w_out, final_norm_g):
    batch, seq, _ = x_prompt.shape
    dec_batch, dec_seq, _ = x_sample.shape
    depth = w_mod.shape[0]

    cond = jnp.zeros((COND_ROWS, D_MODEL), F32).at[0].set(c_ctx).at[1:1 + dec_batch].set(c)
    modv = _mod_call(cond, w_mod, b_mod).reshape(depth * COND_ROWS * N_MOD, 1, D_MODEL)
    lbv = _lb_call(lower_bounds).reshape(depth * 2 * HG_HEADS, 1, HG_DK)
    normv = norm_g.astype(F32).reshape(depth * 3, 1, D_MODEL)
    w_ffn_in_b = w_ffn_in.astype(BF16)
    w_ffn_out_b = w_ffn_out.astype(BF16)
    w_in_b = w_in.astype(BF16)
    w_branch_b = w_branch.astype(BF16)
    w_out_b = w_out.astype(BF16)
    rope = _rope_tables(dec_seq)

    def run(x, nb, s, latent):
        n = nb * s
        if latent:
            def tile_row(tm):
                return lambda i: 1 + i // (s // tm)
        else:
            def tile_row(tm):
                return lambda i: 0
        ffn_row = tile_row(min(FFN_TM, n))
        merge_row = tile_row(min(MERGE_TM, n))
        x = x.reshape(n, D_MODEL)
        ctx_out = []
        for l in range(depth):
            x = _ffn_call(x, modv, normv, w_ffn_in_b, w_ffn_out_b, l, 0, ffn_row)
            y = _inproj_call(x, modv, normv, w_in_b, qk_norm_g, l, ffn_row)
            if latent:
                o_att = _attn_call(y, nb, s, COL_AQ, COL_AK, COL_AV, cache_k=cache_k_attn,
                                   cache_v=cache_v_attn, layer=l, rope=rope)
                o_win = _attn_call(y, nb, s, COL_WQ, COL_WK, COL_WV, cache_k=cache_k_win,
                                   cache_v=cache_v_win, layer=l, rope=rope, sink=sink_logit, window=True)
                o_hg, _ = _hgrn_call(y, lbv, nb, s, l, state0=state_hgrn)
            else:
                o_att = _attn_call(y, nb, s, COL_AQ, COL_AK, COL_AV)
                o_win = _attn_call(y, nb, s, COL_WQ, COL_WK, COL_WV, layer=l, sink=sink_logit)
                o_hg, s_fin = _hgrn_call(y, lbv, nb, s, l, emit_state=True)
                kv = lambda col: y[:, col:col + LANES].reshape(nb, s, ATT_KV, HEAD_DIM)
                ctx_out.append((kv(COL_AK), kv(COL_AV), kv(COL_WK), kv(COL_WV), s_fin))
            x = _merge_call(x, y, o_att, o_win, o_hg, modv, hg_norm_g, w_branch_b, w_out_b, l, merge_row)
            x = _ffn_call(x, modv, normv, w_ffn_in_b, w_ffn_out_b, l, 1, ffn_row,
                          final_g=final_norm_g.astype(F32) if l == depth - 1 else None)
        return x.reshape(nb, s, D_MODEL), ctx_out

    y_prompt, ctx_out = run(x_prompt, batch, seq, False)
    y_sample, _ = run(x_sample, dec_batch, dec_seq, True)
    stack = lambda k: jnp.stack([cx[k] for cx in ctx_out], axis=1)
    return (y_prompt, y_sample, stack(0), stack(1), stack(2), stack(3), stack(4))
```

```python
import functools

import jax
import jax.numpy as jnp
from jax import lax
from jax.experimental import pallas as pl
from jax.experimental.pallas import tpu as pltpu

F32 = jnp.float32
BF16 = jnp.bfloat16

D_MODEL = 1024
GRID_W = 64
HEAD_DIM = 64
ATT_KV = 2
WINDOW = 128
HG_HEADS = 4
HG_DK = 128
HG_W = HG_HEADS * HG_DK
BRANCH_W = 512
D_FF = 2816
ROPE_THETA = 10000.0
EPS = 1e-6
NEG_INF = -1e30
TINY = 1e-30
N_MOD = 9
IN_W = 7168

LANES = 128
SUBLANES = 8

COND_ROWS = 16
DENSE_TM = 512
FFN_TF = 256
ATT_TQ = 256
HG_CHUNK = 128
HG_SUB = SUBLANES
VMEM_LIMIT = 56 * 1024 * 1024

COL_AQ, COL_AK, COL_AV = 0, 512, 640
COL_WQ, COL_WK, COL_WV = 768, 1280, 1408
COL_HQ, COL_HF, COL_HI, COL_HG = 1536, 2048, 3072, 3584
COL_GATES = 4096
Y_W = COL_GATES
GATES_W = IN_W - COL_GATES


def _params(sem):
    return pltpu.CompilerParams(dimension_semantics=sem, vmem_limit_bytes=VMEM_LIMIT)


def _resident(block_shape, index_map):
    return pl.BlockSpec(block_shape, index_map, pipeline_mode=pl.Buffered(1))


def _sigmoid(x):
    return 1.0 / (1.0 + jnp.exp(-x))


def _dot(a, b):
    return jnp.dot(a, b, preferred_element_type=F32)


def _dot_nt(a, b):
    return lax.dot_general(a, b, (((1,), (1,)), ((), ())), preferred_element_type=F32)


def _dot_tn(a, b):
    return lax.dot_general(a, b, (((0,), (0,)), ((), ())), preferred_element_type=F32)


def _rms_rows(x, gain):
    return x * lax.rsqrt(jnp.mean(x * x, axis=-1, keepdims=True) + EPS) * gain


def _mod_kernel(c_ref, w_ref, b_ref, o_ref):
    c = c_ref[...]
    h = (c * _sigmoid(c)).astype(BF16)
    o_ref[...] = _dot(h, w_ref[...].astype(BF16)) + b_ref[...]


def _mod_call(cond, w_mod, b_mod):
    depth, d, nm = w_mod.shape
    tn = nm // 8
    return pl.pallas_call(
        _mod_kernel,
        grid=(depth, nm // tn),
        in_specs=[pl.BlockSpec((COND_ROWS, d), lambda l, j: (0, 0)),
                  pl.BlockSpec((None, d, tn), lambda l, j: (l, 0, j)),
                  pl.BlockSpec((None, 1, tn), lambda l, j: (l, 0, j))],
        out_specs=pl.BlockSpec((None, COND_ROWS, tn), lambda l, j: (l, 0, j)),
        out_shape=jax.ShapeDtypeStruct((depth, COND_ROWS, nm), F32),
        compiler_params=_params(("parallel", "parallel")),
        name="mod",
    )(cond, w_mod, b_mod.reshape(depth, 1, nm))


def _lb_kernel(x_ref, o_ref):
    x = x_ref[...]
    m = jnp.max(x, axis=0, keepdims=True)
    e = jnp.exp(x - m)
    s = e / jnp.sum(e, axis=0, keepdims=True)
    acc = jnp.zeros_like(s[0:1])
    for l in range(x.shape[0]):
        acc = acc + s[l:l + 1]
        o_ref[l:l + 1, :] = acc - s[0:1]


def _lb_call(lower_bounds):
    depth = lower_bounds.shape[0]
    x = lower_bounds.reshape(depth, -1).astype(F32)
    return pl.pallas_call(_lb_kernel, out_shape=jax.ShapeDtypeStruct(x.shape, F32), name="lb")(x)


def _ffn_kernel(*refs, final):
    if final:
        x_ref, shift_ref, scale_ref, gate_ref, g_ref, wgu_ref, wd_ref, fg_ref, o_ref = refs
    else:
        x_ref, shift_ref, scale_ref, gate_ref, g_ref, wgu_ref, wd_ref, o_ref = refs
    x = x_ref[...]
    h = (_rms_rows(x, g_ref[...]) * (1.0 + scale_ref[...]) + shift_ref[...]).astype(BF16)
    for c in range(D_FF // FFN_TF):
        a = _dot(h, wgu_ref[:, c * FFN_TF:(c + 1) * FFN_TF])
        u = _dot(h, wgu_ref[:, D_FF + c * FFN_TF:D_FF + (c + 1) * FFN_TF])
        act = (a * _sigmoid(a) * u).astype(BF16)
        part = _dot(act, wd_ref[c * FFN_TF:(c + 1) * FFN_TF, :])
        if c == 0:
            o_ref[...] = part
        else:
            o_ref[...] += part
    out = x + 0.5 * gate_ref[...] * o_ref[...]
    if final:
        out = _rms_rows(out, fg_ref[...])
    o_ref[...] = out


def _mod_spec(layer, k, row_fn):
    return pl.BlockSpec((None, 1, D_MODEL), lambda i: ((layer * COND_ROWS + row_fn(i)) * N_MOD + k, 0, 0))


def _ffn_call(x, modv, normv, w_in_b, w_out_b, layer, which, row_fn, final_g=None):
    n = x.shape[0]
    tm = min(DENSE_TM, n)
    k0 = 0 if which == 0 else 6
    nslot = 0 if which == 0 else 2
    final = final_g is not None
    in_specs = [
        pl.BlockSpec((tm, D_MODEL), lambda i: (i, 0)),
        _mod_spec(layer, k0, row_fn), _mod_spec(layer, k0 + 1, row_fn), _mod_spec(layer, k0 + 2, row_fn),
        pl.BlockSpec((None, 1, D_MODEL), lambda i: (layer * 3 + nslot, 0, 0)),
        _resident((None, None, D_MODEL, 2 * D_FF), lambda i: (layer, which, 0, 0)),
        _resident((None, None, D_FF, D_MODEL), lambda i: (layer, which, 0, 0)),
    ]
    args = [x, modv, modv, modv, normv, w_in_b, w_out_b]
    if final:
        in_specs.append(pl.BlockSpec((1, D_MODEL), lambda i: (0, 0)))
        args.append(final_g.reshape(1, D_MODEL))
    return pl.pallas_call(
        functools.partial(_ffn_kernel, final=final),
        grid=(n // tm,),
        in_specs=in_specs,
        out_specs=pl.BlockSpec((tm, D_MODEL), lambda i: (i, 0)),
        out_shape=jax.ShapeDtypeStruct((n, D_MODEL), F32),
        compiler_params=_params(("parallel",)),
        name="ffn",
    )(*args)


def _head_rms(y, gain):
    lane = lax.broadcasted_iota(jnp.int32, (1, LANES), 1)
    lo = lane < HEAD_DIM
    outs = []
    for c in range(y.shape[1] // LANES):
        blk = y[:, c * LANES:(c + 1) * LANES]
        sq = blk * blk
        s_lo = jnp.sum(jnp.where(lo, sq, 0.0), axis=-1, keepdims=True)
        s_hi = jnp.sum(jnp.where(lo, 0.0, sq), axis=-1, keepdims=True)
        ms = jnp.where(lo, s_lo, s_hi) * (1.0 / HEAD_DIM)
        outs.append(blk * lax.rsqrt(ms + EPS) * gain[:, c * LANES:(c + 1) * LANES])
    return outs[0] if len(outs) == 1 else jnp.concatenate(outs, axis=-1)


def _inproj_kernel(x_ref, shift_ref, scale_ref, g_ref, w_ref, qg_ref, kg_ref, lb_ref, y_ref, gates_ref):
    h = (_rms_rows(x_ref[...], g_ref[...]) * (1.0 + scale_ref[...]) + shift_ref[...]).astype(BF16)

    def proj(c0, c1):
        return _dot(h, w_ref[:, c0:c1])

    y_ref[:, COL_AQ:COL_AK] = _head_rms(proj(COL_AQ, COL_AK), qg_ref[...])
    t = proj(COL_AK, COL_WQ)
    y_ref[:, COL_AK:COL_AV] = _head_rms(t[:, :LANES], kg_ref[...])
    y_ref[:, COL_AV:COL_WQ] = t[:, LANES:]
    y_ref[:, COL_WQ:COL_HQ] = proj(COL_WQ, COL_HQ)
    t = proj(COL_HQ, COL_HF)
    y_ref[:, COL_HQ:COL_HF] = t * _sigmoid(t)
    for dirn in range(2):
        c0 = COL_HF + dirn * HG_W
        lb = lb_ref[:, dirn * HG_W:(dirn + 1) * HG_W]
        y_ref[:, c0:c0 + HG_W] = lb + (1.0 - lb) * _sigmoid(proj(c0, c0 + HG_W))
    y_ref[:, COL_HI:COL_GATES] = proj(COL_HI, COL_GATES)
    for k in range(GATES_W // D_MODEL):
        c0 = COL_GATES + k * D_MODEL
        gates_ref[:, k * D_MODEL:(k + 1) * D_MODEL] = _sigmoid(proj(c0, c0 + D_MODEL)).astype(BF16)


def _inproj_call(x, modv, normv, w_in_b, qk_norm_g, lb_all, layer, row_fn):
    n = x.shape[0]
    tm = min(DENSE_TM, n)
    qg = jnp.tile(qk_norm_g[layer, 0].astype(F32), (COL_AK - COL_AQ) // HEAD_DIM).reshape(1, COL_AK - COL_AQ)
    kg = jnp.tile(qk_norm_g[layer, 1].astype(F32), LANES // HEAD_DIM).reshape(1, LANES)
    return pl.pallas_call(
        _inproj_kernel,
        grid=(n // tm,),
        in_specs=[
            pl.BlockSpec((tm, D_MODEL), lambda i: (i, 0)),
            _mod_spec(layer, 3, row_fn), _mod_spec(layer, 4, row_fn),
            pl.BlockSpec((None, 1, D_MODEL), lambda i: (layer * 3 + 1, 0, 0)),
            _resident((None, D_MODEL, IN_W), lambda i: (layer, 0, 0)),
            pl.BlockSpec((1, COL_AK - COL_AQ), lambda i: (0, 0)),
            pl.BlockSpec((1, LANES), lambda i: (0, 0)),
            pl.BlockSpec((None, 1, 2 * HG_W), lambda i: (layer, 0, 0)),
        ],
        out_specs=[pl.BlockSpec((tm, Y_W), lambda i: (i, 0)),
                   pl.BlockSpec((tm, GATES_W), lambda i: (i, 0))],
        out_shape=[jax.ShapeDtypeStruct((n, Y_W), F32), jax.ShapeDtypeStruct((n, GATES_W), BF16)],
        compiler_params=_params(("parallel",)),
        name="inproj",
    )(x, modv, modv, normv, w_in_b, qg, kg, lb_all)


def _rope(x, cos, sin_signed):
    lane = lax.broadcasted_iota(jnp.int32, (1, LANES), 1)
    first = (lane & 31) < 16
    swapped = jnp.where(first, pltpu.roll(x, LANES - 16, 1), pltpu.roll(x, 16, 1))
    return x * cos + swapped * sin_signed


def _attn_kernel(*refs, tq, seq, past, use_rope, use_sink, window, sink_off):
    it = iter(refs)
    sink_ref = next(it) if use_sink else None
    q_refs = (next(it), next(it))
    k_ref, v_ref = next(it), next(it)
    ck_ref = cv_ref = cos_ref = sin_ref = None
    if past:
        ck_ref, cv_ref = next(it), next(it)
    if use_rope:
        cos_ref, sin_ref = next(it), next(it)
    o_ref = next(it)
    kscr, vscr = next(it), next(it)

    qi = pl.program_id(1)
    lane = lax.broadcasted_iota(jnp.int32, (1, LANES), 1)
    lo = lane < HEAD_DIM
    rows = min(seq, 256)

    def put(scr, x, r0):
        xr = pltpu.roll(x, HEAD_DIM, 1)
        sl = pl.ds(r0, x.shape[0])
        scr[0, sl, :] = jnp.where(lo, x, 0.0).astype(BF16)
        scr[1, sl, :] = jnp.where(lo, 0.0, xr).astype(BF16)
        scr[2, sl, :] = jnp.where(lo, xr, 0.0).astype(BF16)
        scr[3, sl, :] = jnp.where(lo, 0.0, x).astype(BF16)

    @pl.when(qi == 0)
    def _build():
        if past:
            put(kscr, ck_ref[...], 0)
            put(vscr, cv_ref[...], 0)

        def body(c, carry):
            r = pl.multiple_of(c * rows, rows)
            k = k_ref[pl.ds(r, rows), :]
            if use_rope:
                k = _rope(k, cos_ref[pl.ds(r, rows), :], sin_ref[pl.ds(r, rows), :])
            put(kscr, k, past + r)
            put(vscr, v_ref[pl.ds(r, rows), :], past + r)
            return carry

        lax.fori_loop(0, seq // rows, body, 0)

    q0 = pl.multiple_of(qi * tq, tq)
    if use_rope:
        cosq = cos_ref[pl.ds(q0, tq), :]
        sinq = sin_ref[pl.ds(q0, tq), :]
    if window:
        span = tq + 2 * WINDOW
        start = pl.multiple_of(jnp.clip(q0 - WINDOW, 0, seq - span), WINDOW)
        t_pos = q0 + lax.broadcasted_iota(jnp.int32, (tq, 1), 0)
        s_pos = start + lax.broadcasted_iota(jnp.int32, (1, span), 1)
        band_ok = jnp.abs(t_pos - s_pos) <= WINDOW

    for g in range(ATT_KV):
        for p in range(2):
            c0 = p * LANES
            q = q_refs[g][:, c0:c0 + LANES]
            if use_rope:
                q = _rope(q, cosq, sinq)
            qb = (q * (HEAD_DIM ** -0.5)).astype(BF16)
            o_pair = None
            for half in range(2):
                idx = g * 2 + half
                if window:
                    segs = [(_dot_nt(qb, kscr[idx, 0:past, :]), vscr[idx, 0:past, :])]
                    s_band = _dot_nt(qb, kscr[idx, pl.ds(past + start, span), :])
                    segs.append((jnp.where(band_ok, s_band, NEG_INF), vscr[idx, pl.ds(past + start, span), :]))
                else:
                    segs = [(_dot_nt(qb, kscr[idx]), vscr[idx])]
                m = None
                for s, _ in segs:
                    ms = jnp.max(s, axis=-1, keepdims=True)
                    m = ms if m is None else jnp.maximum(m, ms)
                if use_sink:
                    sk = sink_ref[sink_off + g * 4 + p * 2 + half]
                    m = jnp.maximum(m, sk)
                denom = jnp.exp(sk - m) if use_sink else None
                acc = None
                for s, v in segs:
                    e = jnp.exp(s - m)
                    es = jnp.sum(e, axis=-1, keepdims=True)
                    denom = es if denom is None else denom + es
                    pv = _dot(e.astype(BF16), v)
                    acc = pv if acc is None else acc + pv
                acc = acc * (1.0 / denom)
                o_pair = acc if o_pair is None else o_pair + acc
            o_ref[:, g * 256 + c0:g * 256 + c0 + LANES] = o_pair.astype(o_ref.dtype)


def _attn_call(y, nb, seq, qcol, kcol, vcol, *, cache_k=None, cache_v=None, layer=0, rope=None,
               sink=None, window=False):
    tq = min(ATT_TQ, seq)
    nq = seq // tq
    past = 0 if cache_k is None else cache_k.shape[2]
    use_rope = rope is not None
    use_sink = sink is not None
    in_specs, args = [], []
    if use_sink:
        in_specs.append(pl.BlockSpec(memory_space=pltpu.SMEM))
        args.append(sink.reshape(-1).astype(F32))
    for g in range(ATT_KV):
        in_specs.append(pl.BlockSpec((tq, 256), lambda b, qi, g=g: (b * nq + qi, qcol // 256 + g)))
        args.append(y)
    in_specs.append(pl.BlockSpec((seq, LANES), lambda b, qi: (b, kcol // LANES)))
    in_specs.append(pl.BlockSpec((seq, LANES), lambda b, qi: (b, vcol // LANES)))
    args += [y, y]
    if past:
        cshape = cache_k.shape[:3] + (LANES,)
        in_specs += [pl.BlockSpec((None, None, past, LANES), lambda b, qi: (b, layer, 0, 0))] * 2
        args += [cache_k.reshape(cshape), cache_v.reshape(cshape)]
    if use_rope:
        in_specs += [pl.BlockSpec((seq, LANES), lambda b, qi: (0, 0))] * 2
        args += list(rope)
    kern = functools.partial(_attn_kernel, tq=tq, seq=seq, past=past, use_rope=use_rope,
                             use_sink=use_sink, window=window, sink_off=layer * 8)
    return pl.pallas_call(
        kern,
        grid=(nb, nq),
        in_specs=in_specs,
        out_specs=pl.BlockSpec((tq, BRANCH_W), lambda b, qi: (b * nq + qi, 0)),
        out_shape=jax.ShapeDtypeStruct((nb * seq, BRANCH_W), BF16),
        scratch_shapes=[pltpu.VMEM((4, past + seq, LANES), BF16)] * 2,
        compiler_params=_params(("parallel", "arbitrary")),
        name="win" if window or use_sink else "att",
    )(*args)


def _hgrn_unit(q, k, b, v, st, fwd):
    C = HG_CHUNK
    G = C // HG_SUB
    vb = v.astype(BF16)

    b3 = b.reshape(G, HG_SUB, HG_DK)
    q3 = q.reshape(G, HG_SUB, HG_DK)
    k3 = k.reshape(G, HG_SUB, HG_DK)
    v3 = v.reshape(G, HG_SUB, HG_DK)
    r8 = lax.broadcasted_iota(jnp.int32, (1, HG_SUB, 1), 1)
    out3 = None
    for jj in range(HG_SUB):
        ok = (r8 >= jj) if fwd else (r8 <= jj)
        dec = jnp.exp2(jnp.where(ok, b3 - b3[:, jj:jj + 1, :], NEG_INF))
        score = jnp.sum(q3 * (k3[:, jj:jj + 1, :] * dec), axis=-1, keepdims=True)
        term = score * v3[:, jj:jj + 1, :]
        out3 = term if out3 is None else out3 + term
    out = out3.reshape(C, HG_DK)

    ti = lax.broadcasted_iota(jnp.int32, (C, C), 0)
    si = lax.broadcasted_iota(jnp.int32, (C, C), 1)
    tcol = lax.broadcasted_iota(jnp.int32, (C, 1), 0)
    a_mat = None
    hh = HG_SUB
    while hh < C:
        grp = 2 * hh
        later = (tcol & (grp - 1)) >= hh
        is_q = later if fwd else jnp.logical_not(later)
        bg = b.reshape(C // grp, grp, HG_DK)
        bnd = bg[:, hh - 1:hh, :] if fwd else bg[:, hh:hh + 1, :]
        bnd = jnp.broadcast_to(bnd, (C // grp, grp, HG_DK)).reshape(C, HG_DK)
        ex = jnp.exp2(jnp.where(is_q, b - bnd, bnd - b))
        qt = jnp.where(is_q, q * ex, 0.0).astype(BF16)
        kt = jnp.where(is_q, 0.0, k * ex).astype(BF16)
        sc = _dot_nt(qt, kt)
        if grp < C:
            shift = grp.bit_length() - 1
            sc = jnp.where((ti >> shift) == (si >> shift), sc, 0.0)
        a_mat = sc if a_mat is None else a_mat + sc
        hh = grp
    out = out + _dot(a_mat.astype(BF16), vb)

    out = out + _dot_nt((q * jnp.exp2(b)).astype(BF16), st.astype(BF16))
    blast = b[C - 1:C, :] if fwd else b[0:1, :]
    kh = (k * jnp.exp2(blast - b)).astype(BF16)
    st_new = st * jnp.exp2(blast) + _dot_tn(vb, kh)
    return out, st_new


def _hgrn_kernel(*refs, nc, has_init, emit_state):
    it = iter(refs)
    io = [(next(it), next(it), next(it)) for _ in range(2)]
    s0_ref = next(it) if has_init else None
    o_refs = (next(it), next(it))
    sfin_ref = next(it) if emit_state else None
    st_scr = next(it)

    j = pl.program_id(1)
    C = HG_CHUNK

    @pl.when(j == 0)
    def _():
        for dirn in range(2):
            for h in range(HG_HEADS):
                if has_init:
                    st_scr[dirn, h] = s0_ref[dirn, h].T
                else:
                    st_scr[dirn, h] = jnp.zeros((HG_DK, HG_DK), F32)

    ti = lax.broadcasted_iota(jnp.int32, (C, C), 0)
    si = lax.broadcasted_iota(jnp.int32, (C, C), 1)
    for dirn in range(2):
        fwd = dirn == 0
        q_ref, f_ref, v_ref = io[dirn]
        f = f_ref[...]
        l2 = jnp.log2(jnp.maximum(f, TINY))
        tri = jnp.where((ti >= si) if fwd else (ti <= si), 1.0, 0.0).astype(BF16)
        l_hi = l2.astype(BF16)
        r1 = l2 - l_hi.astype(F32)
        l_mid = r1.astype(BF16)
        l_lo = (r1 - l_mid.astype(F32)).astype(BF16)
        b_all = _dot(tri, l_hi) + _dot(tri, l_mid) + _dot(tri, l_lo)
        for h in range(HG_HEADS):
            sl = slice(h * HG_DK, (h + 1) * HG_DK)
            out, st_new = _hgrn_unit(q_ref[:, sl], 1.0 - f[:, sl], b_all[:, sl], v_ref[:, sl],
                                     st_scr[dirn, h], fwd)
            o_refs[dirn][:, sl] = out
            st_scr[dirn, h] = st_new
            if emit_state:
                @pl.when(j == nc - 1)
                def _():
                    sfin_ref[dirn, h] = st_new.T


def _hgrn_call(y, nb, seq, layer, state0=None, emit_state=False):
    nc = seq // HG_CHUNK
    has_init = state0 is not None
    in_specs, args = [], []
    for dirn in range(2):
        rowblk = (lambda n, j: n * nc + j) if dirn == 0 else (lambda n, j: n * nc + nc - 1 - j)
        for col in (COL_HQ, COL_HF + dirn * HG_W, COL_HI):
            in_specs.append(pl.BlockSpec((HG_CHUNK, HG_W), lambda n, j, rowblk=rowblk, col=col: (rowblk(n, j), col // HG_W)))
            args.append(y)
    if has_init:
        in_specs.append(pl.BlockSpec((None, None, 2, HG_HEADS, HG_DK, HG_DK), lambda n, j: (n, layer, 0, 0, 0, 0)))
        args.append(state0)
    out_specs = [pl.BlockSpec((HG_CHUNK, HG_W), lambda n, j: (n * nc + j, 0)),
                 pl.BlockSpec((HG_CHUNK, HG_W), lambda n, j: (n * nc + nc - 1 - j, 0))]
    out_shape = [jax.ShapeDtypeStruct((nb * seq, HG_W), F32)] * 2
    if emit_state:
        out_specs.append(pl.BlockSpec((None, 2, HG_HEADS, HG_DK, HG_DK), lambda n, j: (n, 0, 0, 0, 0)))
        out_shape.append(jax.ShapeDtypeStruct((nb, 2, HG_HEADS, HG_DK, HG_DK), F32))
    res = pl.pallas_call(
        functools.partial(_hgrn_kernel, nc=nc, has_init=has_init, emit_state=emit_state),
        grid=(nb, nc),
        in_specs=in_specs,
        out_specs=out_specs,
        out_shape=out_shape,
        scratch_shapes=[pltpu.VMEM((2, HG_HEADS, HG_DK, HG_DK), F32)],
        compiler_params=_params(("parallel", "arbitrary")),
        name="hgrn",
    )(*args)
    return (res[0], res[1], res[2]) if emit_state else (res[0], res[1], None)


def _merge_kernel(oa_ref, ow_ref, of_ref, ob_ref, hg_ref, gates_ref, x_ref, gate_ref,
                  hgn_ref, wb_ref, wo_ref, o_ref):
    o = of_ref[...] + ob_ref[...]
    hg = hg_ref[...]
    hgn = hgn_ref[...]
    parts = []
    for h in range(HG_HEADS):
        sl = slice(h * HG_DK, (h + 1) * HG_DK)
        g = hg[:, sl]
        parts.append(_rms_rows(o[:, sl], hgn) * (g * _sigmoid(g)))
    o_hg = jnp.concatenate(parts, axis=-1).astype(BF16)
    branch = (oa_ref[...], o_hg, ow_ref[...])
    merged = None
    for k in range(3):
        term = gates_ref[:, k * D_MODEL:(k + 1) * D_MODEL].astype(F32) * _dot(branch[k], wb_ref[k])
        merged = term if merged is None else merged + term
    yv = _dot(merged.astype(BF16), wo_ref[...])
    o_ref[...] = x_ref[...] + gate_ref[...] * yv


def _merge_call(x, y, gates, o_att, o_win, o_f, o_b, modv, hg_norm_g, w_branch_b, w_out_b, layer, row_fn):
    n = x.shape[0]
    tm = min(DENSE_TM, n)
    row = lambda i: (i, 0)
    in_specs = [
        pl.BlockSpec((tm, BRANCH_W), row),
        pl.BlockSpec((tm, BRANCH_W), row),
        pl.BlockSpec((tm, HG_W), row),
        pl.BlockSpec((tm, HG_W), row),
        pl.BlockSpec((tm, HG_W), lambda i: (i, COL_HG // HG_W)),
        pl.BlockSpec((tm, GATES_W), row),
        pl.BlockSpec((tm, D_MODEL), row),
        _mod_spec(layer, 5, row_fn),
        pl.BlockSpec((1, HG_DK), lambda i: (0, 0)),
        _resident((None, 3, BRANCH_W, D_MODEL), lambda i: (layer, 0, 0, 0)),
        _resident((None, D_MODEL, D_MODEL), lambda i: (layer, 0, 0)),
    ]
    return pl.pallas_call(
        _merge_kernel,
        grid=(n // tm,),
        in_specs=in_specs,
        out_specs=pl.BlockSpec((tm, D_MODEL), row),
        out_shape=jax.ShapeDtypeStruct((n, D_MODEL), F32),
        compiler_params=_params(("parallel",)),
        name="merge",
    )(o_att, o_win, o_f, o_b, y, gates, x, modv, hg_norm_g[layer].reshape(1, HG_DK).astype(F32),
      w_branch_b, w_out_b)


def _rope_tables(n_lat):
    t = jnp.arange(n_lat, dtype=jnp.int32)
    row = (t // GRID_W).astype(F32)
    col = (t % GRID_W).astype(F32)
    axis_dim = HEAD_DIM // 2
    inv = ROPE_THETA ** (-jnp.arange(0, axis_dim, 2, dtype=F32) / axis_dim)
    ang_r = row[:, None] * inv
    ang_c = col[:, None] * inv
    cos64 = jnp.concatenate([jnp.cos(ang_r), jnp.cos(ang_r), jnp.cos(ang_c), jnp.cos(ang_c)], axis=-1)
    sin64 = jnp.concatenate([-jnp.sin(ang_r), jnp.sin(ang_r), -jnp.sin(ang_c), jnp.sin(ang_c)], axis=-1)
    return jnp.tile(cos64, (1, 2)), jnp.tile(sin64, (1, 2))


def kernel(x_prompt, x_sample, cache_k_attn, cache_v_attn, cache_k_win, cache_v_win, state_hgrn, c, c_ctx,
           w_mod, b_mod, norm_g, w_ffn_in, w_ffn_out, w_in, qk_norm_g, lower_bounds, hg_norm_g, sink_logit,
           w_branch, w_out, final_norm_g):
    batch, seq, _ = x_prompt.shape
    dec_batch, dec_seq, _ = x_sample.shape
    depth = w_mod.shape[0]

    cond = jnp.zeros((COND_ROWS, D_MODEL), F32).at[0].set(c_ctx).at[1:1 + dec_batch].set(c)
    modv = _mod_call(cond, w_mod, b_mod).reshape(depth * COND_ROWS * N_MOD, 1, D_MODEL)
    lb_all = _lb_call(lower_bounds).reshape(depth, 1, 2 * HG_W)
    normv = norm_g.astype(F32).reshape(depth * 3, 1, D_MODEL)
    w_ffn_in_b = w_ffn_in.astype(BF16)
    w_ffn_out_b = w_ffn_out.astype(BF16)
    w_in_b = w_in.astype(BF16)
    w_branch_b = w_branch.astype(BF16)
    w_out_b = w_out.astype(BF16)
    rope = _rope_tables(dec_seq)

    def run(x, nb, s, latent):
        n = nb * s
        tm = min(DENSE_TM, n)
        row_fn = (lambda i: 1 + i // (s // tm)) if latent else (lambda i: 0)
        x = x.reshape(n, D_MODEL)
        ctx_out = []
        for l in range(depth):
            x = _ffn_call(x, modv, normv, w_ffn_in_b, w_ffn_out_b, l, 0, row_fn)
            y, gates = _inproj_call(x, modv, normv, w_in_b, qk_norm_g, lb_all, l, row_fn)
            if latent:
                o_att = _attn_call(y, nb, s, COL_AQ, COL_AK, COL_AV, cache_k=cache_k_attn,
                                   cache_v=cache_v_attn, layer=l, rope=rope)
                o_win = _attn_call(y, nb, s, COL_WQ, COL_WK, COL_WV, cache_k=cache_k_win,
                                   cache_v=cache_v_win, layer=l, rope=rope, sink=sink_logit, window=True)
                o_f, o_b, _ = _hgrn_call(y, nb, s, l, state0=state_hgrn)
            else:
                o_att = _attn_call(y, nb, s, COL_AQ, COL_AK, COL_AV)
                o_win = _attn_call(y, nb, s, COL_WQ, COL_WK, COL_WV, layer=l, sink=sink_logit)
                o_f, o_b, s_fin = _hgrn_call(y, nb, s, l, emit_state=True)
                kv = lambda col: y[:, col:col + LANES].reshape(nb, s, ATT_KV, HEAD_DIM)
                ctx_out.append((kv(COL_AK), kv(COL_AV), kv(COL_WK), kv(COL_WV), s_fin))
            x = _merge_call(x, y, gates, o_att, o_win, o_f, o_b, modv, hg_norm_g, w_branch_b, w_out_b, l, row_fn)
            x = _ffn_call(x, modv, normv, w_ffn_in_b, w_ffn_out_b, l, 1, row_fn,
                          final_g=final_norm_g.astype(F32) if l == depth - 1 else None)
        return x.reshape(nb, s, D_MODEL), ctx_out

    y_prompt, ctx_out = run(x_prompt, batch, seq, False)
    y_sample, _ = run(x_sample, dec_batch, dec_seq, True)
    stack = lambda k: jnp.stack([cx[k] for cx in ctx_out], axis=1)
    return (y_prompt, y_sample, stack(0), stack(1), stack(2), stack(3), stack(4))
```

```python
import functools

import jax
import jax.numpy as jnp
from jax import lax
from jax.experimental import pallas as pl
from jax.experimental.pallas import tpu as pltpu

F32 = jnp.float32
BF16 = jnp.bfloat16

D_MODEL = 1024
GRID_W = 64
HEAD_DIM = 64
ATT_KV = 2
WINDOW = 128
HG_HEADS = 4
HG_DK = 128
HG_W = HG_HEADS * HG_DK
BRANCH_W = 512
D_FF = 2816
ROPE_THETA = 10000.0
EPS = 1e-6
NEG_INF = -1e30
TINY = 1e-30
LOG2E = 1.4426950408889634
N_MOD = 9
IN_W = 7168

LANES = 128
SUBLANES = 8

COND_ROWS = 16
DENSE_TM = 512
FFN_TF = 256
ATT_TQ = 128
HG_CHUNK = 128
HG_SUB = SUBLANES
VMEM_LIMIT = 56 * 1024 * 1024

COL_AQ, COL_AK, COL_AV = 0, 512, 640
COL_WQ, COL_WK, COL_WV = 768, 1280, 1408
COL_HQ, COL_HF, COL_HI, COL_HG = 1536, 2048, 3072, 3584
COL_GATES = 4096
Y_W = COL_GATES
GATES_W = IN_W - COL_GATES


def _params(sem):
    return pltpu.CompilerParams(dimension_semantics=sem, vmem_limit_bytes=VMEM_LIMIT)


def _resident(block_shape, index_map):
    return pl.BlockSpec(block_shape, index_map, pipeline_mode=pl.Buffered(1))


def _sigmoid(x):
    return 1.0 / (1.0 + jnp.exp(-x))


def _dot(a, b):
    return jnp.dot(a, b, preferred_element_type=F32)


def _dot_nt(a, b):
    return lax.dot_general(a, b, (((1,), (1,)), ((), ())), preferred_element_type=F32)


def _dot_tn(a, b):
    return lax.dot_general(a, b, (((0,), (0,)), ((), ())), preferred_element_type=F32)


def _rms_rows(x, gain):
    return x * lax.rsqrt(jnp.mean(x * x, axis=-1, keepdims=True) + EPS) * gain


def _mod_kernel(c_ref, w_ref, b_ref, o_ref):
    c = c_ref[...]
    h = (c * _sigmoid(c)).astype(BF16)
    o_ref[...] = _dot(h, w_ref[...].astype(BF16)) + b_ref[...]


def _mod_call(cond, w_mod, b_mod):
    depth, d, nm = w_mod.shape
    tn = nm // 8
    return pl.pallas_call(
        _mod_kernel,
        grid=(depth, nm // tn),
        in_specs=[pl.BlockSpec((COND_ROWS, d), lambda l, j: (0, 0)),
                  pl.BlockSpec((None, d, tn), lambda l, j: (l, 0, j)),
                  pl.BlockSpec((None, 1, tn), lambda l, j: (l, 0, j))],
        out_specs=pl.BlockSpec((None, COND_ROWS, tn), lambda l, j: (l, 0, j)),
        out_shape=jax.ShapeDtypeStruct((depth, COND_ROWS, nm), F32),
        compiler_params=_params(("parallel", "parallel")),
        name="mod",
    )(cond, w_mod, b_mod.reshape(depth, 1, nm))


def _lb_kernel(x_ref, o_ref):
    x = x_ref[...]
    m = jnp.max(x, axis=0, keepdims=True)
    e = jnp.exp(x - m)
    s = e / jnp.sum(e, axis=0, keepdims=True)
    acc = jnp.zeros_like(s[0:1])
    for l in range(x.shape[0]):
        acc = acc + s[l:l + 1]
        o_ref[l:l + 1, :] = acc - s[0:1]


def _lb_call(lower_bounds):
    depth = lower_bounds.shape[0]
    x = lower_bounds.reshape(depth, -1).astype(F32)
    return pl.pallas_call(_lb_kernel, out_shape=jax.ShapeDtypeStruct(x.shape, F32), name="lb")(x)


def _ffn_kernel(*refs, final):
    if final:
        x_ref, shift_ref, scale_ref, gate_ref, g_ref, wgu_ref, wd_ref, fg_ref, o_ref = refs
    else:
        x_ref, shift_ref, scale_ref, gate_ref, g_ref, wgu_ref, wd_ref, o_ref = refs
    x = x_ref[...]
    h = (_rms_rows(x, g_ref[...]) * (1.0 + scale_ref[...]) + shift_ref[...]).astype(BF16)
    for c in range(D_FF // FFN_TF):
        a = _dot(h, wgu_ref[:, c * FFN_TF:(c + 1) * FFN_TF])
        u = _dot(h, wgu_ref[:, D_FF + c * FFN_TF:D_FF + (c + 1) * FFN_TF])
        act = (a * _sigmoid(a) * u).astype(BF16)
        part = _dot(act, wd_ref[c * FFN_TF:(c + 1) * FFN_TF, :])
        if c == 0:
            o_ref[...] = part
        else:
            o_ref[...] += part
    out = x + 0.5 * gate_ref[...] * o_ref[...]
    if final:
        out = _rms_rows(out, fg_ref[...])
    o_ref[...] = out


def _mod_spec(layer, k, row_fn):
    return pl.BlockSpec((None, 1, D_MODEL), lambda i: ((layer * COND_ROWS + row_fn(i)) * N_MOD + k, 0, 0))


def _ffn_call(x, modv, normv, w_in_b, w_out_b, layer, which, row_fn, final_g=None):
    n = x.shape[0]
    tm = min(DENSE_TM, n)
    k0 = 0 if which == 0 else 6
    nslot = 0 if which == 0 else 2
    final = final_g is not None
    in_specs = [
        pl.BlockSpec((tm, D_MODEL), lambda i: (i, 0)),
        _mod_spec(layer, k0, row_fn), _mod_spec(layer, k0 + 1, row_fn), _mod_spec(layer, k0 + 2, row_fn),
        pl.BlockSpec((None, 1, D_MODEL), lambda i: (layer * 3 + nslot, 0, 0)),
        _resident((None, None, D_MODEL, 2 * D_FF), lambda i: (layer, which, 0, 0)),
        _resident((None, None, D_FF, D_MODEL), lambda i: (layer, which, 0, 0)),
    ]
    args = [x, modv, modv, modv, normv, w_in_b, w_out_b]
    if final:
        in_specs.append(pl.BlockSpec((1, D_MODEL), lambda i: (0, 0)))
        args.append(final_g.reshape(1, D_MODEL))
    return pl.pallas_call(
        functools.partial(_ffn_kernel, final=final),
        grid=(n // tm,),
        in_specs=in_specs,
        out_specs=pl.BlockSpec((tm, D_MODEL), lambda i: (i, 0)),
        out_shape=jax.ShapeDtypeStruct((n, D_MODEL), F32),
        compiler_params=_params(("parallel",)),
        name="ffn",
    )(*args)


def _head_rms(y, gain):
    lane = lax.broadcasted_iota(jnp.int32, (1, LANES), 1)
    lo = lane < HEAD_DIM
    outs = []
    for c in range(y.shape[1] // LANES):
        blk = y[:, c * LANES:(c + 1) * LANES]
        sq = blk * blk
        s_lo = jnp.sum(jnp.where(lo, sq, 0.0), axis=-1, keepdims=True)
        s_hi = jnp.sum(jnp.where(lo, 0.0, sq), axis=-1, keepdims=True)
        ms = jnp.where(lo, s_lo, s_hi) * (1.0 / HEAD_DIM)
        outs.append(blk * lax.rsqrt(ms + EPS) * gain[:, c * LANES:(c + 1) * LANES])
    return outs[0] if len(outs) == 1 else jnp.concatenate(outs, axis=-1)


def _inproj_kernel(x_ref, shift_ref, scale_ref, g_ref, w_ref, qg_ref, kg_ref, lb_ref, y_ref, gates_ref):
    h = (_rms_rows(x_ref[...], g_ref[...]) * (1.0 + scale_ref[...]) + shift_ref[...]).astype(BF16)

    def proj(c0, c1):
        return _dot(h, w_ref[:, c0:c1])

    y_ref[:, COL_AQ:COL_AK] = _head_rms(proj(COL_AQ, COL_AK), qg_ref[...])
    t = proj(COL_AK, COL_WQ)
    y_ref[:, COL_AK:COL_AV] = _head_rms(t[:, :LANES], kg_ref[...])
    y_ref[:, COL_AV:COL_WQ] = t[:, LANES:]
    y_ref[:, COL_WQ:COL_HQ] = proj(COL_WQ, COL_HQ)
    t = proj(COL_HQ, COL_HF)
    y_ref[:, COL_HQ:COL_HF] = t * _sigmoid(t)
    for dirn in range(2):
        c0 = COL_HF + dirn * HG_W
        lb = lb_ref[:, dirn * HG_W:(dirn + 1) * HG_W]
        y_ref[:, c0:c0 + HG_W] = lb + (1.0 - lb) * _sigmoid(proj(c0, c0 + HG_W))
    y_ref[:, COL_HI:COL_GATES] = proj(COL_HI, COL_GATES)
    for k in range(GATES_W // D_MODEL):
        c0 = COL_GATES + k * D_MODEL
        gates_ref[:, k * D_MODEL:(k + 1) * D_MODEL] = _sigmoid(proj(c0, c0 + D_MODEL)).astype(BF16)


def _inproj_call(x, modv, normv, w_in_b, qk_norm_g, lb_all, layer, row_fn):
    n = x.shape[0]
    tm = min(DENSE_TM, n)
    qg = jnp.tile(qk_norm_g[layer, 0].astype(F32), (COL_AK - COL_AQ) // HEAD_DIM).reshape(1, COL_AK - COL_AQ)
    kg = jnp.tile(qk_norm_g[layer, 1].astype(F32), LANES // HEAD_DIM).reshape(1, LANES)
    return pl.pallas_call(
        _inproj_kernel,
        grid=(n // tm,),
        in_specs=[
            pl.BlockSpec((tm, D_MODEL), lambda i: (i, 0)),
            _mod_spec(layer, 3, row_fn), _mod_spec(layer, 4, row_fn),
            pl.BlockSpec((None, 1, D_MODEL), lambda i: (layer * 3 + 1, 0, 0)),
            _resident((None, D_MODEL, IN_W), lambda i: (layer, 0, 0)),
            pl.BlockSpec((1, COL_AK - COL_AQ), lambda i: (0, 0)),
            pl.BlockSpec((1, LANES), lambda i: (0, 0)),
            pl.BlockSpec((None, 1, 2 * HG_W), lambda i: (layer, 0, 0)),
        ],
        out_specs=[pl.BlockSpec((tm, Y_W), lambda i: (i, 0)),
                   pl.BlockSpec((tm, GATES_W), lambda i: (i, 0))],
        out_shape=[jax.ShapeDtypeStruct((n, Y_W), F32), jax.ShapeDtypeStruct((n, GATES_W), BF16)],
        compiler_params=_params(("parallel",)),
        name="inproj",
    )(x, modv, modv, normv, w_in_b, qg, kg, lb_all)


def _rope(x, cos, sin_signed):
    lane = lax.broadcasted_iota(jnp.int32, (1, LANES), 1)
    first = (lane & 31) < 16
    swapped = jnp.where(first, pltpu.roll(x, LANES - 16, 1), pltpu.roll(x, 16, 1))
    return x * cos + swapped * sin_signed


def _attn_kernel(*refs, tq, seq, past, use_rope, use_sink, window, sink_off):
    it = iter(refs)
    sink_ref = next(it) if use_sink else None
    q_refs = (next(it), next(it))
    k_ref, v_ref = next(it), next(it)
    ck_ref = cv_ref = cos_ref = sin_ref = None
    if past:
        ck_ref, cv_ref = next(it), next(it)
    if use_rope:
        cos_ref, sin_ref = next(it), next(it)
    o_ref = next(it)
    kscr = next(it)
    vscr = next(it)

    qi = pl.program_id(1)
    lane = lax.broadcasted_iota(jnp.int32, (1, LANES), 1)
    lo = lane < HEAD_DIM
    one_lo = jnp.where(lane == 0, 1.0, 0.0)
    one_hi = jnp.where(lane == HEAD_DIM, 1.0, 0.0)
    rows = min(seq, 256)

    def put(x, xv, r0):
        sl = pl.ds(r0, x.shape[0])
        xr = pltpu.roll(x, HEAD_DIM, 1)
        kscr[0, sl, :] = jnp.where(lo, x, xr).astype(BF16)
        kscr[1, sl, :] = jnp.where(lo, xr, x).astype(BF16)
        vr = pltpu.roll(xv, HEAD_DIM, 1)
        vscr[0, sl, :] = jnp.where(lo, xv, one_hi).astype(BF16)
        vscr[1, sl, :] = jnp.where(lo, one_lo, vr).astype(BF16)
        vscr[2, sl, :] = jnp.where(lo, vr, one_hi).astype(BF16)
        vscr[3, sl, :] = jnp.where(lo, one_lo, xv).astype(BF16)

    @pl.when(qi == 0)
    def _build():
        if past:
            put(ck_ref[...], cv_ref[...], 0)

        def body(c, carry):
            r = pl.multiple_of(c * rows, rows)
            k = k_ref[pl.ds(r, rows), :]
            if use_rope:
                k = _rope(k, cos_ref[pl.ds(r, rows), :], sin_ref[pl.ds(r, rows), :])
            put(k, v_ref[pl.ds(r, rows), :], past + r)
            return carry

        lax.fori_loop(0, seq // rows, body, 0)

    q0 = pl.multiple_of(qi * tq, tq)
    if use_rope:
        cosq = cos_ref[pl.ds(q0, tq), :]
        sinq = sin_ref[pl.ds(q0, tq), :]
    nk = past + seq
    if window:
        span = tq + 2 * WINDOW
        start = pl.multiple_of(jnp.clip(q0 - WINDOW, 0, seq - span), WINDOW)
        t_pos = q0 + lax.broadcasted_iota(jnp.int32, (tq, 1), 0)
        s_pos = start + lax.broadcasted_iota(jnp.int32, (1, span), 1)
        band_ok = jnp.abs(t_pos - s_pos) <= WINDOW
        segments = [(slice(0, past), None), (pl.ds(past + start, span), band_ok)]
    else:
        segments = [(slice(0, nk), None)]

    q_scale = (HEAD_DIM ** -0.5) * LOG2E
    def score_phase(g):
        xs = []
        for p in range(2):
            q = q_refs[g][:, p * LANES:(p + 1) * LANES]
            if use_rope:
                q = _rope(q, cosq, sinq)
            xs.append(q * q_scale)
        q4 = jnp.concatenate([jnp.where(lo, xs[0], 0.0), jnp.where(lo, xs[1], 0.0),
                              jnp.where(lo, 0.0, xs[0]), jnp.where(lo, 0.0, xs[1])], axis=0).astype(BF16)
        scores = []
        for ksl, ok in segments:
            s = _dot_nt(q4, kscr[g, ksl, :])
            if ok is not None:
                s = jnp.where(jnp.concatenate([ok] * 4, axis=0), s, NEG_INF)
            scores.append(s)
        return scores

    def softmax_phase(g, scores):
        probs = [[] for _ in segments]
        extra = []
        for blk, head in enumerate((0, 2, 1, 3)):
            rsl = slice(blk * tq, (blk + 1) * tq)
            m = None
            for s in scores:
                ms = jnp.max(s[rsl], axis=-1, keepdims=True)
                m = ms if m is None else jnp.maximum(m, ms)
            if use_sink:
                sk = sink_ref[sink_off + g * 4 + head] * LOG2E
                m = jnp.maximum(m, sk)
                extra.append(jnp.exp2(sk - m))
            for i, s in enumerate(scores):
                probs[i].append(jnp.exp2(s[rsl] - m).astype(BF16))
        return probs, extra

    def value_phase(g, probs):
        acc = [None, None]
        for i, (ksl, _) in enumerate(segments):
            for par in range(2):
                e = jnp.concatenate(probs[i][2 * par:2 * par + 2], axis=0)
                pv = _dot(e, vscr[2 * g + par, ksl, :])
                acc[par] = pv if acc[par] is None else acc[par] + pv
        return acc

    def output_phase(g, acc, extra):
        den = [jnp.sum(jnp.where(lane == HEAD_DIM, acc[0], 0.0), axis=-1, keepdims=True),
               jnp.sum(jnp.where(lane == 0, acc[1], 0.0), axis=-1, keepdims=True)]
        if use_sink:
            den[0] = den[0] + jnp.concatenate(extra[0:2], axis=0)
            den[1] = den[1] + jnp.concatenate(extra[2:4], axis=0)
        outs = [acc[par] * (1.0 / den[par]) for par in range(2)]
        for p in range(2):
            rsl = slice(p * tq, (p + 1) * tq)
            o_pair = jnp.where(lo, outs[0][rsl], outs[1][rsl])
            o_ref[:, g * 256 + p * LANES:g * 256 + (p + 1) * LANES] = o_pair.astype(o_ref.dtype)

    scores = [score_phase(g) for g in range(ATT_KV)]
    accs, extras = [], []
    for g in range(ATT_KV):
        probs, extra = softmax_phase(g, scores[g])
        accs.append(value_phase(g, probs))
        extras.append(extra)
    for g in range(ATT_KV):
        output_phase(g, accs[g], extras[g])


def _attn_call(y, nb, seq, qcol, kcol, vcol, *, cache_k=None, cache_v=None, layer=0, rope=None,
               sink=None, window=False):
    tq = min(ATT_TQ, seq)
    nq = seq // tq
    past = 0 if cache_k is None else cache_k.shape[2]
    use_rope = rope is not None
    use_sink = sink is not None
    in_specs, args = [], []
    if use_sink:
        in_specs.append(pl.BlockSpec(memory_space=pltpu.SMEM))
        args.append(sink.reshape(-1).astype(F32))
    for g in range(ATT_KV):
        in_specs.append(pl.BlockSpec((tq, 256), lambda b, qi, g=g: (b * nq + qi, qcol // 256 + g)))
        args.append(y)
    in_specs.append(pl.BlockSpec((seq, LANES), lambda b, qi: (b, kcol // LANES)))
    in_specs.append(pl.BlockSpec((seq, LANES), lambda b, qi: (b, vcol // LANES)))
    args += [y, y]
    if past:
        cshape = cache_k.shape[:3] + (LANES,)
        in_specs += [pl.BlockSpec((None, None, past, LANES), lambda b, qi: (b, layer, 0, 0))] * 2
        args += [cache_k.reshape(cshape), cache_v.reshape(cshape)]
    if use_rope:
        in_specs += [pl.BlockSpec((seq, LANES), lambda b, qi: (0, 0))] * 2
        args += list(rope)
    kern = functools.partial(_attn_kernel, tq=tq, seq=seq, past=past, use_rope=use_rope,
                             use_sink=use_sink, window=window, sink_off=layer * 8)
    return pl.pallas_call(
        kern,
        grid=(nb, nq),
        in_specs=in_specs,
        out_specs=pl.BlockSpec((tq, BRANCH_W), lambda b, qi: (b * nq + qi, 0)),
        out_shape=jax.ShapeDtypeStruct((nb * seq, BRANCH_W), BF16),
        scratch_shapes=[pltpu.VMEM((2, past + seq, LANES), BF16), pltpu.VMEM((4, past + seq, LANES), BF16)],
        compiler_params=_params(("parallel", "arbitrary")),
        name="win" if window or use_sink else "att",
    )(*args)


def _hgrn_unit(q, k, b, v, st, fwd):
    C = HG_CHUNK
    G = C // HG_SUB
    vb = v.astype(BF16)

    b3 = b.reshape(G, HG_SUB, HG_DK)
    q3 = q.reshape(G, HG_SUB, HG_DK)
    k3 = k.reshape(G, HG_SUB, HG_DK)
    v3 = v.reshape(G, HG_SUB, HG_DK)
    c3 = jnp.log2(k3) - b3
    r8 = lax.broadcasted_iota(jnp.int32, (1, HG_SUB, 1), 1)
    out3 = None
    for jj in range(HG_SUB):
        ok = (r8 >= jj) if fwd else (r8 <= jj)
        kdec = jnp.exp2(jnp.where(ok, b3 + c3[:, jj:jj + 1, :], NEG_INF))
        score = jnp.sum(q3 * kdec, axis=-1, keepdims=True)
        term = score * v3[:, jj:jj + 1, :]
        out3 = term if out3 is None else out3 + term
    out = out3.reshape(C, HG_DK)

    ti = lax.broadcasted_iota(jnp.int32, (C, C), 0)
    si = lax.broadcasted_iota(jnp.int32, (C, C), 1)
    tcol = lax.broadcasted_iota(jnp.int32, (C, 1), 0)
    a_mat = None
    hh = HG_SUB
    while hh < C:
        grp = 2 * hh
        later = (tcol & (grp - 1)) >= hh
        is_q = later if fwd else jnp.logical_not(later)
        bg = b.reshape(C // grp, grp, HG_DK)
        bnd = bg[:, hh - 1:hh, :] if fwd else bg[:, hh:hh + 1, :]
        bnd = jnp.broadcast_to(bnd, (C // grp, grp, HG_DK)).reshape(C, HG_DK)
        ex = jnp.exp2(jnp.where(is_q, b - bnd, bnd - b))
        qt = jnp.where(is_q, q * ex, 0.0).astype(BF16)
        kt = jnp.where(is_q, 0.0, k * ex).astype(BF16)
        sc = _dot_nt(qt, kt)
        if grp < C:
            shift = grp.bit_length() - 1
            sc = jnp.where((ti >> shift) == (si >> shift), sc, 0.0)
        a_mat = sc if a_mat is None else a_mat + sc
        hh = grp
    out = out + _dot(a_mat.astype(BF16), vb)

    out = out + _dot_nt((q * jnp.exp2(b)).astype(BF16), st.astype(BF16))
    blast = b[C - 1:C, :] if fwd else b[0:1, :]
    kh = (k * jnp.exp2(blast - b)).astype(BF16)
    st_new = st * jnp.exp2(blast) + _dot_tn(vb, kh)
    return out, st_new


def _hgrn_kernel(*refs, nc, has_init, emit_state):
    it = iter(refs)
    io = [(next(it), next(it), next(it)) for _ in range(2)]
    s0_ref = next(it) if has_init else None
    o_refs = (next(it), next(it))
    sfin_ref = next(it) if emit_state else None
    st_scr = next(it)

    j = pl.program_id(1)
    C = HG_CHUNK

    @pl.when(j == 0)
    def _():
        for dirn in range(2):
            for h in range(HG_HEADS):
                if has_init:
                    st_scr[dirn, h] = s0_ref[dirn, h].T
                else:
                    st_scr[dirn, h] = jnp.zeros((HG_DK, HG_DK), F32)

    ti = lax.broadcasted_iota(jnp.int32, (C, C), 0)
    si = lax.broadcasted_iota(jnp.int32, (C, C), 1)
    for dirn in range(2):
        fwd = dirn == 0
        q_ref, f_ref, v_ref = io[dirn]
        f = f_ref[...]
        l2 = jnp.log2(jnp.maximum(f, TINY))
        tri = jnp.where((ti >= si) if fwd else (ti <= si), 1.0, 0.0).astype(BF16)
        l_hi = l2.astype(BF16)
        r1 = l2 - l_hi.astype(F32)
        l_mid = r1.astype(BF16)
        l_lo = (r1 - l_mid.astype(F32)).astype(BF16)
        b_all = _dot(tri, l_hi) + _dot(tri, l_mid) + _dot(tri, l_lo)
        for h in range(HG_HEADS):
            sl = slice(h * HG_DK, (h + 1) * HG_DK)
            out, st_new = _hgrn_unit(q_ref[:, sl], jnp.maximum(1.0 - f[:, sl], 0.0), b_all[:, sl], v_ref[:, sl],
                                     st_scr[dirn, h], fwd)
            o_refs[dirn][:, sl] = out
            st_scr[dirn, h] = st_new
            if emit_state:
                @pl.when(j == nc - 1)
                def _():
                    sfin_ref[dirn, h] = st_new.T


def _hgrn_call(y, nb, seq, layer, state0=None, emit_state=False):
    nc = seq // HG_CHUNK
    has_init = state0 is not None
    in_specs, args = [], []
    for dirn in range(2):
        rowblk = (lambda n, j: n * nc + j) if dirn == 0 else (lambda n, j: n * nc + nc - 1 - j)
        for col in (COL_HQ, COL_HF + dirn * HG_W, COL_HI):
            in_specs.append(pl.BlockSpec((HG_CHUNK, HG_W), lambda n, j, rowblk=rowblk, col=col: (rowblk(n, j), col // HG_W)))
            args.append(y)
    if has_init:
        in_specs.append(pl.BlockSpec((None, None, 2, HG_HEADS, HG_DK, HG_DK), lambda n, j: (n, layer, 0, 0, 0, 0)))
        args.append(state0)
    out_specs = [pl.BlockSpec((HG_CHUNK, HG_W), lambda n, j: (n * nc + j, 0)),
                 pl.BlockSpec((HG_CHUNK, HG_W), lambda n, j: (n * nc + nc - 1 - j, 0))]
    out_shape = [jax.ShapeDtypeStruct((nb * seq, HG_W), F32)] * 2
    if emit_state:
        out_specs.append(pl.BlockSpec((None, 2, HG_HEADS, HG_DK, HG_DK), lambda n, j: (n, 0, 0, 0, 0)))
        out_shape.append(jax.ShapeDtypeStruct((nb, 2, HG_HEADS, HG_DK, HG_DK), F32))
    res = pl.pallas_call(
        functools.partial(_hgrn_kernel, nc=nc, has_init=has_init, emit_state=emit_state),
        grid=(nb, nc),
        in_specs=in_specs,
        out_specs=out_specs,
        out_shape=out_shape,
        scratch_shapes=[pltpu.VMEM((2, HG_HEADS, HG_DK, HG_DK), F32)],
        compiler_params=_params(("parallel", "arbitrary")),
        name="hgrn",
    )(*args)
    return (res[0], res[1], res[2]) if emit_state else (res[0], res[1], None)


def _merge_kernel(oa_ref, ow_ref, of_ref, ob_ref, hg_ref, gates_ref, x_ref, gate_ref,
                  hgn_ref, wb_ref, wo_ref, o_ref):
    o = of_ref[...] + ob_ref[...]
    hg = hg_ref[...]
    hgn = hgn_ref[...]
    parts = []
    for h in range(HG_HEADS):
        sl = slice(h * HG_DK, (h + 1) * HG_DK)
        g = hg[:, sl]
        parts.append(_rms_rows(o[:, sl], hgn) * (g * _sigmoid(g)))
    o_hg = jnp.concatenate(parts, axis=-1).astype(BF16)
    branch = (oa_ref[...], o_hg, ow_ref[...])
    merged = None
    for k in range(3):
        term = gates_ref[:, k * D_MODEL:(k + 1) * D_MODEL].astype(F32) * _dot(branch[k], wb_ref[k])
        merged = term if merged is None else merged + term
    yv = _dot(merged.astype(BF16), wo_ref[...])
    o_ref[...] = x_ref[...] + gate_ref[...] * yv


def _merge_call(x, y, gates, o_att, o_win, o_f, o_b, modv, hg_norm_g, w_branch_b, w_out_b, layer, row_fn):
    n = x.shape[0]
    tm = min(DENSE_TM, n)
    row = lambda i: (i, 0)
    in_specs = [
        pl.BlockSpec((tm, BRANCH_W), row),
        pl.BlockSpec((tm, BRANCH_W), row),
        pl.BlockSpec((tm, HG_W), row),
        pl.BlockSpec((tm, HG_W), row),
        pl.BlockSpec((tm, HG_W), lambda i: (i, COL_HG // HG_W)),
        pl.BlockSpec((tm, GATES_W), row),
        pl.BlockSpec((tm, D_MODEL), row),
        _mod_spec(layer, 5, row_fn),
        pl.BlockSpec((1, HG_DK), lambda i: (0, 0)),
        _resident((None, 3, BRANCH_W, D_MODEL), lambda i: (layer, 0, 0, 0)),
        _resident((None, D_MODEL, D_MODEL), lambda i: (layer, 0, 0)),
    ]
    return pl.pallas_call(
        _merge_kernel,
        grid=(n // tm,),
        in_specs=in_specs,
        out_specs=pl.BlockSpec((tm, D_MODEL), row),
        out_shape=jax.ShapeDtypeStruct((n, D_MODEL), F32),
        compiler_params=_params(("parallel",)),
        name="merge",
    )(o_att, o_win, o_f, o_b, y, gates, x, modv, hg_norm_g[layer].reshape(1, HG_DK).astype(F32),
      w_branch_b, w_out_b)


def _rope_tables(n_lat):
    t = jnp.arange(n_lat, dtype=jnp.int32)
    row = (t // GRID_W).astype(F32)
    col = (t % GRID_W).astype(F32)
    axis_dim = HEAD_DIM // 2
    inv = ROPE_THETA ** (-jnp.arange(0, axis_dim, 2, dtype=F32) / axis_dim)
    ang_r = row[:, None] * inv
    ang_c = col[:, None] * inv
    cos64 = jnp.concatenate([jnp.cos(ang_r), jnp.cos(ang_r), jnp.cos(ang_c), jnp.cos(ang_c)], axis=-1)
    sin64 = jnp.concatenate([-jnp.sin(ang_r), jnp.sin(ang_r), -jnp.sin(ang_c), jnp.sin(ang_c)], axis=-1)
    return jnp.tile(cos64, (1, 2)), jnp.tile(sin64, (1, 2))


def kernel(x_prompt, x_sample, cache_k_attn, cache_v_attn, cache_k_win, cache_v_win, state_hgrn, c, c_ctx,
           w_mod, b_mod, norm_g, w_ffn_in, w_ffn_out, w_in, qk_norm_g, lower_bounds, hg_norm_g, sink_logit,
           w_branch, w_out, final_norm_g):
    batch, seq, _ = x_prompt.shape
    dec_batch, dec_seq, _ = x_sample.shape
    depth = w_mod.shape[0]

    cond = jnp.zeros((COND_ROWS, D_MODEL), F32).at[0].set(c_ctx).at[1:1 + dec_batch].set(c)
    modv = _mod_call(cond, w_mod, b_mod).reshape(depth * COND_ROWS * N_MOD, 1, D_MODEL)
    lb_all = _lb_call(lower_bounds).reshape(depth, 1, 2 * HG_W)
    normv = norm_g.astype(F32).reshape(depth * 3, 1, D_MODEL)
    w_ffn_in_b = w_ffn_in.astype(BF16)
    w_ffn_out_b = w_ffn_out.astype(BF16)
    w_in_b = w_in.astype(BF16)
    w_branch_b = w_branch.astype(BF16)
    w_out_b = w_out.astype(BF16)
    rope = _rope_tables(dec_seq)

    def run(x, nb, s, latent):
        n = nb * s
        tm = min(DENSE_TM, n)
        row_fn = (lambda i: 1 + i // (s // tm)) if latent else (lambda i: 0)
        x = x.reshape(n, D_MODEL)
        ctx_out = []
        for l in range(depth):
            x = _ffn_call(x, modv, normv, w_ffn_in_b, w_ffn_out_b, l, 0, row_fn)
            y, gates = _inproj_call(x, modv, normv, w_in_b, qk_norm_g, lb_all, l, row_fn)
            if latent:
                o_att = _attn_call(y, nb, s, COL_AQ, COL_AK, COL_AV, cache_k=cache_k_attn,
                                   cache_v=cache_v_attn, layer=l, rope=rope)
                o_win = _attn_call(y, nb, s, COL_WQ, COL_WK, COL_WV, cache_k=cache_k_win,
                                   cache_v=cache_v_win, layer=l, rope=rope, sink=sink_logit, window=True)
                o_f, o_b, _ = _hgrn_call(y, nb, s, l, state0=state_hgrn)
            else:
                o_att = _attn_call(y, nb, s, COL_AQ, COL_AK, COL_AV)
                o_win = _attn_call(y, nb, s, COL_WQ, COL_WK, COL_WV, layer=l, sink=sink_logit)
                o_f, o_b, s_fin = _hgrn_call(y, nb, s, l, emit_state=True)
                kv = lambda col: y[:, col:col + LANES].reshape(nb, s, ATT_KV, HEAD_DIM)
                ctx_out.append((kv(COL_AK), kv(COL_AV), kv(COL_WK), kv(COL_WV), s_fin))
            x = _merge_call(x, y, gates, o_att, o_win, o_f, o_b, modv, hg_norm_g, w_branch_b, w_out_b, l, row_fn)
            x = _ffn_call(x, modv, normv, w_ffn_in_b, w_ffn_out_b, l, 1, row_fn,
                          final_g=final_norm_g.astype(F32) if l == depth - 1 else None)
        return x.reshape(nb, s, D_MODEL), ctx_out

    y_prompt, ctx_out = run(x_prompt, batch, seq, False)
    y_sample, _ = run(x_sample, dec_batch, dec_seq, True)
    stack = lambda k: jnp.stack([cx[k] for cx in ctx_out], axis=1)
    return (y_prompt, y_sample, stack(0), stack(1), stack(2), stack(3), stack(4))
```

```python
import functools

import jax
import jax.numpy as jnp
from jax import lax
from jax.experimental import pallas as pl
from jax.experimental.pallas import tpu as pltpu

F32 = jnp.float32
BF16 = jnp.bfloat16

D_MODEL = 1024
GRID_W = 64
HEAD_DIM = 64
ATT_KV = 2
WINDOW = 128
HG_HEADS = 4
HG_DK = 128
HG_W = HG_HEADS * HG_DK
BRANCH_W = 512
D_FF = 2816
ROPE_THETA = 10000.0
EPS = 1e-6
NEG_INF = -1e30
TINY = 1e-30
LOG2E = 1.4426950408889634
N_MOD = 9
IN_W = 7168

LANES = 128
SUBLANES = 8

COND_ROWS = 16
DENSE_TM = 512
FFN_TF = 256
ATT_TQ = 256
ATT_SUB = 128
HG_CHUNK = 128
HG_SUB = SUBLANES
PAIR_SUB = 16
VMEM_LIMIT = 56 * 1024 * 1024

COL_AQ, COL_AK, COL_AV = 0, 512, 640
COL_WQ, COL_WK, COL_WV = 768, 1280, 1408
COL_HQ, COL_HF, COL_HI, COL_HG = 1536, 2048, 3072, 3584
COL_GATES = 4096
Y_W = COL_GATES
GATES_W = IN_W - COL_GATES


def _params(sem):
    return pltpu.CompilerParams(dimension_semantics=sem, vmem_limit_bytes=VMEM_LIMIT)


def _resident(block_shape, index_map):
    return pl.BlockSpec(block_shape, index_map, pipeline_mode=pl.Buffered(1))


def _sigmoid(x):
    return 1.0 / (1.0 + jnp.exp(-x))


def _dot(a, b):
    return jnp.dot(a, b, preferred_element_type=F32)


def _dot_nt(a, b):
    return lax.dot_general(a, b, (((1,), (1,)), ((), ())), preferred_element_type=F32)


def _dot_tn(a, b):
    return lax.dot_general(a, b, (((0,), (0,)), ((), ())), preferred_element_type=F32)


def _rms_rows(x, gain):
    return x * lax.rsqrt(jnp.mean(x * x, axis=-1, keepdims=True) + EPS) * gain


def _mod_kernel(c_ref, w_ref, b_ref, o_ref):
    c = c_ref[...]
    h = (c * _sigmoid(c)).astype(BF16)
    o_ref[...] = _dot(h, w_ref[...].astype(BF16)) + b_ref[...]


def _mod_call(cond, w_mod, b_mod):
    depth, d, nm = w_mod.shape
    tn = nm // 8
    return pl.pallas_call(
        _mod_kernel,
        grid=(depth, nm // tn),
        in_specs=[pl.BlockSpec((COND_ROWS, d), lambda l, j: (0, 0)),
                  pl.BlockSpec((None, d, tn), lambda l, j: (l, 0, j)),
                  pl.BlockSpec((None, 1, tn), lambda l, j: (l, 0, j))],
        out_specs=pl.BlockSpec((None, COND_ROWS, tn), lambda l, j: (l, 0, j)),
        out_shape=jax.ShapeDtypeStruct((depth, COND_ROWS, nm), F32),
        compiler_params=_params(("parallel", "parallel")),
        name="mod",
    )(cond, w_mod, b_mod.reshape(depth, 1, nm))


def _lb_kernel(x_ref, o_ref):
    x = x_ref[...]
    m = jnp.max(x, axis=0, keepdims=True)
    e = jnp.exp(x - m)
    s = e / jnp.sum(e, axis=0, keepdims=True)
    acc = jnp.zeros_like(s[0:1])
    for l in range(x.shape[0]):
        acc = acc + s[l:l + 1]
        o_ref[l:l + 1, :] = acc - s[0:1]


def _lb_call(lower_bounds):
    depth = lower_bounds.shape[0]
    x = lower_bounds.reshape(depth, -1).astype(F32)
    return pl.pallas_call(_lb_kernel, out_shape=jax.ShapeDtypeStruct(x.shape, F32), name="lb")(x)


def _ffn_kernel(*refs, final):
    if final:
        x_ref, shift_ref, scale_ref, gate_ref, g_ref, wgu_ref, wd_ref, fg_ref, o_ref = refs
    else:
        x_ref, shift_ref, scale_ref, gate_ref, g_ref, wgu_ref, wd_ref, o_ref = refs
    x = x_ref[...]
    h = (_rms_rows(x, g_ref[...]) * (1.0 + scale_ref[...]) + shift_ref[...]).astype(BF16)
    for c in range(D_FF // FFN_TF):
        a = _dot(h, wgu_ref[:, c * FFN_TF:(c + 1) * FFN_TF])
        u = _dot(h, wgu_ref[:, D_FF + c * FFN_TF:D_FF + (c + 1) * FFN_TF])
        act = (a * _sigmoid(a) * u).astype(BF16)
        part = _dot(act, wd_ref[c * FFN_TF:(c + 1) * FFN_TF, :])
        if c == 0:
            o_ref[...] = part
        else:
            o_ref[...] += part
    out = x + 0.5 * gate_ref[...] * o_ref[...]
    if final:
        out = _rms_rows(out, fg_ref[...])
    o_ref[...] = out


def _mod_spec(layer, k, row_fn):
    return pl.BlockSpec((None, 1, D_MODEL), lambda i: ((layer * COND_ROWS + row_fn(i)) * N_MOD + k, 0, 0))


def _ffn_call(x, modv, normv, w_in_b, w_out_b, layer, which, row_fn, final_g=None):
    n = x.shape[0]
    tm = min(DENSE_TM, n)
    k0 = 0 if which == 0 else 6
    nslot = 0 if which == 0 else 2
    final = final_g is not None
    in_specs = [
        pl.BlockSpec((tm, D_MODEL), lambda i: (i, 0)),
        _mod_spec(layer, k0, row_fn), _mod_spec(layer, k0 + 1, row_fn), _mod_spec(layer, k0 + 2, row_fn),
        pl.BlockSpec((None, 1, D_MODEL), lambda i: (layer * 3 + nslot, 0, 0)),
        _resident((None, None, D_MODEL, 2 * D_FF), lambda i: (layer, which, 0, 0)),
        _resident((None, None, D_FF, D_MODEL), lambda i: (layer, which, 0, 0)),
    ]
    args = [x, modv, modv, modv, normv, w_in_b, w_out_b]
    if final:
        in_specs.append(pl.BlockSpec((1, D_MODEL), lambda i: (0, 0)))
        args.append(final_g.reshape(1, D_MODEL))
    return pl.pallas_call(
        functools.partial(_ffn_kernel, final=final),
        grid=(n // tm,),
        in_specs=in_specs,
        out_specs=pl.BlockSpec((tm, D_MODEL), lambda i: (i, 0)),
        out_shape=jax.ShapeDtypeStruct((n, D_MODEL), F32),
        compiler_params=_params(("parallel",)),
        name="ffn",
    )(*args)


def _head_rms(y, gain):
    lane = lax.broadcasted_iota(jnp.int32, (1, LANES), 1)
    lo = lane < HEAD_DIM
    outs = []
    for c in range(y.shape[1] // LANES):
        blk = y[:, c * LANES:(c + 1) * LANES]
        sq = blk * blk
        s_lo = jnp.sum(jnp.where(lo, sq, 0.0), axis=-1, keepdims=True)
        s_hi = jnp.sum(jnp.where(lo, 0.0, sq), axis=-1, keepdims=True)
        ms = jnp.where(lo, s_lo, s_hi) * (1.0 / HEAD_DIM)
        outs.append(blk * lax.rsqrt(ms + EPS) * gain[:, c * LANES:(c + 1) * LANES])
    return outs[0] if len(outs) == 1 else jnp.concatenate(outs, axis=-1)


def _inproj_kernel(x_ref, shift_ref, scale_ref, g_ref, w_ref, qg_ref, kg_ref, lb_ref, y_ref, gates_ref):
    h = (_rms_rows(x_ref[...], g_ref[...]) * (1.0 + scale_ref[...]) + shift_ref[...]).astype(BF16)

    def proj(c0, c1):
        return _dot(h, w_ref[:, c0:c1])

    y_ref[:, COL_AQ:COL_AK] = _head_rms(proj(COL_AQ, COL_AK), qg_ref[...])
    t = proj(COL_AK, COL_WQ)
    y_ref[:, COL_AK:COL_AV] = _head_rms(t[:, :LANES], kg_ref[...])
    y_ref[:, COL_AV:COL_WQ] = t[:, LANES:]
    y_ref[:, COL_WQ:COL_HQ] = proj(COL_WQ, COL_HQ)
    t = proj(COL_HQ, COL_HF)
    y_ref[:, COL_HQ:COL_HF] = t * _sigmoid(t)
    for dirn in range(2):
        c0 = COL_HF + dirn * HG_W
        lb = lb_ref[:, dirn * HG_W:(dirn + 1) * HG_W]
        y_ref[:, c0:c0 + HG_W] = lb + (1.0 - lb) * _sigmoid(proj(c0, c0 + HG_W))
    y_ref[:, COL_HI:COL_GATES] = proj(COL_HI, COL_GATES)
    for k in range(GATES_W // D_MODEL):
        c0 = COL_GATES + k * D_MODEL
        gates_ref[:, k * D_MODEL:(k + 1) * D_MODEL] = _sigmoid(proj(c0, c0 + D_MODEL)).astype(BF16)


def _inproj_call(x, modv, normv, w_in_b, qk_norm_g, lb_all, layer, row_fn):
    n = x.shape[0]
    tm = min(DENSE_TM, n)
    qg = jnp.tile(qk_norm_g[layer, 0].astype(F32), (COL_AK - COL_AQ) // HEAD_DIM).reshape(1, COL_AK - COL_AQ)
    kg = jnp.tile(qk_norm_g[layer, 1].astype(F32), LANES // HEAD_DIM).reshape(1, LANES)
    return pl.pallas_call(
        _inproj_kernel,
        grid=(n // tm,),
        in_specs=[
            pl.BlockSpec((tm, D_MODEL), lambda i: (i, 0)),
            _mod_spec(layer, 3, row_fn), _mod_spec(layer, 4, row_fn),
            pl.BlockSpec((None, 1, D_MODEL), lambda i: (layer * 3 + 1, 0, 0)),
            _resident((None, D_MODEL, IN_W), lambda i: (layer, 0, 0)),
            pl.BlockSpec((1, COL_AK - COL_AQ), lambda i: (0, 0)),
            pl.BlockSpec((1, LANES), lambda i: (0, 0)),
            pl.BlockSpec((None, 1, 2 * HG_W), lambda i: (layer, 0, 0)),
        ],
        out_specs=[pl.BlockSpec((tm, Y_W), lambda i: (i, 0)),
                   pl.BlockSpec((tm, GATES_W), lambda i: (i, 0))],
        out_shape=[jax.ShapeDtypeStruct((n, Y_W), F32), jax.ShapeDtypeStruct((n, GATES_W), BF16)],
        compiler_params=_params(("parallel",)),
        name="inproj",
    )(x, modv, modv, normv, w_in_b, qg, kg, lb_all)


def _rope(x, cos, sin_signed):
    lane = lax.broadcasted_iota(jnp.int32, (1, LANES), 1)
    first = (lane & 31) < 16
    swapped = jnp.where(first, pltpu.roll(x, LANES - 16, 1), pltpu.roll(x, 16, 1))
    return x * cos + swapped * sin_signed


def _attn_kernel(*refs, tq, seq, past, use_rope, use_sink, window, sink_off):
    it = iter(refs)
    sink_ref = next(it) if use_sink else None
    q_refs = (next(it), next(it))
    k_ref, v_ref = next(it), next(it)
    ck_ref = cv_ref = cos_ref = sin_ref = None
    if past:
        ck_ref, cv_ref = next(it), next(it)
    if use_rope:
        cos_ref, sin_ref = next(it), next(it)
    o_ref = next(it)
    kscr = next(it)
    vscr = next(it)

    qi = pl.program_id(1)
    lane = lax.broadcasted_iota(jnp.int32, (1, LANES), 1)
    lo = lane < HEAD_DIM
    one_lo = jnp.where(lane == 0, 1.0, 0.0)
    one_hi = jnp.where(lane == HEAD_DIM, 1.0, 0.0)
    rows = min(seq, 256)

    def put(x, xv, r0):
        sl = pl.ds(r0, x.shape[0])
        xr = pltpu.roll(x, HEAD_DIM, 1)
        kscr[0, sl, :] = jnp.where(lo, x, xr).astype(BF16)
        kscr[1, sl, :] = jnp.where(lo, xr, x).astype(BF16)
        vr = pltpu.roll(xv, HEAD_DIM, 1)
        vscr[0, sl, :] = jnp.where(lo, xv, one_hi).astype(BF16)
        vscr[1, sl, :] = jnp.where(lo, one_lo, vr).astype(BF16)
        vscr[2, sl, :] = jnp.where(lo, vr, one_hi).astype(BF16)
        vscr[3, sl, :] = jnp.where(lo, one_lo, xv).astype(BF16)

    @pl.when(qi == 0)
    def _build():
        if past:
            put(ck_ref[...], cv_ref[...], 0)

        def body(c, carry):
            r = pl.multiple_of(c * rows, rows)
            k = k_ref[pl.ds(r, rows), :]
            if use_rope:
                k = _rope(k, cos_ref[pl.ds(r, rows), :], sin_ref[pl.ds(r, rows), :])
            put(k, v_ref[pl.ds(r, rows), :], past + r)
            return carry

        lax.fori_loop(0, seq // rows, body, 0)

    ts = min(ATT_SUB, tq)
    q0 = pl.multiple_of(qi * tq, tq)
    nk = past + seq
    q_scale = (HEAD_DIM ** -0.5) * LOG2E

    def unit_segments(h):
        if not window:
            return [(slice(0, nk), None)]
        span = ts + 2 * WINDOW
        t0 = q0 + h * ts
        start = pl.multiple_of(jnp.clip(t0 - WINDOW, 0, seq - span), WINDOW)
        t_pos = t0 + lax.broadcasted_iota(jnp.int32, (ts, 1), 0)
        s_pos = start + lax.broadcasted_iota(jnp.int32, (1, span), 1)
        band_ok = jnp.abs(t_pos - s_pos) <= WINDOW
        return [(slice(0, past), None), (pl.ds(past + start, span), band_ok)]

    def score_phase(g, h, segments):
        xs = []
        for p in range(2):
            q = q_refs[g][h * ts:(h + 1) * ts, p * LANES:(p + 1) * LANES]
            if use_rope:
                tsl = pl.ds(q0 + h * ts, ts)
                q = _rope(q, cos_ref[tsl, :], sin_ref[tsl, :])
            xs.append(q * q_scale)
        q4 = jnp.concatenate([jnp.where(lo, xs[0], 0.0), jnp.where(lo, xs[1], 0.0),
                              jnp.where(lo, 0.0, xs[0]), jnp.where(lo, 0.0, xs[1])], axis=0).astype(BF16)
        scores = []
        for ksl, ok in segments:
            s = _dot_nt(q4, kscr[g, ksl, :])
            if ok is not None:
                s = jnp.where(jnp.concatenate([ok] * 4, axis=0), s, NEG_INF)
            scores.append(s)
        return scores

    def softmax_phase(g, scores):
        probs = [[] for _ in scores]
        extra = []
        for blk, head in enumerate((0, 2, 1, 3)):
            rsl = slice(blk * ts, (blk + 1) * ts)
            m = None
            for s in scores:
                ms = jnp.max(s[rsl], axis=-1, keepdims=True)
                m = ms if m is None else jnp.maximum(m, ms)
            if use_sink:
                sk = sink_ref[sink_off + g * 4 + head] * LOG2E
                m = jnp.maximum(m, sk)
                extra.append(jnp.exp2(sk - m))
            for i, s in enumerate(scores):
                probs[i].append(jnp.exp2(s[rsl] - m).astype(BF16))
        return probs, extra

    def value_phase(g, probs, segments):
        acc = [None, None]
        for i, (ksl, _) in enumerate(segments):
            for par in range(2):
                e = jnp.concatenate(probs[i][2 * par:2 * par + 2], axis=0)
                pv = _dot(e, vscr[2 * g + par, ksl, :])
                acc[par] = pv if acc[par] is None else acc[par] + pv
        return acc

    def output_phase(g, h, acc, extra):
        den = [jnp.sum(jnp.where(lane == HEAD_DIM, acc[0], 0.0), axis=-1, keepdims=True),
               jnp.sum(jnp.where(lane == 0, acc[1], 0.0), axis=-1, keepdims=True)]
        if use_sink:
            den[0] = den[0] + jnp.concatenate(extra[0:2], axis=0)
            den[1] = den[1] + jnp.concatenate(extra[2:4], axis=0)
        outs = [acc[par] * (1.0 / den[par]) for par in range(2)]
        for p in range(2):
            rsl = slice(p * ts, (p + 1) * ts)
            o_pair = jnp.where(lo, outs[0][rsl], outs[1][rsl])
            o_ref[h * ts:(h + 1) * ts, g * 256 + p * LANES:g * 256 + (p + 1) * LANES] = o_pair.astype(o_ref.dtype)

    units = [(g, h) for h in range(tq // ts) for g in range(ATT_KV)]
    segs = {h: unit_segments(h) for h in range(tq // ts)}
    scores = [score_phase(g, h, segs[h]) for g, h in units]
    accs, extras = [], []
    for (g, h), sc in zip(units, scores):
        probs, extra = softmax_phase(g, sc)
        accs.append(value_phase(g, probs, segs[h]))
        extras.append(extra)
    for (g, h), acc, extra in zip(units, accs, extras):
        output_phase(g, h, acc, extra)


def _attn_call(y, nb, seq, qcol, kcol, vcol, *, cache_k=None, cache_v=None, layer=0, rope=None,
               sink=None, window=False):
    tq = min(ATT_TQ, seq)
    nq = seq // tq
    past = 0 if cache_k is None else cache_k.shape[2]
    use_rope = rope is not None
    use_sink = sink is not None
    in_specs, args = [], []
    if use_sink:
        in_specs.append(pl.BlockSpec(memory_space=pltpu.SMEM))
        args.append(sink.reshape(-1).astype(F32))
    for g in range(ATT_KV):
        in_specs.append(pl.BlockSpec((tq, 256), lambda b, qi, g=g: (b * nq + qi, qcol // 256 + g)))
        args.append(y)
    in_specs.append(pl.BlockSpec((seq, LANES), lambda b, qi: (b, kcol // LANES)))
    in_specs.append(pl.BlockSpec((seq, LANES), lambda b, qi: (b, vcol // LANES)))
    args += [y, y]
    if past:
        cshape = cache_k.shape[:3] + (LANES,)
        in_specs += [pl.BlockSpec((None, None, past, LANES), lambda b, qi: (b, layer, 0, 0))] * 2
        args += [cache_k.reshape(cshape), cache_v.reshape(cshape)]
    if use_rope:
        in_specs += [pl.BlockSpec((seq, LANES), lambda b, qi: (0, 0))] * 2
        args += list(rope)
    kern = functools.partial(_attn_kernel, tq=tq, seq=seq, past=past, use_rope=use_rope,
                             use_sink=use_sink, window=window, sink_off=layer * 8)
    return pl.pallas_call(
        kern,
        grid=(nb, nq),
        in_specs=in_specs,
        out_specs=pl.BlockSpec((tq, BRANCH_W), lambda b, qi: (b * nq + qi, 0)),
        out_shape=jax.ShapeDtypeStruct((nb * seq, BRANCH_W), BF16),
        scratch_shapes=[pltpu.VMEM((2, past + seq, LANES), BF16), pltpu.VMEM((4, past + seq, LANES), BF16)],
        compiler_params=_params(("parallel", "arbitrary")),
        name="win" if window or use_sink else "att",
    )(*args)


def _pair_table(fwd):
    import numpy as np
    C = HG_CHUNK
    t = np.arange(C)[:, None]
    s = np.arange(C)[None, :]
    seen = (s <= t) if fwd else (s >= t)
    tab = np.full((C, C), -1, np.int32)
    hh, idx = C // 2, (C // HG_SUB).bit_length() - 2
    while hh >= HG_SUB:
        tab = np.where((t // (2 * hh) == s // (2 * hh)) & seen, idx, tab)
        hh, idx = hh // 2, idx - 1
    tab = np.where((t // HG_SUB == s // HG_SUB), np.where(seen, PAIR_SUB + s % HG_SUB, -1), tab)
    return tab.astype(np.int32)


def _hgrn_unit(q, k, b, v, st, sub_masks, level_masks, fwd):
    C = HG_CHUNK
    G = C // HG_SUB
    vb = v.astype(BF16)

    b3 = b.reshape(G, HG_SUB, HG_DK)
    q3 = q.reshape(G, HG_SUB, HG_DK)
    c3 = (jnp.log2(k) - b).reshape(G, HG_SUB, HG_DK)
    a3 = jnp.zeros((G, HG_SUB, C), F32)
    for jj in range(HG_SUB):
        kdec = jnp.exp2(b3 + c3[:, jj:jj + 1, :])
        score = jnp.sum(q3 * kdec, axis=-1, keepdims=True)
        a3 = jnp.where(sub_masks[jj], score, a3)
    a_mat = a3.reshape(C, C)

    tcol = lax.broadcasted_iota(jnp.int32, (C, 1), 0)
    hh, idx = HG_SUB, 0
    while hh < C:
        grp = 2 * hh
        later = (tcol & (grp - 1)) >= hh
        is_q = later if fwd else jnp.logical_not(later)
        bg = b.reshape(C // grp, grp, HG_DK)
        bnd = bg[:, hh - 1:hh, :] if fwd else bg[:, hh:hh + 1, :]
        bnd = jnp.broadcast_to(bnd, (C // grp, grp, HG_DK)).reshape(C, HG_DK)
        w = (jnp.where(is_q, q, k) * jnp.exp2(jnp.where(is_q, b - bnd, bnd - b))).astype(BF16)
        a_mat = jnp.where(level_masks[idx], _dot_nt(w, w), a_mat)
        hh, idx = grp, idx + 1
    out = _dot(a_mat.astype(BF16), vb)

    out = out + _dot_nt((q * jnp.exp2(b)).astype(BF16), st.astype(BF16))
    blast = b[C - 1:C, :] if fwd else b[0:1, :]
    kh = (k * jnp.exp2(blast - b)).astype(BF16)
    st_new = st * jnp.exp2(blast) + _dot_tn(vb, kh)
    return out, st_new


def _hgrn_kernel(*refs, nc, has_init, emit_state):
    it = iter(refs)
    io = [(next(it), next(it), next(it)) for _ in range(2)]
    pair_ref = next(it)
    s0_ref = next(it) if has_init else None
    o_refs = (next(it), next(it))
    sfin_ref = next(it) if emit_state else None
    st_scr = next(it)

    j = pl.program_id(1)
    C = HG_CHUNK

    @pl.when(j == 0)
    def _():
        for dirn in range(2):
            for h in range(HG_HEADS):
                if has_init:
                    st_scr[dirn, h] = s0_ref[dirn, h].T
                else:
                    st_scr[dirn, h] = jnp.zeros((HG_DK, HG_DK), F32)

    ti = lax.broadcasted_iota(jnp.int32, (C, C), 0)
    si = lax.broadcasted_iota(jnp.int32, (C, C), 1)
    for dirn in range(2):
        fwd = dirn == 0
        q_ref, f_ref, v_ref = io[dirn]
        f = f_ref[...]
        l2 = jnp.log2(jnp.maximum(f, TINY))
        tri = jnp.where((ti >= si) if fwd else (ti <= si), 1.0, 0.0).astype(BF16)
        l_hi = l2.astype(BF16)
        r1 = l2 - l_hi.astype(F32)
        l_mid = r1.astype(BF16)
        l_lo = (r1 - l_mid.astype(F32)).astype(BF16)
        b_all = _dot(tri, l_hi) + _dot(tri, l_mid) + _dot(tri, l_lo)
        pair = pair_ref[dirn]
        pair3 = pair.reshape(C // HG_SUB, HG_SUB, C)
        sub_masks = [pair3 == PAIR_SUB + jj for jj in range(HG_SUB)]
        level_masks = [pair == idx for idx in range((C // HG_SUB).bit_length() - 1)]
        for h in range(HG_HEADS):
            sl = slice(h * HG_DK, (h + 1) * HG_DK)
            out, st_new = _hgrn_unit(q_ref[:, sl], jnp.maximum(1.0 - f[:, sl], 0.0), b_all[:, sl], v_ref[:, sl],
                                     st_scr[dirn, h], sub_masks, level_masks, fwd)
            o_refs[dirn][:, sl] = out
            st_scr[dirn, h] = st_new
            if emit_state:
                @pl.when(j == nc - 1)
                def _():
                    sfin_ref[dirn, h] = st_new.T


def _hgrn_call(y, nb, seq, layer, state0=None, emit_state=False):
    nc = seq // HG_CHUNK
    has_init = state0 is not None
    in_specs, args = [], []
    for dirn in range(2):
        rowblk = (lambda n, j: n * nc + j) if dirn == 0 else (lambda n, j: n * nc + nc - 1 - j)
        for col in (COL_HQ, COL_HF + dirn * HG_W, COL_HI):
            in_specs.append(pl.BlockSpec((HG_CHUNK, HG_W), lambda n, j, rowblk=rowblk, col=col: (rowblk(n, j), col // HG_W)))
            args.append(y)
    in_specs.append(pl.BlockSpec((2, HG_CHUNK, HG_CHUNK), lambda n, j: (0, 0, 0)))
    args.append(jnp.asarray([_pair_table(True), _pair_table(False)]))
    if has_init:
        in_specs.append(pl.BlockSpec((None, None, 2, HG_HEADS, HG_DK, HG_DK), lambda n, j: (n, layer, 0, 0, 0, 0)))
        args.append(state0)
    out_specs = [pl.BlockSpec((HG_CHUNK, HG_W), lambda n, j: (n * nc + j, 0)),
                 pl.BlockSpec((HG_CHUNK, HG_W), lambda n, j: (n * nc + nc - 1 - j, 0))]
    out_shape = [jax.ShapeDtypeStruct((nb * seq, HG_W), F32)] * 2
    if emit_state:
        out_specs.append(pl.BlockSpec((None, 2, HG_HEADS, HG_DK, HG_DK), lambda n, j: (n, 0, 0, 0, 0)))
        out_shape.append(jax.ShapeDtypeStruct((nb, 2, HG_HEADS, HG_DK, HG_DK), F32))
    res = pl.pallas_call(
        functools.partial(_hgrn_kernel, nc=nc, has_init=has_init, emit_state=emit_state),
        grid=(nb, nc),
        in_specs=in_specs,
        out_specs=out_specs,
        out_shape=out_shape,
        scratch_shapes=[pltpu.VMEM((2, HG_HEADS, HG_DK, HG_DK), F32)],
        compiler_params=_params(("parallel", "arbitrary")),
        name="hgrn",
    )(*args)
    return (res[0], res[1], res[2]) if emit_state else (res[0], res[1], None)


def _merge_kernel(oa_ref, ow_ref, of_ref, ob_ref, hg_ref, gates_ref, x_ref, gate_ref,
                  hgn_ref, wb_ref, wo_ref, o_ref):
    o = of_ref[...] + ob_ref[...]
    hg = hg_ref[...]
    hgn = hgn_ref[...]
    parts = []
    for h in range(HG_HEADS):
        sl = slice(h * HG_DK, (h + 1) * HG_DK)
        g = hg[:, sl]
        parts.append(_rms_rows(o[:, sl], hgn) * (g * _sigmoid(g)))
    o_hg = jnp.concatenate(parts, axis=-1).astype(BF16)
    branch = (oa_ref[...], o_hg, ow_ref[...])
    merged = None
    for k in range(3):
        term = gates_ref[:, k * D_MODEL:(k + 1) * D_MODEL].astype(F32) * _dot(branch[k], wb_ref[k])
        merged = term if merged is None else merged + term
    yv = _dot(merged.astype(BF16), wo_ref[...])
    o_ref[...] = x_ref[...] + gate_ref[...] * yv


def _merge_call(x, y, gates, o_att, o_win, o_f, o_b, modv, hg_norm_g, w_branch_b, w_out_b, layer, row_fn):
    n = x.shape[0]
    tm = min(DENSE_TM, n)
    row = lambda i: (i, 0)
    in_specs = [
        pl.BlockSpec((tm, BRANCH_W), row),
        pl.BlockSpec((tm, BRANCH_W), row),
        pl.BlockSpec((tm, HG_W), row),
        pl.BlockSpec((tm, HG_W), row),
        pl.BlockSpec((tm, HG_W), lambda i: (i, COL_HG // HG_W)),
        pl.BlockSpec((tm, GATES_W), row),
        pl.BlockSpec((tm, D_MODEL), row),
        _mod_spec(layer, 5, row_fn),
        pl.BlockSpec((1, HG_DK), lambda i: (0, 0)),
        _resident((None, 3, BRANCH_W, D_MODEL), lambda i: (layer, 0, 0, 0)),
        _resident((None, D_MODEL, D_MODEL), lambda i: (layer, 0, 0)),
    ]
    return pl.pallas_call(
        _merge_kernel,
        grid=(n // tm,),
        in_specs=in_specs,
        out_specs=pl.BlockSpec((tm, D_MODEL), row),
        out_shape=jax.ShapeDtypeStruct((n, D_MODEL), F32),
        compiler_params=_params(("parallel",)),
        name="merge",
    )(o_att, o_win, o_f, o_b, y, gates, x, modv, hg_norm_g[layer].reshape(1, HG_DK).astype(F32),
      w_branch_b, w_out_b)


def _rope_tables(n_lat):
    t = jnp.arange(n_lat, dtype=jnp.int32)
    row = (t // GRID_W).astype(F32)
    col = (t % GRID_W).astype(F32)
    axis_dim = HEAD_DIM // 2
    inv = ROPE_THETA ** (-jnp.arange(0, axis_dim, 2, dtype=F32) / axis_dim)
    ang_r = row[:, None] * inv
    ang_c = col[:, None] * inv
    cos64 = jnp.concatenate([jnp.cos(ang_r), jnp.cos(ang_r), jnp.cos(ang_c), jnp.cos(ang_c)], axis=-1)
    sin64 = jnp.concatenate([-jnp.sin(ang_r), jnp.sin(ang_r), -jnp.sin(ang_c), jnp.sin(ang_c)], axis=-1)
    return jnp.tile(cos64, (1, 2)), jnp.tile(sin64, (1, 2))


def kernel(x_prompt, x_sample, cache_k_attn, cache_v_attn, cache_k_win, cache_v_win, state_hgrn, c, c_ctx,
           w_mod, b_mod, norm_g, w_ffn_in, w_ffn_out, w_in, qk_norm_g, lower_bounds, hg_norm_g, sink_logit,
           w_branch, w_out, final_norm_g):
    batch, seq, _ = x_prompt.shape
    dec_batch, dec_seq, _ = x_sample.shape
    depth = w_mod.shape[0]

    cond = jnp.zeros((COND_ROWS, D_MODEL), F32).at[0].set(c_ctx).at[1:1 + dec_batch].set(c)
    modv = _mod_call(cond, w_mod, b_mod).reshape(depth * COND_ROWS * N_MOD, 1, D_MODEL)
    lb_all = _lb_call(lower_bounds).reshape(depth, 1, 2 * HG_W)
    normv = norm_g.astype(F32).reshape(depth * 3, 1, D_MODEL)
    w_ffn_in_b = w_ffn_in.astype(BF16)
    w_ffn_out_b = w_ffn_out.astype(BF16)
    w_in_b = w_in.astype(BF16)
    w_branch_b = w_branch.astype(BF16)
    w_out_b = w_out.astype(BF16)
    rope = _rope_tables(dec_seq)

    def run(x, nb, s, latent):
        n = nb * s
        tm = min(DENSE_TM, n)
        row_fn = (lambda i: 1 + i // (s // tm)) if latent else (lambda i: 0)
        x = x.reshape(n, D_MODEL)
        ctx_out = []
        for l in range(depth):
            x = _ffn_call(x, modv, normv, w_ffn_in_b, w_ffn_out_b, l, 0, row_fn)
            y, gates = _inproj_call(x, modv, normv, w_in_b, qk_norm_g, lb_all, l, row_fn)
            if latent:
                o_att = _attn_call(y, nb, s, COL_AQ, COL_AK, COL_AV, cache_k=cache_k_attn,
                                   cache_v=cache_v_attn, layer=l, rope=rope)
                o_win = _attn_call(y, nb, s, COL_WQ, COL_WK, COL_WV, cache_k=cache_k_win,
                                   cache_v=cache_v_win, layer=l, rope=rope, sink=sink_logit, window=True)
                o_f, o_b, _ = _hgrn_call(y, nb, s, l, state0=state_hgrn)
            else:
                o_att = _attn_call(y, nb, s, COL_AQ, COL_AK, COL_AV)
                o_win = _attn_call(y, nb, s, COL_WQ, COL_WK, COL_WV, layer=l, sink=sink_logit)
                o_f, o_b, s_fin = _hgrn_call(y, nb, s, l, emit_state=True)
                kv = lambda col: y[:, col:col + LANES].reshape(nb, s, ATT_KV, HEAD_DIM)
                ctx_out.append((kv(COL_AK), kv(COL_AV), kv(COL_WK), kv(COL_WV), s_fin))
            x = _merge_call(x, y, gates, o_att, o_win, o_f, o_b, modv, hg_norm_g, w_branch_b, w_out_b, l, row_fn)
            x = _ffn_call(x, modv, normv, w_ffn_in_b, w_ffn_out_b, l, 1, row_fn,
                          final_g=final_norm_g.astype(F32) if l == depth - 1 else None)
        return x.reshape(nb, s, D_MODEL), ctx_out

    y_prompt, ctx_out = run(x_prompt, batch, seq, False)
    y_sample, _ = run(x_sample, dec_batch, dec_seq, True)
    stack = lambda k: jnp.stack([cx[k] for cx in ctx_out], axis=1)
    return (y_prompt, y_sample, stack(0), stack(1), stack(2), stack(3), stack(4))
```

```python
import functools

import jax
import jax.numpy as jnp
from jax import lax
from jax.experimental import pallas as pl
from jax.experimental.pallas import tpu as pltpu

F32 = jnp.float32
BF16 = jnp.bfloat16

D_MODEL = 1024
GRID_W = 64
HEAD_DIM = 64
ATT_KV = 2
WINDOW = 128
HG_HEADS = 4
HG_DK = 128
HG_W = HG_HEADS * HG_DK
BRANCH_W = 512
D_FF = 2816
ROPE_THETA = 10000.0
EPS = 1e-6
NEG_INF = -1e30
TINY = 1e-30
LOG2E = 1.4426950408889634
N_MOD = 9
IN_W = 7168

LANES = 128
SUBLANES = 8

COND_ROWS = 16
DENSE_TM = 512
FFN_TM = 1024
FFN_TF = 256
ATT_TQ = 256
ATT_SUB = 128
HG_CHUNK = 128
HG_SUB = SUBLANES
PAIR_SUB = 16
VMEM_LIMIT = 56 * 1024 * 1024

COL_AQ, COL_AK, COL_AV = 0, 512, 640
COL_WQ, COL_WK, COL_WV = 768, 1280, 1408
COL_HQ, COL_HF, COL_HI, COL_HG = 1536, 2048, 3072, 3584
COL_GATES = 4096
Y_W = COL_GATES
GATES_W = IN_W - COL_GATES


def _params(sem):
    return pltpu.CompilerParams(dimension_semantics=sem, vmem_limit_bytes=VMEM_LIMIT)


def _resident(block_shape, index_map):
    return pl.BlockSpec(block_shape, index_map, pipeline_mode=pl.Buffered(1))


def _sigmoid(x):
    return 1.0 / (1.0 + jnp.exp(-x))


def _dot(a, b):
    return jnp.dot(a, b, preferred_element_type=F32)


def _dot_nt(a, b):
    return lax.dot_general(a, b, (((1,), (1,)), ((), ())), preferred_element_type=F32)


def _dot_tn(a, b):
    return lax.dot_general(a, b, (((0,), (0,)), ((), ())), preferred_element_type=F32)


def _rms_rows(x, gain):
    return x * lax.rsqrt(jnp.mean(x * x, axis=-1, keepdims=True) + EPS) * gain


def _mod_kernel(c_ref, w_ref, b_ref, o_ref):
    c = c_ref[...]
    h = (c * _sigmoid(c)).astype(BF16)
    o_ref[...] = _dot(h, w_ref[...].astype(BF16)) + b_ref[...]


def _mod_call(cond, w_mod, b_mod):
    depth, d, nm = w_mod.shape
    tn = nm // 8
    return pl.pallas_call(
        _mod_kernel,
        grid=(depth, nm // tn),
        in_specs=[pl.BlockSpec((COND_ROWS, d), lambda l, j: (0, 0)),
                  pl.BlockSpec((None, d, tn), lambda l, j: (l, 0, j)),
                  pl.BlockSpec((None, 1, tn), lambda l, j: (l, 0, j))],
        out_specs=pl.BlockSpec((None, COND_ROWS, tn), lambda l, j: (l, 0, j)),
        out_shape=jax.ShapeDtypeStruct((depth, COND_ROWS, nm), F32),
        compiler_params=_params(("parallel", "parallel")),
        name="mod",
    )(cond, w_mod, b_mod.reshape(depth, 1, nm))


def _lb_kernel(x_ref, o_ref):
    x = x_ref[...]
    m = jnp.max(x, axis=0, keepdims=True)
    e = jnp.exp(x - m)
    s = e / jnp.sum(e, axis=0, keepdims=True)
    acc = jnp.zeros_like(s[0:1])
    for l in range(x.shape[0]):
        acc = acc + s[l:l + 1]
        o_ref[l:l + 1, :] = acc - s[0:1]


def _lb_call(lower_bounds):
    depth = lower_bounds.shape[0]
    x = lower_bounds.reshape(depth, -1).astype(F32)
    return pl.pallas_call(_lb_kernel, out_shape=jax.ShapeDtypeStruct(x.shape, F32), name="lb")(x)


def _ffn_block(x, shift, scale, gate, g, wgu_ref, wd_ref, o_ref, final_g):
    h = (_rms_rows(x, g) * (1.0 + scale) + shift).astype(BF16)
    for c in range(D_FF // FFN_TF):
        a = _dot(h, wgu_ref[:, c * FFN_TF:(c + 1) * FFN_TF])
        u = _dot(h, wgu_ref[:, D_FF + c * FFN_TF:D_FF + (c + 1) * FFN_TF])
        act = (a * _sigmoid(a) * u).astype(BF16)
        part = _dot(act, wd_ref[c * FFN_TF:(c + 1) * FFN_TF, :])
        if c == 0:
            o_ref[...] = part
        else:
            o_ref[...] += part
    out = x + 0.5 * gate * o_ref[...]
    if final_g is not None:
        out = _rms_rows(out, final_g)
    o_ref[...] = out


def _ffn_kernel(x_ref, shift_ref, scale_ref, gate_ref, g_ref, wgu_ref, wd_ref, o_ref):
    _ffn_block(x_ref[...], shift_ref[...], scale_ref[...], gate_ref[...], g_ref[...], wgu_ref, wd_ref, o_ref, None)


def _mod_spec(layer, k, row_fn):
    return pl.BlockSpec((None, 1, D_MODEL), lambda i: ((layer * COND_ROWS + row_fn(i)) * N_MOD + k, 0, 0))


def _ffn_specs(layer, which, row_fn):
    k0 = 3 * (2 * which)
    return [
        _mod_spec(layer, k0, row_fn), _mod_spec(layer, k0 + 1, row_fn), _mod_spec(layer, k0 + 2, row_fn),
        pl.BlockSpec((None, 1, D_MODEL), lambda i: (layer * 3 + 2 * which, 0, 0)),
        _resident((None, None, D_MODEL, 2 * D_FF), lambda i: (layer, which, 0, 0)),
        _resident((None, None, D_FF, D_MODEL), lambda i: (layer, which, 0, 0)),
    ]


def _ffn_call(x, modv, normv, w_in_b, w_out_b, layer, row_fn, tm):
    n = x.shape[0]
    return pl.pallas_call(
        _ffn_kernel,
        grid=(n // tm,),
        in_specs=[pl.BlockSpec((tm, D_MODEL), lambda i: (i, 0))] + _ffn_specs(layer, 0, row_fn),
        out_specs=pl.BlockSpec((tm, D_MODEL), lambda i: (i, 0)),
        out_shape=jax.ShapeDtypeStruct((n, D_MODEL), F32),
        compiler_params=_params(("parallel",)),
        name="ffn",
    )(x, modv, modv, modv, normv, w_in_b, w_out_b)


def _head_rms(y, gain):
    lane = lax.broadcasted_iota(jnp.int32, (1, LANES), 1)
    lo = lane < HEAD_DIM
    outs = []
    for c in range(y.shape[1] // LANES):
        blk = y[:, c * LANES:(c + 1) * LANES]
        sq = blk * blk
        s_lo = jnp.sum(jnp.where(lo, sq, 0.0), axis=-1, keepdims=True)
        s_hi = jnp.sum(jnp.where(lo, 0.0, sq), axis=-1, keepdims=True)
        ms = jnp.where(lo, s_lo, s_hi) * (1.0 / HEAD_DIM)
        outs.append(blk * lax.rsqrt(ms + EPS) * gain[:, c * LANES:(c + 1) * LANES])
    return outs[0] if len(outs) == 1 else jnp.concatenate(outs, axis=-1)


def _inproj_kernel(x_ref, shift_ref, scale_ref, g_ref, w_ref, qg_ref, kg_ref, lb_ref, y_ref, gates_ref):
    h = (_rms_rows(x_ref[...], g_ref[...]) * (1.0 + scale_ref[...]) + shift_ref[...]).astype(BF16)

    def proj(c0, c1):
        return _dot(h, w_ref[:, c0:c1])

    y_ref[:, COL_AQ:COL_AK] = _head_rms(proj(COL_AQ, COL_AK), qg_ref[...])
    t = proj(COL_AK, COL_WQ)
    y_ref[:, COL_AK:COL_AV] = _head_rms(t[:, :LANES], kg_ref[...])
    y_ref[:, COL_AV:COL_WQ] = t[:, LANES:]
    y_ref[:, COL_WQ:COL_HQ] = proj(COL_WQ, COL_HQ)
    t = proj(COL_HQ, COL_HF)
    y_ref[:, COL_HQ:COL_HF] = t * _sigmoid(t)
    for dirn in range(2):
        c0 = COL_HF + dirn * HG_W
        lb = lb_ref[:, dirn * HG_W:(dirn + 1) * HG_W]
        y_ref[:, c0:c0 + HG_W] = lb + (1.0 - lb) * _sigmoid(proj(c0, c0 + HG_W))
    y_ref[:, COL_HI:COL_GATES] = proj(COL_HI, COL_GATES)
    for k in range(GATES_W // D_MODEL):
        c0 = COL_GATES + k * D_MODEL
        gates_ref[:, k * D_MODEL:(k + 1) * D_MODEL] = _sigmoid(proj(c0, c0 + D_MODEL)).astype(BF16)


def _inproj_call(x, modv, normv, w_in_b, qk_norm_g, lb_all, layer, row_fn, tm):
    n = x.shape[0]
    qg = jnp.tile(qk_norm_g[layer, 0].astype(F32), (COL_AK - COL_AQ) // HEAD_DIM).reshape(1, COL_AK - COL_AQ)
    kg = jnp.tile(qk_norm_g[layer, 1].astype(F32), LANES // HEAD_DIM).reshape(1, LANES)
    return pl.pallas_call(
        _inproj_kernel,
        grid=(n // tm,),
        in_specs=[
            pl.BlockSpec((tm, D_MODEL), lambda i: (i, 0)),
            _mod_spec(layer, 3, row_fn), _mod_spec(layer, 4, row_fn),
            pl.BlockSpec((None, 1, D_MODEL), lambda i: (layer * 3 + 1, 0, 0)),
            _resident((None, D_MODEL, IN_W), lambda i: (layer, 0, 0)),
            pl.BlockSpec((1, COL_AK - COL_AQ), lambda i: (0, 0)),
            pl.BlockSpec((1, LANES), lambda i: (0, 0)),
            pl.BlockSpec((None, 1, 2 * HG_W), lambda i: (layer, 0, 0)),
        ],
        out_specs=[pl.BlockSpec((tm, Y_W), lambda i: (i, 0)),
                   pl.BlockSpec((tm, GATES_W), lambda i: (i, 0))],
        out_shape=[jax.ShapeDtypeStruct((n, Y_W), F32), jax.ShapeDtypeStruct((n, GATES_W), BF16)],
        compiler_params=_params(("parallel",)),
        name="inproj",
    )(x, modv, modv, normv, w_in_b, qg, kg, lb_all)


def _rope(x, cos, sin_signed):
    lane = lax.broadcasted_iota(jnp.int32, (1, LANES), 1)
    first = (lane & 31) < 16
    swapped = jnp.where(first, pltpu.roll(x, LANES - 16, 1), pltpu.roll(x, 16, 1))
    return x * cos + swapped * sin_signed


def _attn_kernel(*refs, tq, seq, past, use_rope, use_sink, window, sink_off):
    it = iter(refs)
    sink_ref = next(it) if use_sink else None
    q_refs = (next(it), next(it))
    k_ref, v_ref = next(it), next(it)
    ck_ref = cv_ref = cos_ref = sin_ref = None
    if past:
        ck_ref, cv_ref = next(it), next(it)
    if use_rope:
        cos_ref, sin_ref = next(it), next(it)
    o_ref = next(it)
    kscr = next(it)
    vscr = next(it)

    qi = pl.program_id(1)
    lane = lax.broadcasted_iota(jnp.int32, (1, LANES), 1)
    lo = lane < HEAD_DIM
    one_lo = jnp.where(lane == 0, 1.0, 0.0)
    one_hi = jnp.where(lane == HEAD_DIM, 1.0, 0.0)
    rows = min(seq, 256)

    def put(x, xv, r0):
        sl = pl.ds(r0, x.shape[0])
        xr = pltpu.roll(x, HEAD_DIM, 1)
        kscr[0, sl, :] = jnp.where(lo, x, xr).astype(BF16)
        kscr[1, sl, :] = jnp.where(lo, xr, x).astype(BF16)
        vr = pltpu.roll(xv, HEAD_DIM, 1)
        vscr[0, sl, :] = jnp.where(lo, xv, one_hi).astype(BF16)
        vscr[1, sl, :] = jnp.where(lo, one_lo, vr).astype(BF16)
        vscr[2, sl, :] = jnp.where(lo, vr, one_hi).astype(BF16)
        vscr[3, sl, :] = jnp.where(lo, one_lo, xv).astype(BF16)

    @pl.when(qi == 0)
    def _build():
        if past:
            put(ck_ref[...], cv_ref[...], 0)

        def body(c, carry):
            r = pl.multiple_of(c * rows, rows)
            k = k_ref[pl.ds(r, rows), :]
            if use_rope:
                k = _rope(k, cos_ref[pl.ds(r, rows), :], sin_ref[pl.ds(r, rows), :])
            put(k, v_ref[pl.ds(r, rows), :], past + r)
            return carry

        lax.fori_loop(0, seq // rows, body, 0)

    ts = min(ATT_SUB, tq)
    q0 = pl.multiple_of(qi * tq, tq)
    nk = past + seq
    q_scale = (HEAD_DIM ** -0.5) * LOG2E

    def unit_segments(h):
        if not window:
            return [(slice(0, nk), None)]
        span = ts + 2 * WINDOW
        t0 = q0 + h * ts
        start = pl.multiple_of(jnp.clip(t0 - WINDOW, 0, seq - span), WINDOW)
        t_pos = t0 + lax.broadcasted_iota(jnp.int32, (ts, 1), 0)
        s_pos = start + lax.broadcasted_iota(jnp.int32, (1, span), 1)
        band_ok = jnp.abs(t_pos - s_pos) <= WINDOW
        return [(slice(0, past), None), (pl.ds(past + start, span), band_ok)]

    def score_phase(g, h, segments):
        xs = []
        for p in range(2):
            q = q_refs[g][h * ts:(h + 1) * ts, p * LANES:(p + 1) * LANES]
            if use_rope:
                tsl = pl.ds(q0 + h * ts, ts)
                q = _rope(q, cos_ref[tsl, :], sin_ref[tsl, :])
            xs.append(q * q_scale)
        q4 = jnp.concatenate([jnp.where(lo, xs[0], 0.0), jnp.where(lo, xs[1], 0.0),
                              jnp.where(lo, 0.0, xs[0]), jnp.where(lo, 0.0, xs[1])], axis=0).astype(BF16)
        scores = []
        for ksl, ok in segments:
            s = _dot_nt(q4, kscr[g, ksl, :])
            if ok is not None:
                s = jnp.where(jnp.concatenate([ok] * 4, axis=0), s, NEG_INF)
            scores.append(s)
        return scores

    def softmax_phase(g, scores):
        probs = [[] for _ in scores]
        extra = []
        for blk, head in enumerate((0, 2, 1, 3)):
            rsl = slice(blk * ts, (blk + 1) * ts)
            m = None
            for s in scores:
                ms = jnp.max(s[rsl], axis=-1, keepdims=True)
                m = ms if m is None else jnp.maximum(m, ms)
            if use_sink:
                sk = sink_ref[sink_off + g * 4 + head] * LOG2E
                m = jnp.maximum(m, sk)
                extra.append(jnp.exp2(sk - m))
            for i, s in enumerate(scores):
                probs[i].append(jnp.exp2(s[rsl] - m).astype(BF16))
        return probs, extra

    def value_phase(g, probs, segments):
        acc = [None, None]
        for i, (ksl, _) in enumerate(segments):
            for par in range(2):
                e = jnp.concatenate(probs[i][2 * par:2 * par + 2], axis=0)
                pv = _dot(e, vscr[2 * g + par, ksl, :])
                acc[par] = pv if acc[par] is None else acc[par] + pv
        return acc

    def output_phase(g, h, acc, extra):
        den = [jnp.sum(jnp.where(lane == HEAD_DIM, acc[0], 0.0), axis=-1, keepdims=True),
               jnp.sum(jnp.where(lane == 0, acc[1], 0.0), axis=-1, keepdims=True)]
        if use_sink:
            den[0] = den[0] + jnp.concatenate(extra[0:2], axis=0)
            den[1] = den[1] + jnp.concatenate(extra[2:4], axis=0)
        outs = [acc[par] * (1.0 / den[par]) for par in range(2)]
        for p in range(2):
            rsl = slice(p * ts, (p + 1) * ts)
            o_pair = jnp.where(lo, outs[0][rsl], outs[1][rsl])
            o_ref[h * ts:(h + 1) * ts, g * 256 + p * LANES:g * 256 + (p + 1) * LANES] = o_pair.astype(o_ref.dtype)

    units = [(g, h) for h in range(tq // ts) for g in range(ATT_KV)]
    segs = {h: unit_segments(h) for h in range(tq // ts)}
    scores = [score_phase(g, h, segs[h]) for g, h in units]
    accs, extras = [], []
    for (g, h), sc in zip(units, scores):
        probs, extra = softmax_phase(g, sc)
        accs.append(value_phase(g, probs, segs[h]))
        extras.append(extra)
    for (g, h), acc, extra in zip(units, accs, extras):
        output_phase(g, h, acc, extra)


def _attn_call(y, nb, seq, qcol, kcol, vcol, *, cache_k=None, cache_v=None, layer=0, rope=None,
               sink=None, window=False):
    tq = min(ATT_TQ, seq)
    nq = seq // tq
    past = 0 if cache_k is None else cache_k.shape[2]
    use_rope = rope is not None
    use_sink = sink is not None
    in_specs, args = [], []
    if use_sink:
        in_specs.append(pl.BlockSpec(memory_space=pltpu.SMEM))
        args.append(sink.reshape(-1).astype(F32))
    for g in range(ATT_KV):
        in_specs.append(pl.BlockSpec((tq, 256), lambda b, qi, g=g: (b * nq + qi, qcol // 256 + g)))
        args.append(y)
    in_specs.append(pl.BlockSpec((seq, LANES), lambda b, qi: (b, kcol // LANES)))
    in_specs.append(pl.BlockSpec((seq, LANES), lambda b, qi: (b, vcol // LANES)))
    args += [y, y]
    if past:
        cshape = cache_k.shape[:3] + (LANES,)
        in_specs += [pl.BlockSpec((None, None, past, LANES), lambda b, qi: (b, layer, 0, 0))] * 2
        args += [cache_k.reshape(cshape), cache_v.reshape(cshape)]
    if use_rope:
        in_specs += [pl.BlockSpec((seq, LANES), lambda b, qi: (0, 0))] * 2
        args += list(rope)
    kern = functools.partial(_attn_kernel, tq=tq, seq=seq, past=past, use_rope=use_rope,
                             use_sink=use_sink, window=window, sink_off=layer * 8)
    return pl.pallas_call(
        kern,
        grid=(nb, nq),
        in_specs=in_specs,
        out_specs=pl.BlockSpec((tq, BRANCH_W), lambda b, qi: (b * nq + qi, 0)),
        out_shape=jax.ShapeDtypeStruct((nb * seq, BRANCH_W), BF16),
        scratch_shapes=[pltpu.VMEM((2, past + seq, LANES), BF16), pltpu.VMEM((4, past + seq, LANES), BF16)],
        compiler_params=_params(("parallel", "arbitrary")),
        name="win" if window or use_sink else "att",
    )(*args)


def _pair_table(fwd):
    import numpy as np
    C = HG_CHUNK
    t = np.arange(C)[:, None]
    s = np.arange(C)[None, :]
    seen = (s <= t) if fwd else (s >= t)
    tab = np.full((C, C), -1, np.int32)
    hh, idx = C // 2, (C // HG_SUB).bit_length() - 2
    while hh >= HG_SUB:
        tab = np.where((t // (2 * hh) == s // (2 * hh)) & seen, idx, tab)
        hh, idx = hh // 2, idx - 1
    tab = np.where((t // HG_SUB == s // HG_SUB), np.where(seen, PAIR_SUB + s % HG_SUB, -1), tab)
    return tab.astype(np.int32)


def _hgrn_unit(q, k, b, v, st, sub_masks, level_masks, fwd):
    C = HG_CHUNK
    G = C // HG_SUB
    vb = v.astype(BF16)

    b3 = b.reshape(G, HG_SUB, HG_DK)
    q3 = q.reshape(G, HG_SUB, HG_DK)
    c3 = (jnp.log2(k) - b).reshape(G, HG_SUB, HG_DK)
    a3 = jnp.zeros((G, HG_SUB, C), F32)
    for jj in range(HG_SUB):
        kdec = jnp.exp2(b3 + c3[:, jj:jj + 1, :])
        score = jnp.sum(q3 * kdec, axis=-1, keepdims=True)
        a3 = jnp.where(sub_masks[jj], score, a3)
    a_mat = a3.reshape(C, C)

    tcol = lax.broadcasted_iota(jnp.int32, (C, 1), 0)
    hh, idx = HG_SUB, 0
    while hh < C:
        grp = 2 * hh
        later = (tcol & (grp - 1)) >= hh
        is_q = later if fwd else jnp.logical_not(later)
        bg = b.reshape(C // grp, grp, HG_DK)
        bnd = bg[:, hh - 1:hh, :] if fwd else bg[:, hh:hh + 1, :]
        bnd = jnp.broadcast_to(bnd, (C // grp, grp, HG_DK)).reshape(C, HG_DK)
        w = (jnp.where(is_q, q, k) * jnp.exp2(jnp.where(is_q, b - bnd, bnd - b))).astype(BF16)
        a_mat = jnp.where(level_masks[idx], _dot_nt(w, w), a_mat)
        hh, idx = grp, idx + 1
    out = _dot(a_mat.astype(BF16), vb)

    out = out + _dot_nt((q * jnp.exp2(b)).astype(BF16), st.astype(BF16))
    blast = b[C - 1:C, :] if fwd else b[0:1, :]
    kh = (k * jnp.exp2(blast - b)).astype(BF16)
    st_new = st * jnp.exp2(blast) + _dot_tn(vb, kh)
    return out, st_new


def _hgrn_kernel(*refs, nc, has_init, emit_state):
    it = iter(refs)
    io = [(next(it), next(it), next(it)) for _ in range(2)]
    pair_ref = next(it)
    s0_ref = next(it) if has_init else None
    o_refs = (next(it), next(it))
    sfin_ref = next(it) if emit_state else None
    st_scr = next(it)

    j = pl.program_id(1)
    C = HG_CHUNK

    @pl.when(j == 0)
    def _():
        for dirn in range(2):
            for h in range(HG_HEADS):
                if has_init:
                    st_scr[dirn, h] = s0_ref[dirn, h].T
                else:
                    st_scr[dirn, h] = jnp.zeros((HG_DK, HG_DK), F32)

    ti = lax.broadcasted_iota(jnp.int32, (C, C), 0)
    si = lax.broadcasted_iota(jnp.int32, (C, C), 1)
    for dirn in range(2):
        fwd = dirn == 0
        q_ref, f_ref, v_ref = io[dirn]
        f = f_ref[...]
        l2 = jnp.log2(jnp.maximum(f, TINY))
        tri = jnp.where((ti >= si) if fwd else (ti <= si), 1.0, 0.0).astype(BF16)
        l_hi = l2.astype(BF16)
        r1 = l2 - l_hi.astype(F32)
        l_mid = r1.astype(BF16)
        l_lo = (r1 - l_mid.astype(F32)).astype(BF16)
        b_all = _dot(tri, l_hi) + _dot(tri, l_mid) + _dot(tri, l_lo)
        pair = pair_ref[dirn]
        pair3 = pair.reshape(C // HG_SUB, HG_SUB, C)
        sub_masks = [pair3 == PAIR_SUB + jj for jj in range(HG_SUB)]
        level_masks = [pair == idx for idx in range((C // HG_SUB).bit_length() - 1)]
        for h in range(HG_HEADS):
            sl = slice(h * HG_DK, (h + 1) * HG_DK)
            out, st_new = _hgrn_unit(q_ref[:, sl], jnp.maximum(1.0 - f[:, sl], 0.0), b_all[:, sl], v_ref[:, sl],
                                     st_scr[dirn, h], sub_masks, level_masks, fwd)
            o_refs[dirn][:, sl] = out
            st_scr[dirn, h] = st_new
            if emit_state:
                @pl.when(j == nc - 1)
                def _():
                    sfin_ref[dirn, h] = st_new.T


def _hgrn_call(y, nb, seq, layer, state0=None, emit_state=False):
    nc = seq // HG_CHUNK
    has_init = state0 is not None
    in_specs, args = [], []
    for dirn in range(2):
        rowblk = (lambda n, j: n * nc + j) if dirn == 0 else (lambda n, j: n * nc + nc - 1 - j)
        for col in (COL_HQ, COL_HF + dirn * HG_W, COL_HI):
            in_specs.append(pl.BlockSpec((HG_CHUNK, HG_W), lambda n, j, rowblk=rowblk, col=col: (rowblk(n, j), col // HG_W)))
            args.append(y)
    in_specs.append(pl.BlockSpec((2, HG_CHUNK, HG_CHUNK), lambda n, j: (0, 0, 0)))
    args.append(jnp.asarray([_pair_table(True), _pair_table(False)]))
    if has_init:
        in_specs.append(pl.BlockSpec((None, None, 2, HG_HEADS, HG_DK, HG_DK), lambda n, j: (n, layer, 0, 0, 0, 0)))
        args.append(state0)
    out_specs = [pl.BlockSpec((HG_CHUNK, HG_W), lambda n, j: (n * nc + j, 0)),
                 pl.BlockSpec((HG_CHUNK, HG_W), lambda n, j: (n * nc + nc - 1 - j, 0))]
    out_shape = [jax.ShapeDtypeStruct((nb * seq, HG_W), F32)] * 2
    if emit_state:
        out_specs.append(pl.BlockSpec((None, 2, HG_HEADS, HG_DK, HG_DK), lambda n, j: (n, 0, 0, 0, 0)))
        out_shape.append(jax.ShapeDtypeStruct((nb, 2, HG_HEADS, HG_DK, HG_DK), F32))
    res = pl.pallas_call(
        functools.partial(_hgrn_kernel, nc=nc, has_init=has_init, emit_state=emit_state),
        grid=(nb, nc),
        in_specs=in_specs,
        out_specs=out_specs,
        out_shape=out_shape,
        scratch_shapes=[pltpu.VMEM((2, HG_HEADS, HG_DK, HG_DK), F32)],
        compiler_params=_params(("parallel", "arbitrary")),
        name="hgrn",
    )(*args)
    return (res[0], res[1], res[2]) if emit_state else (res[0], res[1], None)


def _merge_kernel(*refs, final):
    (oa_ref, ow_ref, of_ref, ob_ref, hg_ref, gates_ref, x_ref, gate_ref, hgn_ref, wb_ref, wo_ref,
     shift2_ref, scale2_ref, gate2_ref, g2_ref, wgu_ref, wd_ref) = refs[:17]
    fg_ref = refs[17] if final else None
    o_ref = refs[-1]
    o = of_ref[...] + ob_ref[...]
    hg = hg_ref[...]
    hgn = hgn_ref[...]
    parts = []
    for h in range(HG_HEADS):
        sl = slice(h * HG_DK, (h + 1) * HG_DK)
        g = hg[:, sl]
        parts.append(_rms_rows(o[:, sl], hgn) * (g * _sigmoid(g)))
    o_hg = jnp.concatenate(parts, axis=-1).astype(BF16)
    branch = (oa_ref[...], o_hg, ow_ref[...])
    merged = None
    for k in range(3):
        term = gates_ref[:, k * D_MODEL:(k + 1) * D_MODEL].astype(F32) * _dot(branch[k], wb_ref[k])
        merged = term if merged is None else merged + term
    yv = _dot(merged.astype(BF16), wo_ref[...])
    x_mid = x_ref[...] + gate_ref[...] * yv
    _ffn_block(x_mid, shift2_ref[...], scale2_ref[...], gate2_ref[...], g2_ref[...], wgu_ref, wd_ref, o_ref,
               fg_ref[...] if final else None)


def _merge_call(x, y, gates, o_att, o_win, o_f, o_b, modv, normv, hg_norm_g, w_branch_b, w_out_b,
                w_ffn_in_b, w_ffn_out_b, layer, row_fn, tm, final_g=None):
    n = x.shape[0]
    final = final_g is not None
    row = lambda i: (i, 0)
    in_specs = [
        pl.BlockSpec((tm, BRANCH_W), row),
        pl.BlockSpec((tm, BRANCH_W), row),
        pl.BlockSpec((tm, HG_W), row),
        pl.BlockSpec((tm, HG_W), row),
        pl.BlockSpec((tm, HG_W), lambda i: (i, COL_HG // HG_W)),
        pl.BlockSpec((tm, GATES_W), row),
        pl.BlockSpec((tm, D_MODEL), row),
        _mod_spec(layer, 5, row_fn),
        pl.BlockSpec((1, HG_DK), lambda i: (0, 0)),
        _resident((None, 3, BRANCH_W, D_MODEL), lambda i: (layer, 0, 0, 0)),
        _resident((None, D_MODEL, D_MODEL), lambda i: (layer, 0, 0)),
    ] + _ffn_specs(layer, 1, row_fn)
    args = [o_att, o_win, o_f, o_b, y, gates, x, modv, hg_norm_g[layer].reshape(1, HG_DK).astype(F32),
            w_branch_b, w_out_b, modv, modv, modv, normv, w_ffn_in_b, w_ffn_out_b]
    if final:
        in_specs.append(pl.BlockSpec((1, D_MODEL), lambda i: (0, 0)))
        args.append(final_g.reshape(1, D_MODEL))
    return pl.pallas_call(
        functools.partial(_merge_kernel, final=final),
        grid=(n // tm,),
        in_specs=in_specs,
        out_specs=pl.BlockSpec((tm, D_MODEL), row),
        out_shape=jax.ShapeDtypeStruct((n, D_MODEL), F32),
        compiler_params=_params(("parallel",)),
        name="merge_ffn",
    )(*args)


def _rope_tables(n_lat):
    t = jnp.arange(n_lat, dtype=jnp.int32)
    row = (t // GRID_W).astype(F32)
    col = (t % GRID_W).astype(F32)
    axis_dim = HEAD_DIM // 2
    inv = ROPE_THETA ** (-jnp.arange(0, axis_dim, 2, dtype=F32) / axis_dim)
    ang_r = row[:, None] * inv
    ang_c = col[:, None] * inv
    cos64 = jnp.concatenate([jnp.cos(ang_r), jnp.cos(ang_r), jnp.cos(ang_c), jnp.cos(ang_c)], axis=-1)
    sin64 = jnp.concatenate([-jnp.sin(ang_r), jnp.sin(ang_r), -jnp.sin(ang_c), jnp.sin(ang_c)], axis=-1)
    return jnp.tile(cos64, (1, 2)), jnp.tile(sin64, (1, 2))


def kernel(x_prompt, x_sample, cache_k_attn, cache_v_attn, cache_k_win, cache_v_win, state_hgrn, c, c_ctx,
           w_mod, b_mod, norm_g, w_ffn_in, w_ffn_out, w_in, qk_norm_g, lower_bounds, hg_norm_g, sink_logit,
           w_branch, w_out, final_norm_g):
    batch, seq, _ = x_prompt.shape
    dec_batch, dec_seq, _ = x_sample.shape
    depth = w_mod.shape[0]

    cond = jnp.zeros((COND_ROWS, D_MODEL), F32).at[0].set(c_ctx).at[1:1 + dec_batch].set(c)
    modv = _mod_call(cond, w_mod, b_mod).reshape(depth * COND_ROWS * N_MOD, 1, D_MODEL)
    lb_all = _lb_call(lower_bounds).reshape(depth, 1, 2 * HG_W)
    normv = norm_g.astype(F32).reshape(depth * 3, 1, D_MODEL)
    w_ffn_in_b = w_ffn_in.astype(BF16)
    w_ffn_out_b = w_ffn_out.astype(BF16)
    w_in_b = w_in.astype(BF16)
    w_branch_b = w_branch.astype(BF16)
    w_out_b = w_out.astype(BF16)
    rope = _rope_tables(dec_seq)

    def run(x, nb, s, latent):
        n = nb * s

        def rows_of(tm):
            return (lambda i: 1 + i // (s // tm)) if latent else (lambda i: 0)

        tm = min(DENSE_TM, s if latent else n)
        tm_ffn = min(FFN_TM, s if latent else n)
        x = x.reshape(n, D_MODEL)
        ctx_out = []
        for l in range(depth):
            x = _ffn_call(x, modv, normv, w_ffn_in_b, w_ffn_out_b, l, rows_of(tm_ffn), tm_ffn)
            y, gates = _inproj_call(x, modv, normv, w_in_b, qk_norm_g, lb_all, l, rows_of(tm), tm)
            if latent:
                o_att = _attn_call(y, nb, s, COL_AQ, COL_AK, COL_AV, cache_k=cache_k_attn,
                                   cache_v=cache_v_attn, layer=l, rope=rope)
                o_win = _attn_call(y, nb, s, COL_WQ, COL_WK, COL_WV, cache_k=cache_k_win,
                                   cache_v=cache_v_win, layer=l, rope=rope, sink=sink_logit, window=True)
                o_f, o_b, _ = _hgrn_call(y, nb, s, l, state0=state_hgrn)
            else:
                o_att = _attn_call(y, nb, s, COL_AQ, COL_AK, COL_AV)
                o_win = _attn_call(y, nb, s, COL_WQ, COL_WK, COL_WV, layer=l, sink=sink_logit)
                o_f, o_b, s_fin = _hgrn_call(y, nb, s, l, emit_state=True)
                kv = lambda col: y[:, col:col + LANES].reshape(nb, s, ATT_KV, HEAD_DIM)
                ctx_out.append((kv(COL_AK), kv(COL_AV), kv(COL_WK), kv(COL_WV), s_fin))
            x = _merge_call(x, y, gates, o_att, o_win, o_f, o_b, modv, normv, hg_norm_g, w_branch_b, w_out_b,
                            w_ffn_in_b, w_ffn_out_b, l, rows_of(tm), tm,
                            final_g=final_norm_g.astype(F32) if l == depth - 1 else None)
        return x.reshape(nb, s, D_MODEL), ctx_out

    y_prompt, ctx_out = run(x_prompt, batch, seq, False)
    y_sample, _ = run(x_sample, dec_batch, dec_seq, True)
    stack = lambda k: jnp.stack([cx[k] for cx in ctx_out], axis=1)
    return (y_prompt, y_sample, stack(0), stack(1), stack(2), stack(3), stack(4))
```

```python
import functools

import jax
import jax.numpy as jnp
from jax import lax
from jax.experimental import pallas as pl
from jax.experimental.pallas import tpu as pltpu

F32 = jnp.float32
BF16 = jnp.bfloat16

D_MODEL = 1024
GRID_W = 64
HEAD_DIM = 64
ATT_KV = 2
WINDOW = 128
HG_HEADS = 4
HG_DK = 128
HG_W = HG_HEADS * HG_DK
BRANCH_W = 512
D_FF = 2816
ROPE_THETA = 10000.0
EPS = 1e-6
NEG_INF = -1e30
TINY = 1e-30
LOG2E = 1.4426950408889634
N_MOD = 9
IN_W = 7168

LANES = 128
SUBLANES = 8

COND_ROWS = 16
DENSE_TM = 512
FFN_TM = 1024
FFN_TF = 256
ATT_TQ = 256
ATT_SUB = 128
HG_CHUNK = 128
HG_STEP_CHUNKS = 2
HG_SUB = SUBLANES
PAIR_SUB = 16
VMEM_LIMIT = 56 * 1024 * 1024

COL_AQ, COL_AK, COL_AV = 0, 512, 640
COL_WQ, COL_WK, COL_WV = 768, 1280, 1408
COL_HQ, COL_HF, COL_HI, COL_HG = 1536, 2048, 3072, 3584
COL_GATES = 4096
Y_W = COL_GATES
GATES_W = IN_W - COL_GATES


def _params(sem):
    return pltpu.CompilerParams(dimension_semantics=sem, vmem_limit_bytes=VMEM_LIMIT)


def _resident(block_shape, index_map):
    return pl.BlockSpec(block_shape, index_map, pipeline_mode=pl.Buffered(1))


def _sigmoid(x):
    return 1.0 / (1.0 + jnp.exp(-x))


def _dot(a, b):
    return jnp.dot(a, b, preferred_element_type=F32)


def _dot_nt(a, b):
    return lax.dot_general(a, b, (((1,), (1,)), ((), ())), preferred_element_type=F32)


def _dot_tn(a, b):
    return lax.dot_general(a, b, (((0,), (0,)), ((), ())), preferred_element_type=F32)


def _rms_rows(x, gain):
    return x * lax.rsqrt(jnp.mean(x * x, axis=-1, keepdims=True) + EPS) * gain


def _mod_kernel(c_ref, w_ref, b_ref, o_ref):
    c = c_ref[...]
    h = (c * _sigmoid(c)).astype(BF16)
    o_ref[...] = _dot(h, w_ref[...].astype(BF16)) + b_ref[...]


def _mod_call(cond, w_mod, b_mod):
    depth, d, nm = w_mod.shape
    tn = nm // 8
    return pl.pallas_call(
        _mod_kernel,
        grid=(depth, nm // tn),
        in_specs=[pl.BlockSpec((COND_ROWS, d), lambda l, j: (0, 0)),
                  pl.BlockSpec((None, d, tn), lambda l, j: (l, 0, j)),
                  pl.BlockSpec((None, 1, tn), lambda l, j: (l, 0, j))],
        out_specs=pl.BlockSpec((None, COND_ROWS, tn), lambda l, j: (l, 0, j)),
        out_shape=jax.ShapeDtypeStruct((depth, COND_ROWS, nm), F32),
        compiler_params=_params(("parallel", "parallel")),
        name="mod",
    )(cond, w_mod, b_mod.reshape(depth, 1, nm))


def _lb_kernel(x_ref, o_ref):
    x = x_ref[...]
    m = jnp.max(x, axis=0, keepdims=True)
    e = jnp.exp(x - m)
    s = e / jnp.sum(e, axis=0, keepdims=True)
    acc = jnp.zeros_like(s[0:1])
    for l in range(x.shape[0]):
        acc = acc + s[l:l + 1]
        o_ref[l:l + 1, :] = acc - s[0:1]


def _lb_call(lower_bounds):
    depth = lower_bounds.shape[0]
    x = lower_bounds.reshape(depth, -1).astype(F32)
    return pl.pallas_call(_lb_kernel, out_shape=jax.ShapeDtypeStruct(x.shape, F32), name="lb")(x)


def _ffn_block(x, shift, scale, gate, g, wgu_ref, wd_ref, o_ref, final_g):
    h = (_rms_rows(x, g) * (1.0 + scale) + shift).astype(BF16)
    for c in range(D_FF // FFN_TF):
        a = _dot(h, wgu_ref[:, c * FFN_TF:(c + 1) * FFN_TF])
        u = _dot(h, wgu_ref[:, D_FF + c * FFN_TF:D_FF + (c + 1) * FFN_TF])
        act = (a * _sigmoid(a) * u).astype(BF16)
        part = _dot(act, wd_ref[c * FFN_TF:(c + 1) * FFN_TF, :])
        if c == 0:
            o_ref[...] = part
        else:
            o_ref[...] += part
    out = x + 0.5 * gate * o_ref[...]
    if final_g is not None:
        out = _rms_rows(out, final_g)
    o_ref[...] = out


def _ffn_kernel(x_ref, shift_ref, scale_ref, gate_ref, g_ref, wgu_ref, wd_ref, o_ref):
    _ffn_block(x_ref[...], shift_ref[...], scale_ref[...], gate_ref[...], g_ref[...], wgu_ref, wd_ref, o_ref, None)


def _mod_spec(layer, k, row_fn):
    return pl.BlockSpec((None, 1, D_MODEL), lambda i: ((layer * COND_ROWS + row_fn(i)) * N_MOD + k, 0, 0))


def _ffn_specs(layer, which, row_fn):
    k0 = 3 * (2 * which)
    return [
        _mod_spec(layer, k0, row_fn), _mod_spec(layer, k0 + 1, row_fn), _mod_spec(layer, k0 + 2, row_fn),
        pl.BlockSpec((None, 1, D_MODEL), lambda i: (layer * 3 + 2 * which, 0, 0)),
        _resident((None, None, D_MODEL, 2 * D_FF), lambda i: (layer, which, 0, 0)),
        _resident((None, None, D_FF, D_MODEL), lambda i: (layer, which, 0, 0)),
    ]


def _ffn_call(x, modv, normv, w_in_b, w_out_b, layer, row_fn, tm):
    n = x.shape[0]
    return pl.pallas_call(
        _ffn_kernel,
        grid=(n // tm,),
        in_specs=[pl.BlockSpec((tm, D_MODEL), lambda i: (i, 0))] + _ffn_specs(layer, 0, row_fn),
        out_specs=pl.BlockSpec((tm, D_MODEL), lambda i: (i, 0)),
        out_shape=jax.ShapeDtypeStruct((n, D_MODEL), F32),
        compiler_params=_params(("parallel",)),
        name="ffn",
    )(x, modv, modv, modv, normv, w_in_b, w_out_b)


def _head_rms(y, gain):
    lane = lax.broadcasted_iota(jnp.int32, (1, LANES), 1)
    lo = lane < HEAD_DIM
    outs = []
    for c in range(y.shape[1] // LANES):
        blk = y[:, c * LANES:(c + 1) * LANES]
        sq = blk * blk
        s_lo = jnp.sum(jnp.where(lo, sq, 0.0), axis=-1, keepdims=True)
        s_hi = jnp.sum(jnp.where(lo, 0.0, sq), axis=-1, keepdims=True)
        ms = jnp.where(lo, s_lo, s_hi) * (1.0 / HEAD_DIM)
        outs.append(blk * lax.rsqrt(ms + EPS) * gain[:, c * LANES:(c + 1) * LANES])
    return outs[0] if len(outs) == 1 else jnp.concatenate(outs, axis=-1)


def _inproj_kernel(x_ref, shift_ref, scale_ref, g_ref, w_ref, qg_ref, kg_ref, lb_ref, y_ref, gates_ref):
    h = (_rms_rows(x_ref[...], g_ref[...]) * (1.0 + scale_ref[...]) + shift_ref[...]).astype(BF16)

    def proj(c0, c1):
        return _dot(h, w_ref[:, c0:c1])

    y_ref[:, COL_AQ:COL_AK] = _head_rms(proj(COL_AQ, COL_AK), qg_ref[...])
    t = proj(COL_AK, COL_WQ)
    y_ref[:, COL_AK:COL_AV] = _head_rms(t[:, :LANES], kg_ref[...])
    y_ref[:, COL_AV:COL_WQ] = t[:, LANES:]
    y_ref[:, COL_WQ:COL_HQ] = proj(COL_WQ, COL_HQ)
    t = proj(COL_HQ, COL_HF)
    y_ref[:, COL_HQ:COL_HF] = t * _sigmoid(t)
    for dirn in range(2):
        c0 = COL_HF + dirn * HG_W
        lb = lb_ref[:, dirn * HG_W:(dirn + 1) * HG_W]
        y_ref[:, c0:c0 + HG_W] = lb + (1.0 - lb) * _sigmoid(proj(c0, c0 + HG_W))
    y_ref[:, COL_HI:COL_GATES] = proj(COL_HI, COL_GATES)
    for k in range(GATES_W // D_MODEL):
        c0 = COL_GATES + k * D_MODEL
        gates_ref[:, k * D_MODEL:(k + 1) * D_MODEL] = _sigmoid(proj(c0, c0 + D_MODEL)).astype(BF16)


def _inproj_call(x, modv, normv, w_in_b, qk_norm_g, lb_all, layer, row_fn, tm):
    n = x.shape[0]
    qg = jnp.tile(qk_norm_g[layer, 0].astype(F32), (COL_AK - COL_AQ) // HEAD_DIM).reshape(1, COL_AK - COL_AQ)
    kg = jnp.tile(qk_norm_g[layer, 1].astype(F32), LANES // HEAD_DIM).reshape(1, LANES)
    return pl.pallas_call(
        _inproj_kernel,
        grid=(n // tm,),
        in_specs=[
            pl.BlockSpec((tm, D_MODEL), lambda i: (i, 0)),
            _mod_spec(layer, 3, row_fn), _mod_spec(layer, 4, row_fn),
            pl.BlockSpec((None, 1, D_MODEL), lambda i: (layer * 3 + 1, 0, 0)),
            _resident((None, D_MODEL, IN_W), lambda i: (layer, 0, 0)),
            pl.BlockSpec((1, COL_AK - COL_AQ), lambda i: (0, 0)),
            pl.BlockSpec((1, LANES), lambda i: (0, 0)),
            pl.BlockSpec((None, 1, 2 * HG_W), lambda i: (layer, 0, 0)),
        ],
        out_specs=[pl.BlockSpec((tm, Y_W), lambda i: (i, 0)),
                   pl.BlockSpec((tm, GATES_W), lambda i: (i, 0))],
        out_shape=[jax.ShapeDtypeStruct((n, Y_W), F32), jax.ShapeDtypeStruct((n, GATES_W), BF16)],
        compiler_params=_params(("parallel",)),
        name="inproj",
    )(x, modv, modv, normv, w_in_b, qg, kg, lb_all)


def _rope(x, cos, sin_signed):
    lane = lax.broadcasted_iota(jnp.int32, (1, LANES), 1)
    first = (lane & 31) < 16
    swapped = jnp.where(first, pltpu.roll(x, LANES - 16, 1), pltpu.roll(x, 16, 1))
    return x * cos + swapped * sin_signed


def _attn_kernel(*refs, tq, seq, past, use_rope, use_sink, window, sink_off):
    it = iter(refs)
    sink_ref = next(it) if use_sink else None
    q_refs = (next(it), next(it))
    k_ref, v_ref = next(it), next(it)
    ck_ref = cv_ref = cos_ref = sin_ref = None
    if past:
        ck_ref, cv_ref = next(it), next(it)
    if use_rope:
        cos_ref, sin_ref = next(it), next(it)
    o_ref = next(it)
    kscr = next(it)
    vscr = next(it)

    qi = pl.program_id(1)
    lane = lax.broadcasted_iota(jnp.int32, (1, LANES), 1)
    lo = lane < HEAD_DIM
    one_lo = jnp.where(lane == 0, 1.0, 0.0)
    one_hi = jnp.where(lane == HEAD_DIM, 1.0, 0.0)
    rows = min(seq, 256)

    def put(x, xv, r0):
        sl = pl.ds(r0, x.shape[0])
        xr = pltpu.roll(x, HEAD_DIM, 1)
        kscr[0, sl, :] = jnp.where(lo, x, xr).astype(BF16)
        kscr[1, sl, :] = jnp.where(lo, xr, x).astype(BF16)
        vr = pltpu.roll(xv, HEAD_DIM, 1)
        vscr[0, sl, :] = jnp.where(lo, xv, one_hi).astype(BF16)
        vscr[1, sl, :] = jnp.where(lo, one_lo, vr).astype(BF16)
        vscr[2, sl, :] = jnp.where(lo, vr, one_hi).astype(BF16)
        vscr[3, sl, :] = jnp.where(lo, one_lo, xv).astype(BF16)

    @pl.when(qi == 0)
    def _build():
        if past:
            put(ck_ref[...], cv_ref[...], 0)

        def body(c, carry):
            r = pl.multiple_of(c * rows, rows)
            k = k_ref[pl.ds(r, rows), :]
            if use_rope:
                k = _rope(k, cos_ref[pl.ds(r, rows), :], sin_ref[pl.ds(r, rows), :])
            put(k, v_ref[pl.ds(r, rows), :], past + r)
            return carry

        lax.fori_loop(0, seq // rows, body, 0)

    ts = min(ATT_SUB, tq)
    q0 = pl.multiple_of(qi * tq, tq)
    nk = past + seq
    q_scale = (HEAD_DIM ** -0.5) * LOG2E

    def unit_segments(h):
        if not window:
            return [(slice(0, nk), None)]
        span = ts + 2 * WINDOW
        t0 = q0 + h * ts
        start = pl.multiple_of(jnp.clip(t0 - WINDOW, 0, seq - span), WINDOW)
        t_pos = t0 + lax.broadcasted_iota(jnp.int32, (ts, 1), 0)
        s_pos = start + lax.broadcasted_iota(jnp.int32, (1, span), 1)
        band_ok = jnp.abs(t_pos - s_pos) <= WINDOW
        return [(slice(0, past), None), (pl.ds(past + start, span), band_ok)]

    def score_phase(g, h, segments):
        xs = []
        for p in range(2):
            q = q_refs[g][h * ts:(h + 1) * ts, p * LANES:(p + 1) * LANES]
            if use_rope:
                tsl = pl.ds(q0 + h * ts, ts)
                q = _rope(q, cos_ref[tsl, :], sin_ref[tsl, :])
            xs.append(q * q_scale)
        q4 = jnp.concatenate([jnp.where(lo, xs[0], 0.0), jnp.where(lo, xs[1], 0.0),
                              jnp.where(lo, 0.0, xs[0]), jnp.where(lo, 0.0, xs[1])], axis=0).astype(BF16)
        scores = []
        for ksl, ok in segments:
            s = _dot_nt(q4, kscr[g, ksl, :])
            if ok is not None:
                s = jnp.where(jnp.concatenate([ok] * 4, axis=0), s, NEG_INF)
            scores.append(s)
        return scores

    def softmax_phase(g, scores):
        probs = [[] for _ in scores]
        extra = []
        for blk, head in enumerate((0, 2, 1, 3)):
            rsl = slice(blk * ts, (blk + 1) * ts)
            m = None
            for s in scores:
                ms = jnp.max(s[rsl], axis=-1, keepdims=True)
                m = ms if m is None else jnp.maximum(m, ms)
            if use_sink:
                sk = sink_ref[sink_off + g * 4 + head] * LOG2E
                m = jnp.maximum(m, sk)
                extra.append(jnp.exp2(sk - m))
            for i, s in enumerate(scores):
                probs[i].append(jnp.exp2(s[rsl] - m).astype(BF16))
        return probs, extra

    def value_phase(g, probs, segments):
        acc = [None, None]
        for i, (ksl, _) in enumerate(segments):
            for par in range(2):
                e = jnp.concatenate(probs[i][2 * par:2 * par + 2], axis=0)
                pv = _dot(e, vscr[2 * g + par, ksl, :])
                acc[par] = pv if acc[par] is None else acc[par] + pv
        return acc

    def output_phase(g, h, acc, extra):
        den = [jnp.sum(jnp.where(lane == HEAD_DIM, acc[0], 0.0), axis=-1, keepdims=True),
               jnp.sum(jnp.where(lane == 0, acc[1], 0.0), axis=-1, keepdims=True)]
        if use_sink:
            den[0] = den[0] + jnp.concatenate(extra[0:2], axis=0)
            den[1] = den[1] + jnp.concatenate(extra[2:4], axis=0)
        outs = [acc[par] * (1.0 / den[par]) for par in range(2)]
        for p in range(2):
            rsl = slice(p * ts, (p + 1) * ts)
            o_pair = jnp.where(lo, outs[0][rsl], outs[1][rsl])
            o_ref[h * ts:(h + 1) * ts, g * 256 + p * LANES:g * 256 + (p + 1) * LANES] = o_pair.astype(o_ref.dtype)

    units = [(g, h) for h in range(tq // ts) for g in range(ATT_KV)]
    segs = {h: unit_segments(h) for h in range(tq // ts)}
    scores = [score_phase(g, h, segs[h]) for g, h in units]
    accs, extras = [], []
    for (g, h), sc in zip(units, scores):
        probs, extra = softmax_phase(g, sc)
        accs.append(value_phase(g, probs, segs[h]))
        extras.append(extra)
    for (g, h), acc, extra in zip(units, accs, extras):
        output_phase(g, h, acc, extra)


def _attn_call(y, nb, seq, qcol, kcol, vcol, *, cache_k=None, cache_v=None, layer=0, rope=None,
               sink=None, window=False):
    tq = min(ATT_TQ, seq)
    nq = seq // tq
    past = 0 if cache_k is None else cache_k.shape[2]
    use_rope = rope is not None
    use_sink = sink is not None
    in_specs, args = [], []
    if use_sink:
        in_specs.append(pl.BlockSpec(memory_space=pltpu.SMEM))
        args.append(sink.reshape(-1).astype(F32))
    for g in range(ATT_KV):
        in_specs.append(pl.BlockSpec((tq, 256), lambda b, qi, g=g: (b * nq + qi, qcol // 256 + g)))
        args.append(y)
    in_specs.append(pl.BlockSpec((seq, LANES), lambda b, qi: (b, kcol // LANES)))
    in_specs.append(pl.BlockSpec((seq, LANES), lambda b, qi: (b, vcol // LANES)))
    args += [y, y]
    if past:
        cshape = cache_k.shape[:3] + (LANES,)
        in_specs += [pl.BlockSpec((None, None, past, LANES), lambda b, qi: (b, layer, 0, 0))] * 2
        args += [cache_k.reshape(cshape), cache_v.reshape(cshape)]
    if use_rope:
        in_specs += [pl.BlockSpec((seq, LANES), lambda b, qi: (0, 0))] * 2
        args += list(rope)
    kern = functools.partial(_attn_kernel, tq=tq, seq=seq, past=past, use_rope=use_rope,
                             use_sink=use_sink, window=window, sink_off=layer * 8)
    return pl.pallas_call(
        kern,
        grid=(nb, nq),
        in_specs=in_specs,
        out_specs=pl.BlockSpec((tq, BRANCH_W), lambda b, qi: (b * nq + qi, 0)),
        out_shape=jax.ShapeDtypeStruct((nb * seq, BRANCH_W), BF16),
        scratch_shapes=[pltpu.VMEM((2, past + seq, LANES), BF16), pltpu.VMEM((4, past + seq, LANES), BF16)],
        compiler_params=_params(("parallel", "arbitrary")),
        name="win" if window or use_sink else "att",
    )(*args)


def _pair_table(fwd):
    import numpy as np
    C = HG_CHUNK
    t = np.arange(C)[:, None]
    s = np.arange(C)[None, :]
    seen = (s <= t) if fwd else (s >= t)
    tab = np.full((C, C), -1, np.int32)
    hh, idx = C // 2, (C // HG_SUB).bit_length() - 2
    while hh >= HG_SUB:
        tab = np.where((t // (2 * hh) == s // (2 * hh)) & seen, idx, tab)
        hh, idx = hh // 2, idx - 1
    tab = np.where((t // HG_SUB == s // HG_SUB), np.where(seen, PAIR_SUB + s % HG_SUB, -1), tab)
    return tab.astype(np.int32)


def _hgrn_unit(q, k, b, v, st, sub_masks, level_masks, fwd):
    C = HG_CHUNK
    G = C // HG_SUB
    vb = v.astype(BF16)

    b3 = b.reshape(G, HG_SUB, HG_DK)
    q3 = q.reshape(G, HG_SUB, HG_DK)
    c3 = (jnp.log2(k) - b).reshape(G, HG_SUB, HG_DK)
    a3 = jnp.zeros((G, HG_SUB, C), F32)
    for jj in range(HG_SUB):
        kdec = jnp.exp2(b3 + c3[:, jj:jj + 1, :])
        score = jnp.sum(q3 * kdec, axis=-1, keepdims=True)
        a3 = jnp.where(sub_masks[jj], score, a3)
    a_mat = a3.reshape(C, C)

    tcol = lax.broadcasted_iota(jnp.int32, (C, 1), 0)
    hh, idx = HG_SUB, 0
    while hh < C:
        grp = 2 * hh
        later = (tcol & (grp - 1)) >= hh
        is_q = later if fwd else jnp.logical_not(later)
        bg = b.reshape(C // grp, grp, HG_DK)
        bnd = bg[:, hh - 1:hh, :] if fwd else bg[:, hh:hh + 1, :]
        bnd = jnp.broadcast_to(bnd, (C // grp, grp, HG_DK)).reshape(C, HG_DK)
        w = (jnp.where(is_q, q, k) * jnp.exp2(jnp.where(is_q, b - bnd, bnd - b))).astype(BF16)
        a_mat = jnp.where(level_masks[idx], _dot_nt(w, w), a_mat)
        hh, idx = grp, idx + 1
    out = _dot(a_mat.astype(BF16), vb)

    out = out + _dot_nt((q * jnp.exp2(b)).astype(BF16), st.astype(BF16))
    blast = b[C - 1:C, :] if fwd else b[0:1, :]
    kh = (k * jnp.exp2(blast - b)).astype(BF16)
    st_new = st * jnp.exp2(blast) + _dot_tn(vb, kh)
    return out, st_new


def _hgrn_kernel(*refs, nsteps, cps, has_init, emit_state):
    it = iter(refs)
    io = [(next(it), next(it), next(it)) for _ in range(2)]
    pair_ref = next(it)
    s0_ref = next(it) if has_init else None
    o_refs = (next(it), next(it))
    sfin_ref = next(it) if emit_state else None
    st_scr = next(it)

    j = pl.program_id(1)
    C = HG_CHUNK

    @pl.when(j == 0)
    def _():
        for dirn in range(2):
            for h in range(HG_HEADS):
                if has_init:
                    st_scr[dirn, h] = s0_ref[dirn, h].T
                else:
                    st_scr[dirn, h] = jnp.zeros((HG_DK, HG_DK), F32)

    ti = lax.broadcasted_iota(jnp.int32, (C, C), 0)
    si = lax.broadcasted_iota(jnp.int32, (C, C), 1)
    tris, masks = [], []
    for dirn in range(2):
        tris.append(jnp.where((ti >= si) if dirn == 0 else (ti <= si), 1.0, 0.0).astype(BF16))
        pair = pair_ref[dirn]
        pair3 = pair.reshape(C // HG_SUB, HG_SUB, C)
        masks.append(([pair3 == PAIR_SUB + jj for jj in range(HG_SUB)],
                      [pair == idx for idx in range((C // HG_SUB).bit_length() - 1)]))
    states = [[st_scr[dirn, h] for h in range(HG_HEADS)] for dirn in range(2)]
    for cc in range(cps):
        for dirn in range(2):
            fwd = dirn == 0
            q_ref, f_ref, v_ref = io[dirn]
            r0 = (cc if fwd else cps - 1 - cc) * C
            rows = slice(r0, r0 + C)
            f = f_ref[rows, :]
            l2 = jnp.log2(jnp.maximum(f, TINY))
            l_hi = l2.astype(BF16)
            r1 = l2 - l_hi.astype(F32)
            l_mid = r1.astype(BF16)
            l_lo = (r1 - l_mid.astype(F32)).astype(BF16)
            b_all = _dot(tris[dirn], l_hi) + _dot(tris[dirn], l_mid) + _dot(tris[dirn], l_lo)
            for h in range(HG_HEADS):
                sl = slice(h * HG_DK, (h + 1) * HG_DK)
                out, st_new = _hgrn_unit(q_ref[rows, sl], jnp.maximum(1.0 - f[:, sl], 0.0), b_all[:, sl],
                                         v_ref[rows, sl], states[dirn][h], masks[dirn][0], masks[dirn][1], fwd)
                o_refs[dirn][rows, sl] = out
                states[dirn][h] = st_new
    for dirn in range(2):
        for h in range(HG_HEADS):
            st_scr[dirn, h] = states[dirn][h]
            if emit_state:
                @pl.when(j == nsteps - 1)
                def _():
                    sfin_ref[dirn, h] = states[dirn][h].T


def _hgrn_call(y, nb, seq, layer, state0=None, emit_state=False):
    cps = HG_STEP_CHUNKS
    rows = cps * HG_CHUNK
    nsteps = seq // rows
    has_init = state0 is not None
    in_specs, args = [], []
    for dirn in range(2):
        rowblk = (lambda n, j: n * nsteps + j) if dirn == 0 else (lambda n, j: n * nsteps + nsteps - 1 - j)
        for col in (COL_HQ, COL_HF + dirn * HG_W, COL_HI):
            in_specs.append(pl.BlockSpec((rows, HG_W), lambda n, j, rowblk=rowblk, col=col: (rowblk(n, j), col // HG_W)))
            args.append(y)
    in_specs.append(pl.BlockSpec((2, HG_CHUNK, HG_CHUNK), lambda n, j: (0, 0, 0)))
    args.append(jnp.asarray([_pair_table(True), _pair_table(False)]))
    if has_init:
        in_specs.append(pl.BlockSpec((None, None, 2, HG_HEADS, HG_DK, HG_DK), lambda n, j: (n, layer, 0, 0, 0, 0)))
        args.append(state0)
    out_specs = [pl.BlockSpec((rows, HG_W), lambda n, j: (n * nsteps + j, 0)),
                 pl.BlockSpec((rows, HG_W), lambda n, j: (n * nsteps + nsteps - 1 - j, 0))]
    out_shape = [jax.ShapeDtypeStruct((nb * seq, HG_W), F32)] * 2
    if emit_state:
        out_specs.append(pl.BlockSpec((None, 2, HG_HEADS, HG_DK, HG_DK), lambda n, j: (n, 0, 0, 0, 0)))
        out_shape.append(jax.ShapeDtypeStruct((nb, 2, HG_HEADS, HG_DK, HG_DK), F32))
    res = pl.pallas_call(
        functools.partial(_hgrn_kernel, nsteps=nsteps, cps=cps, has_init=has_init, emit_state=emit_state),
        grid=(nb, nsteps),
        in_specs=in_specs,
        out_specs=out_specs,
        out_shape=out_shape,
        scratch_shapes=[pltpu.VMEM((2, HG_HEADS, HG_DK, HG_DK), F32)],
        compiler_params=_params(("parallel", "arbitrary")),
        name="hgrn",
    )(*args)
    return (res[0], res[1], res[2]) if emit_state else (res[0], res[1], None)


def _merge_kernel(*refs, final):
    (oa_ref, ow_ref, of_ref, ob_ref, hg_ref, gates_ref, x_ref, gate_ref, hgn_ref, wb_ref, wo_ref,
     shift2_ref, scale2_ref, gate2_ref, g2_ref, wgu_ref, wd_ref) = refs[:17]
    fg_ref = refs[17] if final else None
    o_ref = refs[-1]
    o = of_ref[...] + ob_ref[...]
    hg = hg_ref[...]
    hgn = hgn_ref[...]
    parts = []
    for h in range(HG_HEADS):
        sl = slice(h * HG_DK, (h + 1) * HG_DK)
        g = hg[:, sl]
        parts.append(_rms_rows(o[:, sl], hgn) * (g * _sigmoid(g)))
    o_hg = jnp.concatenate(parts, axis=-1).astype(BF16)
    branch = (oa_ref[...], o_hg, ow_ref[...])
    merged = None
    for k in range(3):
        term = gates_ref[:, k * D_MODEL:(k + 1) * D_MODEL].astype(F32) * _dot(branch[k], wb_ref[k])
        merged = term if merged is None else merged + term
    yv = _dot(merged.astype(BF16), wo_ref[...])
    x_mid = x_ref[...] + gate_ref[...] * yv
    _ffn_block(x_mid, shift2_ref[...], scale2_ref[...], gate2_ref[...], g2_ref[...], wgu_ref, wd_ref, o_ref,
               fg_ref[...] if final else None)


def _merge_call(x, y, gates, o_att, o_win, o_f, o_b, modv, normv, hg_norm_g, w_branch_b, w_out_b,
                w_ffn_in_b, w_ffn_out_b, layer, row_fn, tm, final_g=None):
    n = x.shape[0]
    final = final_g is not None
    row = lambda i: (i, 0)
    in_specs = [
        pl.BlockSpec((tm, BRANCH_W), row),
        pl.BlockSpec((tm, BRANCH_W), row),
        pl.BlockSpec((tm, HG_W), row),
        pl.BlockSpec((tm, HG_W), row),
        pl.BlockSpec((tm, HG_W), lambda i: (i, COL_HG // HG_W)),
        pl.BlockSpec((tm, GATES_W), row),
        pl.BlockSpec((tm, D_MODEL), row),
        _mod_spec(layer, 5, row_fn),
        pl.BlockSpec((1, HG_DK), lambda i: (0, 0)),
        _resident((None, 3, BRANCH_W, D_MODEL), lambda i: (layer, 0, 0, 0)),
        _resident((None, D_MODEL, D_MODEL), lambda i: (layer, 0, 0)),
    ] + _ffn_specs(layer, 1, row_fn)
    args = [o_att, o_win, o_f, o_b, y, gates, x, modv, hg_norm_g[layer].reshape(1, HG_DK).astype(F32),
            w_branch_b, w_out_b, modv, modv, modv, normv, w_ffn_in_b, w_ffn_out_b]
    if final:
        in_specs.append(pl.BlockSpec((1, D_MODEL), lambda i: (0, 0)))
        args.append(final_g.reshape(1, D_MODEL))
    return pl.pallas_call(
        functools.partial(_merge_kernel, final=final),
        grid=(n // tm,),
        in_specs=in_specs,
        out_specs=pl.BlockSpec((tm, D_MODEL), row),
        out_shape=jax.ShapeDtypeStruct((n, D_MODEL), F32),
        compiler_params=_params(("parallel",)),
        name="merge_ffn",
    )(*args)


def _rope_tables(n_lat):
    t = jnp.arange(n_lat, dtype=jnp.int32)
    row = (t // GRID_W).astype(F32)
    col = (t % GRID_W).astype(F32)
    axis_dim = HEAD_DIM // 2
    inv = ROPE_THETA ** (-jnp.arange(0, axis_dim, 2, dtype=F32) / axis_dim)
    ang_r = row[:, None] * inv
    ang_c = col[:, None] * inv
    cos64 = jnp.concatenate([jnp.cos(ang_r), jnp.cos(ang_r), jnp.cos(ang_c), jnp.cos(ang_c)], axis=-1)
    sin64 = jnp.concatenate([-jnp.sin(ang_r), jnp.sin(ang_r), -jnp.sin(ang_c), jnp.sin(ang_c)], axis=-1)
    return jnp.tile(cos64, (1, 2)), jnp.tile(sin64, (1, 2))


def kernel(x_prompt, x_sample, cache_k_attn, cache_v_attn, cache_k_win, cache_v_win, state_hgrn, c, c_ctx,
           w_mod, b_mod, norm_g, w_ffn_in, w_ffn_out, w_in, qk_norm_g, lower_bounds, hg_norm_g, sink_logit,
           w_branch, w_out, final_norm_g):
    batch, seq, _ = x_prompt.shape
    dec_batch, dec_seq, _ = x_sample.shape
    depth = w_mod.shape[0]

    cond = jnp.zeros((COND_ROWS, D_MODEL), F32).at[0].set(c_ctx).at[1:1 + dec_batch].set(c)
    modv = _mod_call(cond, w_mod, b_mod).reshape(depth * COND_ROWS * N_MOD, 1, D_MODEL)
    lb_all = _lb_call(lower_bounds).reshape(depth, 1, 2 * HG_W)
    normv = norm_g.astype(F32).reshape(depth * 3, 1, D_MODEL)
    w_ffn_in_b = w_ffn_in.astype(BF16)
    w_ffn_out_b = w_ffn_out.astype(BF16)
    w_in_b = w_in.astype(BF16)
    w_branch_b = w_branch.astype(BF16)
    w_out_b = w_out.astype(BF16)
    rope = _rope_tables(dec_seq)

    def run(x, nb, s, latent):
        n = nb * s

        def rows_of(tm):
            return (lambda i: 1 + i // (s // tm)) if latent else (lambda i: 0)

        tm = min(DENSE_TM, s if latent else n)
        tm_ffn = min(FFN_TM, s if latent else n)
        x = x.reshape(n, D_MODEL)
        ctx_out = []
        for l in range(depth):
            x = _ffn_call(x, modv, normv, w_ffn_in_b, w_ffn_out_b, l, rows_of(tm_ffn), tm_ffn)
            y, gates = _inproj_call(x, modv, normv, w_in_b, qk_norm_g, lb_all, l, rows_of(tm), tm)
            if latent:
                o_att = _attn_call(y, nb, s, COL_AQ, COL_AK, COL_AV, cache_k=cache_k_attn,
                                   cache_v=cache_v_attn, layer=l, rope=rope)
                o_win = _attn_call(y, nb, s, COL_WQ, COL_WK, COL_WV, cache_k=cache_k_win,
                                   cache_v=cache_v_win, layer=l, rope=rope, sink=sink_logit, window=True)
                o_f, o_b, _ = _hgrn_call(y, nb, s, l, state0=state_hgrn)
            else:
                o_att = _attn_call(y, nb, s, COL_AQ, COL_AK, COL_AV)
                o_win = _attn_call(y, nb, s, COL_WQ, COL_WK, COL_WV, layer=l, sink=sink_logit)
                o_f, o_b, s_fin = _hgrn_call(y, nb, s, l, emit_state=True)
                kv = lambda col: y[:, col:col + LANES].reshape(nb, s, ATT_KV, HEAD_DIM)
                ctx_out.append((kv(COL_AK), kv(COL_AV), kv(COL_WK), kv(COL_WV), s_fin))
            x = _merge_call(x, y, gates, o_att, o_win, o_f, o_b, modv, normv, hg_norm_g, w_branch_b, w_out_b,
                            w_ffn_in_b, w_ffn_out_b, l, rows_of(tm), tm,
                            final_g=final_norm_g.astype(F32) if l == depth - 1 else None)
        return x.reshape(nb, s, D_MODEL), ctx_out

    y_prompt, ctx_out = run(x_prompt, batch, seq, False)
    y_sample, _ = run(x_sample, dec_batch, dec_seq, True)
    stack = lambda k: jnp.stack([cx[k] for cx in ctx_out], axis=1)
    return (y_prompt, y_sample, stack(0), stack(1), stack(2), stack(3), stack(4))
```

```python
import functools

import jax
import jax.numpy as jnp
from jax import lax
from jax.experimental import pallas as pl
from jax.experimental.pallas import tpu as pltpu

F32 = jnp.float32
BF16 = jnp.bfloat16

D_MODEL = 1024
GRID_W = 64
HEAD_DIM = 64
ATT_KV = 2
WINDOW = 128
HG_HEADS = 4
HG_DK = 128
HG_W = HG_HEADS * HG_DK
BRANCH_W = 512
D_FF = 2816
ROPE_THETA = 10000.0
EPS = 1e-6
NEG_INF = -1e30
TINY = 1e-30
LOG2E = 1.4426950408889634
N_MOD = 9
IN_W = 7168

LANES = 128
SUBLANES = 8

COND_ROWS = 16
DENSE_TM = 512
FFN_TM = 1024
FFN_TF = 256
ATT_TQ = 256
ATT_SUB = 128
HG_CHUNK = 128
HG_STEP_CHUNKS = 2
HG_SUB = SUBLANES
PAIR_SUB = 16
VMEM_LIMIT = 56 * 1024 * 1024

COL_AQ, COL_AK, COL_AV = 0, 512, 640
COL_WQ, COL_WK, COL_WV = 768, 1280, 1408
COL_HQ, COL_HF, COL_HI, COL_HG = 1536, 2048, 3072, 3584
COL_GATES = 4096
Y_W = COL_GATES
GATES_W = IN_W - COL_GATES


def _params(sem):
    return pltpu.CompilerParams(dimension_semantics=sem, vmem_limit_bytes=VMEM_LIMIT)


def _resident(block_shape, index_map):
    return pl.BlockSpec(block_shape, index_map, pipeline_mode=pl.Buffered(1))


def _sigmoid(x):
    return 1.0 / (1.0 + jnp.exp(-x))


def _dot(a, b):
    return jnp.dot(a, b, preferred_element_type=F32)


def _dot_nt(a, b):
    return lax.dot_general(a, b, (((1,), (1,)), ((), ())), preferred_element_type=F32)


def _dot_tn(a, b):
    return lax.dot_general(a, b, (((0,), (0,)), ((), ())), preferred_element_type=F32)


def _rms_rows(x, gain):
    return x * lax.rsqrt(jnp.mean(x * x, axis=-1, keepdims=True) + EPS) * gain


def _mod_kernel(c_ref, w_ref, b_ref, o_ref):
    c = c_ref[...]
    h = (c * _sigmoid(c)).astype(BF16)
    o_ref[...] = _dot(h, w_ref[...].astype(BF16)) + b_ref[...]


def _mod_call(cond, w_mod, b_mod):
    depth, d, nm = w_mod.shape
    tn = nm // 8
    return pl.pallas_call(
        _mod_kernel,
        grid=(depth, nm // tn),
        in_specs=[pl.BlockSpec((COND_ROWS, d), lambda l, j: (0, 0)),
                  pl.BlockSpec((None, d, tn), lambda l, j: (l, 0, j)),
                  pl.BlockSpec((None, 1, tn), lambda l, j: (l, 0, j))],
        out_specs=pl.BlockSpec((None, COND_ROWS, tn), lambda l, j: (l, 0, j)),
        out_shape=jax.ShapeDtypeStruct((depth, COND_ROWS, nm), F32),
        compiler_params=_params(("parallel", "parallel")),
        name="mod",
    )(cond, w_mod, b_mod.reshape(depth, 1, nm))


def _lb_kernel(x_ref, o_ref):
    x = x_ref[...]
    m = jnp.max(x, axis=0, keepdims=True)
    e = jnp.exp(x - m)
    s = e / jnp.sum(e, axis=0, keepdims=True)
    acc = jnp.zeros_like(s[0:1])
    for l in range(x.shape[0]):
        acc = acc + s[l:l + 1]
        o_ref[l:l + 1, :] = acc - s[0:1]


def _lb_call(lower_bounds):
    depth = lower_bounds.shape[0]
    x = lower_bounds.reshape(depth, -1).astype(F32)
    return pl.pallas_call(_lb_kernel, out_shape=jax.ShapeDtypeStruct(x.shape, F32), name="lb")(x)


def _ffn_block(x, shift, scale, gate, g, wgu_ref, wd_ref, o_ref, final_g):
    h = (_rms_rows(x, g) * (1.0 + scale) + shift).astype(BF16)
    for c in range(D_FF // FFN_TF):
        a = _dot(h, wgu_ref[:, c * FFN_TF:(c + 1) * FFN_TF])
        u = _dot(h, wgu_ref[:, D_FF + c * FFN_TF:D_FF + (c + 1) * FFN_TF])
        act = (a * _sigmoid(a) * u).astype(BF16)
        part = _dot(act, wd_ref[c * FFN_TF:(c + 1) * FFN_TF, :])
        if c == 0:
            o_ref[...] = part
        else:
            o_ref[...] += part
    out = x + 0.5 * gate * o_ref[...]
    if final_g is not None:
        out = _rms_rows(out, final_g)
    o_ref[...] = out


def _ffn_kernel(x_ref, shift_ref, scale_ref, gate_ref, g_ref, wgu_ref, wd_ref, o_ref):
    _ffn_block(x_ref[...], shift_ref[...], scale_ref[...], gate_ref[...], g_ref[...], wgu_ref, wd_ref, o_ref, None)


def _mod_spec(layer, k, row_fn):
    return pl.BlockSpec((None, 1, D_MODEL), lambda i: ((layer * COND_ROWS + row_fn(i)) * N_MOD + k, 0, 0))


def _ffn_specs(layer, which, row_fn):
    k0 = 3 * (2 * which)
    return [
        _mod_spec(layer, k0, row_fn), _mod_spec(layer, k0 + 1, row_fn), _mod_spec(layer, k0 + 2, row_fn),
        pl.BlockSpec((None, 1, D_MODEL), lambda i: (layer * 3 + 2 * which, 0, 0)),
        _resident((None, None, D_MODEL, 2 * D_FF), lambda i: (layer, which, 0, 0)),
        _resident((None, None, D_FF, D_MODEL), lambda i: (layer, which, 0, 0)),
    ]


def _ffn_call(x, modv, normv, w_in_b, w_out_b, layer, row_fn, tm):
    n = x.shape[0]
    return pl.pallas_call(
        _ffn_kernel,
        grid=(n // tm,),
        in_specs=[pl.BlockSpec((tm, D_MODEL), lambda i: (i, 0))] + _ffn_specs(layer, 0, row_fn),
        out_specs=pl.BlockSpec((tm, D_MODEL), lambda i: (i, 0)),
        out_shape=jax.ShapeDtypeStruct((n, D_MODEL), F32),
        compiler_params=_params(("parallel",)),
        name="ffn",
    )(x, modv, modv, modv, normv, w_in_b, w_out_b)


def _head_rms(y, gain):
    lane = lax.broadcasted_iota(jnp.int32, (1, LANES), 1)
    lo = lane < HEAD_DIM
    outs = []
    for c in range(y.shape[1] // LANES):
        blk = y[:, c * LANES:(c + 1) * LANES]
        sq = blk * blk
        s_lo = jnp.sum(jnp.where(lo, sq, 0.0), axis=-1, keepdims=True)
        s_hi = jnp.sum(jnp.where(lo, 0.0, sq), axis=-1, keepdims=True)
        ms = jnp.where(lo, s_lo, s_hi) * (1.0 / HEAD_DIM)
        outs.append(blk * lax.rsqrt(ms + EPS) * gain[:, c * LANES:(c + 1) * LANES])
    return outs[0] if len(outs) == 1 else jnp.concatenate(outs, axis=-1)


def _inproj_kernel(x_ref, shift_ref, scale_ref, g_ref, w_ref, qg_ref, kg_ref, lb_ref, y_ref, gates_ref):
    h = (_rms_rows(x_ref[...], g_ref[...]) * (1.0 + scale_ref[...]) + shift_ref[...]).astype(BF16)

    def proj(c0, c1):
        return _dot(h, w_ref[:, c0:c1])

    for k in range(GATES_W // D_MODEL):
        c0 = COL_GATES + k * D_MODEL
        gates_ref[:, k * D_MODEL:(k + 1) * D_MODEL] = _sigmoid(proj(c0, c0 + D_MODEL)).astype(BF16)
    t = proj(COL_HQ, COL_HF)
    y_ref[:, COL_HQ:COL_HF] = t * _sigmoid(t)
    for dirn in range(2):
        c0 = COL_HF + dirn * HG_W
        lb = lb_ref[:, dirn * HG_W:(dirn + 1) * HG_W]
        y_ref[:, c0:c0 + HG_W] = lb + (1.0 - lb) * _sigmoid(proj(c0, c0 + HG_W))
    y_ref[:, COL_AQ:COL_AK] = _head_rms(proj(COL_AQ, COL_AK), qg_ref[...])
    t = proj(COL_AK, COL_WQ)
    y_ref[:, COL_AK:COL_AV] = _head_rms(t[:, :LANES], kg_ref[...])
    y_ref[:, COL_AV:COL_WQ] = t[:, LANES:]
    y_ref[:, COL_WQ:COL_HQ] = proj(COL_WQ, COL_HQ)
    y_ref[:, COL_HI:COL_GATES] = proj(COL_HI, COL_GATES)


def _inproj_call(x, modv, normv, w_in_b, qk_norm_g, lb_all, layer, row_fn, tm):
    n = x.shape[0]
    qg = jnp.tile(qk_norm_g[layer, 0].astype(F32), (COL_AK - COL_AQ) // HEAD_DIM).reshape(1, COL_AK - COL_AQ)
    kg = jnp.tile(qk_norm_g[layer, 1].astype(F32), LANES // HEAD_DIM).reshape(1, LANES)
    return pl.pallas_call(
        _inproj_kernel,
        grid=(n // tm,),
        in_specs=[
            pl.BlockSpec((tm, D_MODEL), lambda i: (i, 0)),
            _mod_spec(layer, 3, row_fn), _mod_spec(layer, 4, row_fn),
            pl.BlockSpec((None, 1, D_MODEL), lambda i: (layer * 3 + 1, 0, 0)),
            _resident((None, D_MODEL, IN_W), lambda i: (layer, 0, 0)),
            pl.BlockSpec((1, COL_AK - COL_AQ), lambda i: (0, 0)),
            pl.BlockSpec((1, LANES), lambda i: (0, 0)),
            pl.BlockSpec((None, 1, 2 * HG_W), lambda i: (layer, 0, 0)),
        ],
        out_specs=[pl.BlockSpec((tm, Y_W), lambda i: (i, 0)),
                   pl.BlockSpec((tm, GATES_W), lambda i: (i, 0))],
        out_shape=[jax.ShapeDtypeStruct((n, Y_W), F32), jax.ShapeDtypeStruct((n, GATES_W), BF16)],
        compiler_params=_params(("parallel",)),
        name="inproj",
    )(x, modv, modv, normv, w_in_b, qg, kg, lb_all)


def _rope(x, cos, sin_signed):
    lane = lax.broadcasted_iota(jnp.int32, (1, LANES), 1)
    first = (lane & 31) < 16
    swapped = jnp.where(first, pltpu.roll(x, LANES - 16, 1), pltpu.roll(x, 16, 1))
    return x * cos + swapped * sin_signed


def _attn_kernel(*refs, tq, seq, past, use_rope, use_sink, window, sink_off):
    it = iter(refs)
    sink_ref = next(it) if use_sink else None
    q_refs = (next(it), next(it))
    k_ref, v_ref = next(it), next(it)
    ck_ref = cv_ref = cos_ref = sin_ref = None
    if past:
        ck_ref, cv_ref = next(it), next(it)
    if use_rope:
        cos_ref, sin_ref = next(it), next(it)
    o_ref = next(it)
    kscr = next(it)
    vscr = next(it)

    qi = pl.program_id(1)
    lane = lax.broadcasted_iota(jnp.int32, (1, LANES), 1)
    lo = lane < HEAD_DIM
    one_lo = jnp.where(lane == 0, 1.0, 0.0)
    one_hi = jnp.where(lane == HEAD_DIM, 1.0, 0.0)
    rows = min(seq, 256)

    def put(x, xv, r0):
        sl = pl.ds(r0, x.shape[0])
        xr = pltpu.roll(x, HEAD_DIM, 1)
        kscr[0, sl, :] = jnp.where(lo, x, xr).astype(BF16)
        kscr[1, sl, :] = jnp.where(lo, xr, x).astype(BF16)
        vr = pltpu.roll(xv, HEAD_DIM, 1)
        vscr[0, sl, :] = jnp.where(lo, xv, one_hi).astype(BF16)
        vscr[1, sl, :] = jnp.where(lo, one_lo, vr).astype(BF16)
        vscr[2, sl, :] = jnp.where(lo, vr, one_hi).astype(BF16)
        vscr[3, sl, :] = jnp.where(lo, one_lo, xv).astype(BF16)

    @pl.when(qi == 0)
    def _build():
        if past:
            put(ck_ref[...], cv_ref[...], 0)

        def body(c, carry):
            r = pl.multiple_of(c * rows, rows)
            k = k_ref[pl.ds(r, rows), :]
            if use_rope:
                k = _rope(k, cos_ref[pl.ds(r, rows), :], sin_ref[pl.ds(r, rows), :])
            put(k, v_ref[pl.ds(r, rows), :], past + r)
            return carry

        lax.fori_loop(0, seq // rows, body, 0)

    ts = min(ATT_SUB, tq)
    q0 = pl.multiple_of(qi * tq, tq)
    nk = past + seq
    q_scale = (HEAD_DIM ** -0.5) * LOG2E

    def unit_segments(h):
        if not window:
            return [(slice(0, nk), None)]
        span = ts + 2 * WINDOW
        t0 = q0 + h * ts
        start = pl.multiple_of(jnp.clip(t0 - WINDOW, 0, seq - span), WINDOW)
        t_pos = t0 + lax.broadcasted_iota(jnp.int32, (ts, 1), 0)
        s_pos = start + lax.broadcasted_iota(jnp.int32, (1, span), 1)
        band_ok = jnp.abs(t_pos - s_pos) <= WINDOW
        return [(slice(0, past), None), (pl.ds(past + start, span), band_ok)]

    def score_phase(g, h, segments):
        xs = []
        for p in range(2):
            q = q_refs[g][h * ts:(h + 1) * ts, p * LANES:(p + 1) * LANES]
            if use_rope:
                tsl = pl.ds(q0 + h * ts, ts)
                q = _rope(q, cos_ref[tsl, :], sin_ref[tsl, :])
            xs.append(q * q_scale)
        q4 = jnp.concatenate([jnp.where(lo, xs[0], 0.0), jnp.where(lo, xs[1], 0.0),
                              jnp.where(lo, 0.0, xs[0]), jnp.where(lo, 0.0, xs[1])], axis=0).astype(BF16)
        return [_dot_nt(q4, kscr[g, ksl, :]) for ksl, _ in segments]

    def softmax_phase(g, scores, segments):
        probs = [[] for _ in scores]
        extra = []
        for blk, head in enumerate((0, 2, 1, 3)):
            rsl = slice(blk * ts, (blk + 1) * ts)
            rows = [s[rsl] if ok is None else jnp.where(ok, s[rsl], NEG_INF) for s, (_, ok) in zip(scores, segments)]
            m = None
            for s in rows:
                ms = jnp.max(s, axis=-1, keepdims=True)
                m = ms if m is None else jnp.maximum(m, ms)
            if use_sink:
                sk = sink_ref[sink_off + g * 4 + head] * LOG2E
                m = jnp.maximum(m, sk)
                extra.append(jnp.exp2(sk - m))
            for i, s in enumerate(rows):
                probs[i].append(jnp.exp2(s - m).astype(BF16))
        return probs, extra

    def value_phase(g, probs, segments):
        acc = [None, None]
        for i, (ksl, _) in enumerate(segments):
            for par in range(2):
                e = jnp.concatenate(probs[i][2 * par:2 * par + 2], axis=0)
                pv = _dot(e, vscr[2 * g + par, ksl, :])
                acc[par] = pv if acc[par] is None else acc[par] + pv
        return acc

    def output_phase(g, h, acc, extra):
        outs = []
        for blk in range(4):
            par = blk // 2
            a = acc[par][(blk % 2) * ts:(blk % 2 + 1) * ts]
            den = jnp.sum(jnp.where(lane == (HEAD_DIM if par == 0 else 0), a, 0.0), axis=-1, keepdims=True)
            if use_sink:
                den = den + extra[blk]
            outs.append(a * (1.0 / den))
        for p in range(2):
            o_pair = jnp.where(lo, outs[p], outs[2 + p])
            o_ref[h * ts:(h + 1) * ts, g * 256 + p * LANES:g * 256 + (p + 1) * LANES] = o_pair.astype(o_ref.dtype)

    units = [(g, h) for h in range(tq // ts) for g in range(ATT_KV)]
    segs = {h: unit_segments(h) for h in range(tq // ts)}
    scores = [score_phase(g, h, segs[h]) for g, h in units]
    accs, extras = [], []
    for (g, h), sc in zip(units, scores):
        probs, extra = softmax_phase(g, sc, segs[h])
        accs.append(value_phase(g, probs, segs[h]))
        extras.append(extra)
    for (g, h), acc, extra in zip(units, accs, extras):
        output_phase(g, h, acc, extra)


def _attn_call(y, nb, seq, qcol, kcol, vcol, *, cache_k=None, cache_v=None, layer=0, rope=None,
               sink=None, window=False):
    tq = min(ATT_TQ, seq)
    nq = seq // tq
    past = 0 if cache_k is None else cache_k.shape[2]
    use_rope = rope is not None
    use_sink = sink is not None
    in_specs, args = [], []
    if use_sink:
        in_specs.append(pl.BlockSpec(memory_space=pltpu.SMEM))
        args.append(sink.reshape(-1).astype(F32))
    for g in range(ATT_KV):
        in_specs.append(pl.BlockSpec((tq, 256), lambda b, qi, g=g: (b * nq + qi, qcol // 256 + g)))
        args.append(y)
    in_specs.append(pl.BlockSpec((seq, LANES), lambda b, qi: (b, kcol // LANES)))
    in_specs.append(pl.BlockSpec((seq, LANES), lambda b, qi: (b, vcol // LANES)))
    args += [y, y]
    if past:
        cshape = cache_k.shape[:3] + (LANES,)
        in_specs += [pl.BlockSpec((None, None, past, LANES), lambda b, qi: (b, layer, 0, 0))] * 2
        args += [cache_k.reshape(cshape), cache_v.reshape(cshape)]
    if use_rope:
        in_specs += [pl.BlockSpec((seq, LANES), lambda b, qi: (0, 0))] * 2
        args += list(rope)
    kern = functools.partial(_attn_kernel, tq=tq, seq=seq, past=past, use_rope=use_rope,
                             use_sink=use_sink, window=window, sink_off=layer * 8)
    return pl.pallas_call(
        kern,
        grid=(nb, nq),
        in_specs=in_specs,
        out_specs=pl.BlockSpec((tq, BRANCH_W), lambda b, qi: (b * nq + qi, 0)),
        out_shape=jax.ShapeDtypeStruct((nb * seq, BRANCH_W), BF16),
        scratch_shapes=[pltpu.VMEM((2, past + seq, LANES), BF16), pltpu.VMEM((4, past + seq, LANES), BF16)],
        compiler_params=_params(("parallel", "arbitrary")),
        name="win" if window or use_sink else "att",
    )(*args)


def _pair_table(fwd):
    import numpy as np
    C = HG_CHUNK
    t = np.arange(C)[:, None]
    s = np.arange(C)[None, :]
    seen = (s <= t) if fwd else (s >= t)
    tab = np.full((C, C), -1, np.int32)
    hh, idx = C // 2, (C // HG_SUB).bit_length() - 2
    while hh >= HG_SUB:
        tab = np.where((t // (2 * hh) == s // (2 * hh)) & seen, idx, tab)
        hh, idx = hh // 2, idx - 1
    tab = np.where((t // HG_SUB == s // HG_SUB), np.where(seen, PAIR_SUB + s % HG_SUB, -1), tab)
    return tab.astype(np.int32)


def _hgrn_unit(q, k, b, v, st, sub_masks, level_masks, fwd):
    C = HG_CHUNK
    G = C // HG_SUB
    vb = v.astype(BF16)

    b3 = b.reshape(G, HG_SUB, HG_DK)
    q3 = q.reshape(G, HG_SUB, HG_DK)
    c3 = (jnp.log2(k) - b).reshape(G, HG_SUB, HG_DK)
    a3 = jnp.zeros((G, HG_SUB, C), F32)
    for jj in range(HG_SUB):
        kdec = jnp.exp2(b3 + c3[:, jj:jj + 1, :])
        score = jnp.sum(q3 * kdec, axis=-1, keepdims=True)
        a3 = jnp.where(sub_masks[jj], score, a3)
    a_mat = a3.reshape(C, C)

    tcol = lax.broadcasted_iota(jnp.int32, (C, 1), 0)
    hh, idx = HG_SUB, 0
    while hh < C:
        grp = 2 * hh
        later = (tcol & (grp - 1)) >= hh
        is_q = later if fwd else jnp.logical_not(later)
        bg = b.reshape(C // grp, grp, HG_DK)
        bnd = bg[:, hh - 1:hh, :] if fwd else bg[:, hh:hh + 1, :]
        bnd = jnp.broadcast_to(bnd, (C // grp, grp, HG_DK)).reshape(C, HG_DK)
        w = (jnp.where(is_q, q, k) * jnp.exp2(jnp.where(is_q, b - bnd, bnd - b))).astype(BF16)
        a_mat = jnp.where(level_masks[idx], _dot_nt(w, w), a_mat)
        hh, idx = grp, idx + 1
    out = _dot(a_mat.astype(BF16), vb)

    out = out + _dot_nt((q * jnp.exp2(b)).astype(BF16), st.astype(BF16))
    blast = b[C - 1:C, :] if fwd else b[0:1, :]
    kh = (k * jnp.exp2(blast - b)).astype(BF16)
    st_new = st * jnp.exp2(blast) + _dot_tn(vb, kh)
    return out, st_new


def _hgrn_kernel(*refs, nsteps, cps, has_init, emit_state):
    it = iter(refs)
    io = [(next(it), next(it), next(it)) for _ in range(2)]
    pair_ref = next(it)
    s0_ref = next(it) if has_init else None
    o_refs = (next(it), next(it))
    sfin_ref = next(it) if emit_state else None
    st_scr = next(it)

    j = pl.program_id(1)
    C = HG_CHUNK

    @pl.when(j == 0)
    def _():
        for dirn in range(2):
            for h in range(HG_HEADS):
                if has_init:
                    st_scr[dirn, h] = s0_ref[dirn, h].T
                else:
                    st_scr[dirn, h] = jnp.zeros((HG_DK, HG_DK), F32)

    ti = lax.broadcasted_iota(jnp.int32, (C, C), 0)
    si = lax.broadcasted_iota(jnp.int32, (C, C), 1)
    tris, masks = [], []
    for dirn in range(2):
        tris.append(jnp.where((ti >= si) if dirn == 0 else (ti <= si), 1.0, 0.0).astype(BF16))
        pair = pair_ref[dirn]
        pair3 = pair.reshape(C // HG_SUB, HG_SUB, C)
        masks.append(([pair3 == PAIR_SUB + jj for jj in range(HG_SUB)],
                      [pair == idx for idx in range((C // HG_SUB).bit_length() - 1)]))
    states = [[st_scr[dirn, h] for h in range(HG_HEADS)] for dirn in range(2)]
    for cc in range(cps):
        for dirn in range(2):
            fwd = dirn == 0
            q_ref, f_ref, v_ref = io[dirn]
            r0 = (cc if fwd else cps - 1 - cc) * C
            rows = slice(r0, r0 + C)
            f = f_ref[rows, :]
            l2 = jnp.log2(jnp.maximum(f, TINY))
            l_hi = l2.astype(BF16)
            r1 = l2 - l_hi.astype(F32)
            l_mid = r1.astype(BF16)
            l_lo = (r1 - l_mid.astype(F32)).astype(BF16)
            b_all = _dot(tris[dirn], l_hi) + _dot(tris[dirn], l_mid) + _dot(tris[dirn], l_lo)
            for h in range(HG_HEADS):
                sl = slice(h * HG_DK, (h + 1) * HG_DK)
                out, st_new = _hgrn_unit(q_ref[rows, sl], jnp.maximum(1.0 - f[:, sl], 0.0), b_all[:, sl],
                                         v_ref[rows, sl], states[dirn][h], masks[dirn][0], masks[dirn][1], fwd)
                o_refs[dirn][rows, sl] = out
                states[dirn][h] = st_new
    for dirn in range(2):
        for h in range(HG_HEADS):
            st_scr[dirn, h] = states[dirn][h]
            if emit_state:
                @pl.when(j == nsteps - 1)
                def _():
                    sfin_ref[dirn, h] = states[dirn][h].T


def _hgrn_call(y, nb, seq, layer, state0=None, emit_state=False):
    cps = HG_STEP_CHUNKS
    rows = cps * HG_CHUNK
    nsteps = seq // rows
    has_init = state0 is not None
    in_specs, args = [], []
    for dirn in range(2):
        rowblk = (lambda n, j: n * nsteps + j) if dirn == 0 else (lambda n, j: n * nsteps + nsteps - 1 - j)
        for col in (COL_HQ, COL_HF + dirn * HG_W, COL_HI):
            in_specs.append(pl.BlockSpec((rows, HG_W), lambda n, j, rowblk=rowblk, col=col: (rowblk(n, j), col // HG_W)))
            args.append(y)
    in_specs.append(pl.BlockSpec((2, HG_CHUNK, HG_CHUNK), lambda n, j: (0, 0, 0)))
    args.append(jnp.asarray([_pair_table(True), _pair_table(False)]))
    if has_init:
        in_specs.append(pl.BlockSpec((None, None, 2, HG_HEADS, HG_DK, HG_DK), lambda n, j: (n, layer, 0, 0, 0, 0)))
        args.append(state0)
    out_specs = [pl.BlockSpec((rows, HG_W), lambda n, j: (n * nsteps + j, 0)),
                 pl.BlockSpec((rows, HG_W), lambda n, j: (n * nsteps + nsteps - 1 - j, 0))]
    out_shape = [jax.ShapeDtypeStruct((nb * seq, HG_W), F32)] * 2
    if emit_state:
        out_specs.append(pl.BlockSpec((None, 2, HG_HEADS, HG_DK, HG_DK), lambda n, j: (n, 0, 0, 0, 0)))
        out_shape.append(jax.ShapeDtypeStruct((nb, 2, HG_HEADS, HG_DK, HG_DK), F32))
    res = pl.pallas_call(
        functools.partial(_hgrn_kernel, nsteps=nsteps, cps=cps, has_init=has_init, emit_state=emit_state),
        grid=(nb, nsteps),
        in_specs=in_specs,
        out_specs=out_specs,
        out_shape=out_shape,
        scratch_shapes=[pltpu.VMEM((2, HG_HEADS, HG_DK, HG_DK), F32)],
        compiler_params=_params(("parallel", "arbitrary")),
        name="hgrn",
    )(*args)
    return (res[0], res[1], res[2]) if emit_state else (res[0], res[1], None)


def _merge_kernel(*refs, final):
    (oa_ref, ow_ref, of_ref, ob_ref, hg_ref, gates_ref, x_ref, gate_ref, hgn_ref, wb_ref, wo_ref,
     shift2_ref, scale2_ref, gate2_ref, g2_ref, wgu_ref, wd_ref) = refs[:17]
    fg_ref = refs[17] if final else None
    o_ref = refs[-1]
    o = of_ref[...] + ob_ref[...]
    hg = hg_ref[...]
    hgn = hgn_ref[...]
    parts = []
    for h in range(HG_HEADS):
        sl = slice(h * HG_DK, (h + 1) * HG_DK)
        g = hg[:, sl]
        parts.append(_rms_rows(o[:, sl], hgn) * (g * _sigmoid(g)))
    o_hg = jnp.concatenate(parts, axis=-1).astype(BF16)
    branch = (oa_ref[...], o_hg, ow_ref[...])
    merged = None
    for k in range(3):
        term = gates_ref[:, k * D_MODEL:(k + 1) * D_MODEL].astype(F32) * _dot(branch[k], wb_ref[k])
        merged = term if merged is None else merged + term
    yv = _dot(merged.astype(BF16), wo_ref[...])
    x_mid = x_ref[...] + gate_ref[...] * yv
    _ffn_block(x_mid, shift2_ref[...], scale2_ref[...], gate2_ref[...], g2_ref[...], wgu_ref, wd_ref, o_ref,
               fg_ref[...] if final else None)


def _merge_call(x, y, gates, o_att, o_win, o_f, o_b, modv, normv, hg_norm_g, w_branch_b, w_out_b,
                w_ffn_in_b, w_ffn_out_b, layer, row_fn, tm, final_g=None):
    n = x.shape[0]
    final = final_g is not None
    row = lambda i: (i, 0)
    in_specs = [
        pl.BlockSpec((tm, BRANCH_W), row),
        pl.BlockSpec((tm, BRANCH_W), row),
        pl.BlockSpec((tm, HG_W), row),
        pl.BlockSpec((tm, HG_W), row),
        pl.BlockSpec((tm, HG_W), lambda i: (i, COL_HG // HG_W)),
        pl.BlockSpec((tm, GATES_W), row),
        pl.BlockSpec((tm, D_MODEL), row),
        _mod_spec(layer, 5, row_fn),
        pl.BlockSpec((1, HG_DK), lambda i: (0, 0)),
        _resident((None, 3, BRANCH_W, D_MODEL), lambda i: (layer, 0, 0, 0)),
        _resident((None, D_MODEL, D_MODEL), lambda i: (layer, 0, 0)),
    ] + _ffn_specs(layer, 1, row_fn)
    args = [o_att, o_win, o_f, o_b, y, gates, x, modv, hg_norm_g[layer].reshape(1, HG_DK).astype(F32),
            w_branch_b, w_out_b, modv, modv, modv, normv, w_ffn_in_b, w_ffn_out_b]
    if final:
        in_specs.append(pl.BlockSpec((1, D_MODEL), lambda i: (0, 0)))
        args.append(final_g.reshape(1, D_MODEL))
    return pl.pallas_call(
        functools.partial(_merge_kernel, final=final),
        grid=(n // tm,),
        in_specs=in_specs,
        out_specs=pl.BlockSpec((tm, D_MODEL), row),
        out_shape=jax.ShapeDtypeStruct((n, D_MODEL), F32),
        compiler_params=_params(("parallel",)),
        name="merge_ffn",
    )(*args)


def _rope_tables(n_lat):
    t = jnp.arange(n_lat, dtype=jnp.int32)
    row = (t // GRID_W).astype(F32)
    col = (t % GRID_W).astype(F32)
    axis_dim = HEAD_DIM // 2
    inv = ROPE_THETA ** (-jnp.arange(0, axis_dim, 2, dtype=F32) / axis_dim)
    ang_r = row[:, None] * inv
    ang_c = col[:, None] * inv
    cos64 = jnp.concatenate([jnp.cos(ang_r), jnp.cos(ang_r), jnp.cos(ang_c), jnp.cos(ang_c)], axis=-1)
    sin64 = jnp.concatenate([-jnp.sin(ang_r), jnp.sin(ang_r), -jnp.sin(ang_c), jnp.sin(ang_c)], axis=-1)
    return jnp.tile(cos64, (1, 2)), jnp.tile(sin64, (1, 2))


def kernel(x_prompt, x_sample, cache_k_attn, cache_v_attn, cache_k_win, cache_v_win, state_hgrn, c, c_ctx,
           w_mod, b_mod, norm_g, w_ffn_in, w_ffn_out, w_in, qk_norm_g, lower_bounds, hg_norm_g, sink_logit,
           w_branch, w_out, final_norm_g):
    batch, seq, _ = x_prompt.shape
    dec_batch, dec_seq, _ = x_sample.shape
    depth = w_mod.shape[0]

    cond = jnp.zeros((COND_ROWS, D_MODEL), F32).at[0].set(c_ctx).at[1:1 + dec_batch].set(c)
    modv = _mod_call(cond, w_mod, b_mod).reshape(depth * COND_ROWS * N_MOD, 1, D_MODEL)
    lb_all = _lb_call(lower_bounds).reshape(depth, 1, 2 * HG_W)
    normv = norm_g.astype(F32).reshape(depth * 3, 1, D_MODEL)
    w_ffn_in_b = w_ffn_in.astype(BF16)
    w_ffn_out_b = w_ffn_out.astype(BF16)
    w_in_b = w_in.astype(BF16)
    w_branch_b = w_branch.astype(BF16)
    w_out_b = w_out.astype(BF16)
    rope = _rope_tables(dec_seq)

    def run(x, nb, s, latent):
        n = nb * s

        def rows_of(tm):
            return (lambda i: 1 + i // (s // tm)) if latent else (lambda i: 0)

        tm = min(DENSE_TM, s if latent else n)
        tm_ffn = min(FFN_TM, s if latent else n)
        x = x.reshape(n, D_MODEL)
        ctx_out = []
        for l in range(depth):
            x = _ffn_call(x, modv, normv, w_ffn_in_b, w_ffn_out_b, l, rows_of(tm_ffn), tm_ffn)
            y, gates = _inproj_call(x, modv, normv, w_in_b, qk_norm_g, lb_all, l, rows_of(tm), tm)
            if latent:
                o_att = _attn_call(y, nb, s, COL_AQ, COL_AK, COL_AV, cache_k=cache_k_attn,
                                   cache_v=cache_v_attn, layer=l, rope=rope)
                o_win = _attn_call(y, nb, s, COL_WQ, COL_WK, COL_WV, cache_k=cache_k_win,
                                   cache_v=cache_v_win, layer=l, rope=rope, sink=sink_logit, window=True)
                o_f, o_b, _ = _hgrn_call(y, nb, s, l, state0=state_hgrn)
            else:
                o_att = _attn_call(y, nb, s, COL_AQ, COL_AK, COL_AV)
                o_win = _attn_call(y, nb, s, COL_WQ, COL_WK, COL_WV, layer=l, sink=sink_logit)
                o_f, o_b, s_fin = _hgrn_call(y, nb, s, l, emit_state=True)
                kv = lambda col: y[:, col:col + LANES].reshape(nb, s, ATT_KV, HEAD_DIM)
                ctx_out.append((kv(COL_AK), kv(COL_AV), kv(COL_WK), kv(COL_WV), s_fin))
            x = _merge_call(x, y, gates, o_att, o_win, o_f, o_b, modv, normv, hg_norm_g, w_branch_b, w_out_b,
                            w_ffn_in_b, w_ffn_out_b, l, rows_of(tm), tm,
                            final_g=final_norm_g.astype(F32) if l == depth - 1 else None)
        return x.reshape(nb, s, D_MODEL), ctx_out

    y_prompt, ctx_out = run(x_prompt, batch, seq, False)
    y_sample, _ = run(x_sample, dec_batch, dec_seq, True)
    stack = lambda k: jnp.stack([cx[k] for cx in ctx_out], axis=1)
    return (y_prompt, y_sample, stack(0), stack(1), stack(2), stack(3), stack(4))
```

```python
import functools

import jax
import jax.numpy as jnp
from jax import lax
from jax.experimental import pallas as pl
from jax.experimental.pallas import tpu as pltpu

F32 = jnp.float32
BF16 = jnp.bfloat16

D_MODEL = 1024
GRID_W = 64
HEAD_DIM = 64
ATT_KV = 2
WINDOW = 128
HG_HEADS = 4
HG_DK = 128
HG_W = HG_HEADS * HG_DK
BRANCH_W = 512
D_FF = 2816
ROPE_THETA = 10000.0
EPS = 1e-6
NEG_INF = -1e30
TINY = 1e-30
LOG2E = 1.4426950408889634
N_MOD = 9
IN_W = 7168

LANES = 128
SUBLANES = 8

COND_ROWS = 16
DENSE_TM = 512
FFN_TM = 1024
FFN_TF = 256
ATT_TQ = 256
ATT_SUB = 128
HG_CHUNK = 128
HG_STEP_CHUNKS = 2
HG_SUB = SUBLANES
PAIR_SUB = 16
VMEM_LIMIT = 56 * 1024 * 1024

COL_AQ, COL_AK, COL_AV = 0, 512, 640
COL_WQ, COL_WK, COL_WV = 768, 1280, 1408
COL_HQ, COL_HF, COL_HI, COL_HG = 1536, 2048, 3072, 3584
COL_GATES = 4096
Y_W = COL_GATES
KV_W = 6 * LANES
GATES_W = IN_W - COL_GATES


def _params(sem):
    return pltpu.CompilerParams(dimension_semantics=sem, vmem_limit_bytes=VMEM_LIMIT)


def _resident(block_shape, index_map):
    return pl.BlockSpec(block_shape, index_map, pipeline_mode=pl.Buffered(1))


def _sigmoid(x):
    return 1.0 / (1.0 + jnp.exp(-x))


def _dot(a, b):
    return jnp.dot(a, b, preferred_element_type=F32)


def _dot_nt(a, b):
    return lax.dot_general(a, b, (((1,), (1,)), ((), ())), preferred_element_type=F32)


def _dot_tn(a, b):
    return lax.dot_general(a, b, (((0,), (0,)), ((), ())), preferred_element_type=F32)


def _rms_rows(x, gain):
    return x * lax.rsqrt(jnp.mean(x * x, axis=-1, keepdims=True) + EPS) * gain


def _mod_kernel(c_ref, w_ref, b_ref, o_ref):
    c = c_ref[...]
    h = (c * _sigmoid(c)).astype(BF16)
    o_ref[...] = _dot(h, w_ref[...].astype(BF16)) + b_ref[...]


def _mod_call(cond, w_mod, b_mod):
    depth, d, nm = w_mod.shape
    tn = nm // 8
    return pl.pallas_call(
        _mod_kernel,
        grid=(depth, nm // tn),
        in_specs=[pl.BlockSpec((COND_ROWS, d), lambda l, j: (0, 0)),
                  pl.BlockSpec((None, d, tn), lambda l, j: (l, 0, j)),
                  pl.BlockSpec((None, 1, tn), lambda l, j: (l, 0, j))],
        out_specs=pl.BlockSpec((None, COND_ROWS, tn), lambda l, j: (l, 0, j)),
        out_shape=jax.ShapeDtypeStruct((depth, COND_ROWS, nm), F32),
        compiler_params=_params(("parallel", "parallel")),
        name="mod",
    )(cond, w_mod, b_mod.reshape(depth, 1, nm))


def _lb_kernel(x_ref, o_ref):
    x = x_ref[...]
    m = jnp.max(x, axis=0, keepdims=True)
    e = jnp.exp(x - m)
    s = e / jnp.sum(e, axis=0, keepdims=True)
    acc = jnp.zeros_like(s[0:1])
    for l in range(x.shape[0]):
        acc = acc + s[l:l + 1]
        o_ref[l:l + 1, :] = acc - s[0:1]


def _lb_call(lower_bounds):
    depth = lower_bounds.shape[0]
    x = lower_bounds.reshape(depth, -1).astype(F32)
    return pl.pallas_call(_lb_kernel, out_shape=jax.ShapeDtypeStruct(x.shape, F32), name="lb")(x)


def _ffn_block(x, shift, scale, gate, g, wgu_ref, wd_ref, o_ref, final_g):
    h = (_rms_rows(x, g) * (1.0 + scale) + shift).astype(BF16)
    for c in range(D_FF // FFN_TF):
        a = _dot(h, wgu_ref[:, c * FFN_TF:(c + 1) * FFN_TF])
        u = _dot(h, wgu_ref[:, D_FF + c * FFN_TF:D_FF + (c + 1) * FFN_TF])
        act = (a * _sigmoid(a) * u).astype(BF16)
        part = _dot(act, wd_ref[c * FFN_TF:(c + 1) * FFN_TF, :])
        if c == 0:
            o_ref[...] = part
        else:
            o_ref[...] += part
    out = x + 0.5 * gate * o_ref[...]
    if final_g is not None:
        out = _rms_rows(out, final_g)
    o_ref[...] = out


def _ffn_kernel(x_ref, shift_ref, scale_ref, gate_ref, g_ref, wgu_ref, wd_ref, o_ref):
    _ffn_block(x_ref[...], shift_ref[...], scale_ref[...], gate_ref[...], g_ref[...], wgu_ref, wd_ref, o_ref, None)


def _mod_spec(layer, k, row_fn):
    return pl.BlockSpec((None, 1, D_MODEL), lambda i: ((layer * COND_ROWS + row_fn(i)) * N_MOD + k, 0, 0))


def _ffn_specs(layer, which, row_fn):
    k0 = 3 * (2 * which)
    return [
        _mod_spec(layer, k0, row_fn), _mod_spec(layer, k0 + 1, row_fn), _mod_spec(layer, k0 + 2, row_fn),
        pl.BlockSpec((None, 1, D_MODEL), lambda i: (layer * 3 + 2 * which, 0, 0)),
        _resident((None, None, D_MODEL, 2 * D_FF), lambda i: (layer, which, 0, 0)),
        _resident((None, None, D_FF, D_MODEL), lambda i: (layer, which, 0, 0)),
    ]


def _ffn_call(x, modv, normv, w_in_b, w_out_b, layer, row_fn, tm):
    n = x.shape[0]
    return pl.pallas_call(
        _ffn_kernel,
        grid=(n // tm,),
        in_specs=[pl.BlockSpec((tm, D_MODEL), lambda i: (i, 0))] + _ffn_specs(layer, 0, row_fn),
        out_specs=pl.BlockSpec((tm, D_MODEL), lambda i: (i, 0)),
        out_shape=jax.ShapeDtypeStruct((n, D_MODEL), F32),
        compiler_params=_params(("parallel",)),
        name="ffn",
    )(x, modv, modv, modv, normv, w_in_b, w_out_b)


def _head_rms(y, gain):
    lane = lax.broadcasted_iota(jnp.int32, (1, LANES), 1)
    lo = lane < HEAD_DIM
    outs = []
    for c in range(y.shape[1] // LANES):
        blk = y[:, c * LANES:(c + 1) * LANES]
        sq = blk * blk
        s_lo = jnp.sum(jnp.where(lo, sq, 0.0), axis=-1, keepdims=True)
        s_hi = jnp.sum(jnp.where(lo, 0.0, sq), axis=-1, keepdims=True)
        ms = jnp.where(lo, s_lo, s_hi) * (1.0 / HEAD_DIM)
        outs.append(blk * lax.rsqrt(ms + EPS) * gain[:, c * LANES:(c + 1) * LANES])
    return outs[0] if len(outs) == 1 else jnp.concatenate(outs, axis=-1)


def _rope(x, cos, sin_signed):
    lane = lax.broadcasted_iota(jnp.int32, (1, LANES), 1)
    first = (lane & 31) < 16
    swapped = jnp.where(first, pltpu.roll(x, LANES - 16, 1), pltpu.roll(x, 16, 1))
    return x * cos + swapped * sin_signed


def _kv_operands(k, v):
    lane = lax.broadcasted_iota(jnp.int32, (1, LANES), 1)
    lo = lane < HEAD_DIM
    one_lo = jnp.where(lane == 0, 1.0, 0.0)
    one_hi = jnp.where(lane == HEAD_DIM, 1.0, 0.0)
    kr = pltpu.roll(k, HEAD_DIM, 1)
    vr = pltpu.roll(v, HEAD_DIM, 1)
    parts = [jnp.where(lo, k, kr), jnp.where(lo, kr, k),
             jnp.where(lo, v, one_hi), jnp.where(lo, one_lo, vr), jnp.where(lo, vr, one_hi), jnp.where(lo, one_lo, v)]
    return jnp.concatenate(parts, axis=-1).astype(BF16)


def _inproj_kernel(*refs, use_rope):
    x_ref, shift_ref, scale_ref, g_ref, w_ref, qg_ref, kg_ref, lb_ref = refs[:8]
    cos_ref, sin_ref = (refs[8], refs[9]) if use_rope else (None, None)
    y_ref, gates_ref, kva_ref, kvw_ref = refs[-4:]
    h = (_rms_rows(x_ref[...], g_ref[...]) * (1.0 + scale_ref[...]) + shift_ref[...]).astype(BF16)

    def proj(c0, c1):
        return _dot(h, w_ref[:, c0:c1])

    for k in range(GATES_W // D_MODEL):
        c0 = COL_GATES + k * D_MODEL
        gates_ref[:, k * D_MODEL:(k + 1) * D_MODEL] = _sigmoid(proj(c0, c0 + D_MODEL)).astype(BF16)
    t = proj(COL_HQ, COL_HF)
    y_ref[:, COL_HQ:COL_HF] = t * _sigmoid(t)
    for dirn in range(2):
        c0 = COL_HF + dirn * HG_W
        lb = lb_ref[:, dirn * HG_W:(dirn + 1) * HG_W]
        y_ref[:, c0:c0 + HG_W] = lb + (1.0 - lb) * _sigmoid(proj(c0, c0 + HG_W))
    y_ref[:, COL_AQ:COL_AK] = _head_rms(proj(COL_AQ, COL_AK), qg_ref[...])
    t = proj(COL_AK, COL_HQ)
    ak = _head_rms(t[:, :LANES], kg_ref[...])
    y_ref[:, COL_AK:COL_AV] = ak
    y_ref[:, COL_AV:COL_HQ] = t[:, LANES:]
    wk = t[:, COL_WK - COL_AK:COL_WV - COL_AK]
    if use_rope:
        ak = _rope(ak, cos_ref[...], sin_ref[...])
        wk = _rope(wk, cos_ref[...], sin_ref[...])
    kva_ref[...] = _kv_operands(ak, t[:, COL_AV - COL_AK:COL_WQ - COL_AK])
    kvw_ref[...] = _kv_operands(wk, t[:, COL_WV - COL_AK:COL_HQ - COL_AK])
    y_ref[:, COL_HI:COL_GATES] = proj(COL_HI, COL_GATES)


def _inproj_call(x, modv, normv, w_in_b, qk_norm_g, lb_all, layer, row_fn, tm, rope=None, seq=None):
    n = x.shape[0]
    use_rope = rope is not None
    qg = jnp.tile(qk_norm_g[layer, 0].astype(F32), (COL_AK - COL_AQ) // HEAD_DIM).reshape(1, COL_AK - COL_AQ)
    kg = jnp.tile(qk_norm_g[layer, 1].astype(F32), LANES // HEAD_DIM).reshape(1, LANES)
    in_specs = [
        pl.BlockSpec((tm, D_MODEL), lambda i: (i, 0)),
        _mod_spec(layer, 3, row_fn), _mod_spec(layer, 4, row_fn),
        pl.BlockSpec((None, 1, D_MODEL), lambda i: (layer * 3 + 1, 0, 0)),
        _resident((None, D_MODEL, IN_W), lambda i: (layer, 0, 0)),
        pl.BlockSpec((1, COL_AK - COL_AQ), lambda i: (0, 0)),
        pl.BlockSpec((1, LANES), lambda i: (0, 0)),
        pl.BlockSpec((None, 1, 2 * HG_W), lambda i: (layer, 0, 0)),
    ]
    args = [x, modv, modv, normv, w_in_b, qg, kg, lb_all]
    if use_rope:
        in_specs += [pl.BlockSpec((tm, LANES), lambda i: (i % (seq // tm), 0))] * 2
        args += list(rope)
    widths = (Y_W, GATES_W, KV_W, KV_W)
    dtypes = (F32, BF16, BF16, BF16)
    return pl.pallas_call(
        functools.partial(_inproj_kernel, use_rope=use_rope),
        grid=(n // tm,),
        in_specs=in_specs,
        out_specs=[pl.BlockSpec((tm, w), lambda i: (i, 0)) for w in widths],
        out_shape=[jax.ShapeDtypeStruct((n, w), dt) for w, dt in zip(widths, dtypes)],
        compiler_params=_params(("parallel",)),
        name="inproj",
    )(*args)


def _attn_kernel(*refs, tq, ts, seq, past, use_rope, use_sink, window, sink_off):
    it = iter(refs)
    sink_ref = next(it) if use_sink else None
    q_refs = (next(it), next(it))
    kv_ref = next(it)
    ck_ref = cv_ref = cos_ref = sin_ref = None
    if past:
        ck_ref, cv_ref = next(it), next(it)
    if use_rope:
        cos_ref, sin_ref = next(it), next(it)
    o_ref = next(it)
    ctx_scr = next(it) if past else None

    qi = pl.program_id(1)
    lane = lax.broadcasted_iota(jnp.int32, (1, LANES), 1)
    lo = lane < HEAD_DIM

    if past:
        @pl.when(qi == 0)
        def _build():
            ctx_scr[...] = _kv_operands(ck_ref[...], cv_ref[...])

    q0 = pl.multiple_of(qi * tq, tq)
    q_scale = (HEAD_DIM ** -0.5) * LOG2E

    def unit_segments(h):
        ctx = [(ctx_scr, slice(0, past), None)] if past else []
        if not window:
            return ctx + [(kv_ref, slice(0, seq), None)]
        span = ts + 2 * WINDOW
        t0 = q0 + h * ts
        start = pl.multiple_of(jnp.clip(t0 - WINDOW, 0, seq - span), WINDOW)
        t_pos = t0 + lax.broadcasted_iota(jnp.int32, (ts, 1), 0)
        s_pos = start + lax.broadcasted_iota(jnp.int32, (1, span), 1)
        band_ok = jnp.abs(t_pos - s_pos) <= WINDOW
        return ctx + [(kv_ref, pl.ds(start, span), band_ok)]

    def keys(g, seg):
        ref, rsl, _ = seg
        return ref[rsl, g * LANES:(g + 1) * LANES]

    def values(g, par, seg):
        ref, rsl, _ = seg
        c0 = (2 + 2 * g + par) * LANES
        return ref[rsl, c0:c0 + LANES]

    def score_phase(g, h, segments):
        xs = []
        for p in range(2):
            q = q_refs[g][h * ts:(h + 1) * ts, p * LANES:(p + 1) * LANES]
            if use_rope:
                tsl = pl.ds(q0 + h * ts, ts)
                q = _rope(q, cos_ref[tsl, :], sin_ref[tsl, :])
            xs.append(q * q_scale)
        q4 = jnp.concatenate([jnp.where(lo, xs[0], 0.0), jnp.where(lo, xs[1], 0.0),
                              jnp.where(lo, 0.0, xs[0]), jnp.where(lo, 0.0, xs[1])], axis=0).astype(BF16)
        return [_dot_nt(q4, keys(g, seg)) for seg in segments]

    def softmax_phase(g, scores, segments):
        probs = [[] for _ in scores]
        extra = []
        for blk, head in enumerate((0, 2, 1, 3)):
            rsl = slice(blk * ts, (blk + 1) * ts)
            rows = [s[rsl] if seg[2] is None else jnp.where(seg[2], s[rsl], NEG_INF) for s, seg in zip(scores, segments)]
            m = None
            for s in rows:
                ms = jnp.max(s, axis=-1, keepdims=True)
                m = ms if m is None else jnp.maximum(m, ms)
            if use_sink:
                sk = sink_ref[sink_off + g * 4 + head] * LOG2E
                m = jnp.maximum(m, sk)
                extra.append(jnp.exp2(sk - m))
            for i, s in enumerate(rows):
                probs[i].append(jnp.exp2(s - m).astype(BF16))
        return probs, extra

    def value_phase(g, probs, segments):
        acc = [None, None]
        for i, seg in enumerate(segments):
            for par in range(2):
                e = jnp.concatenate(probs[i][2 * par:2 * par + 2], axis=0)
                pv = _dot(e, values(g, par, seg))
                acc[par] = pv if acc[par] is None else acc[par] + pv
        return acc

    def output_phase(g, h, acc, extra):
        outs = []
        for blk in range(4):
            par = blk // 2
            a = acc[par][(blk % 2) * ts:(blk % 2 + 1) * ts]
            den = jnp.sum(jnp.where(lane == (HEAD_DIM if par == 0 else 0), a, 0.0), axis=-1, keepdims=True)
            if use_sink:
                den = den + extra[blk]
            outs.append(a * (1.0 / den))
        for p in range(2):
            o_pair = jnp.where(lo, outs[p], outs[2 + p])
            o_ref[h * ts:(h + 1) * ts, g * 256 + p * LANES:g * 256 + (p + 1) * LANES] = o_pair.astype(o_ref.dtype)

    units = [(g, h) for h in range(tq // ts) for g in range(ATT_KV)]
    segs = {h: unit_segments(h) for h in range(tq // ts)}
    scores = [score_phase(g, h, segs[h]) for g, h in units]
    accs, extras = [], []
    for (g, h), sc in zip(units, scores):
        probs, extra = softmax_phase(g, sc, segs[h])
        accs.append(value_phase(g, probs, segs[h]))
        extras.append(extra)
    for (g, h), acc, extra in zip(units, accs, extras):
        output_phase(g, h, acc, extra)


def _attn_call(y, kv, nb, seq, qcol, *, cache_k=None, cache_v=None, layer=0, rope=None, sink=None, window=False):
    tq = min(ATT_TQ, seq)
    ts = min(ATT_SUB, tq)
    nq = seq // tq
    past = 0 if cache_k is None else cache_k.shape[2]
    use_rope = rope is not None
    use_sink = sink is not None
    in_specs, args = [], []
    if use_sink:
        in_specs.append(pl.BlockSpec(memory_space=pltpu.SMEM))
        args.append(sink.reshape(-1).astype(F32))
    for g in range(ATT_KV):
        in_specs.append(pl.BlockSpec((tq, 256), lambda b, qi, g=g: (b * nq + qi, qcol // 256 + g)))
        args.append(y)
    in_specs.append(pl.BlockSpec((seq, KV_W), lambda b, qi: (b, 0)))
    args.append(kv)
    if past:
        cshape = cache_k.shape[:3] + (LANES,)
        in_specs += [pl.BlockSpec((None, None, past, LANES), lambda b, qi: (b, layer, 0, 0))] * 2
        args += [cache_k.reshape(cshape), cache_v.reshape(cshape)]
    if use_rope:
        in_specs += [pl.BlockSpec((seq, LANES), lambda b, qi: (0, 0))] * 2
        args += list(rope)
    kern = functools.partial(_attn_kernel, tq=tq, ts=ts, seq=seq, past=past, use_rope=use_rope,
                             use_sink=use_sink, window=window, sink_off=layer * 8)
    return pl.pallas_call(
        kern,
        grid=(nb, nq),
        in_specs=in_specs,
        out_specs=pl.BlockSpec((tq, BRANCH_W), lambda b, qi: (b * nq + qi, 0)),
        out_shape=jax.ShapeDtypeStruct((nb * seq, BRANCH_W), BF16),
        scratch_shapes=[pltpu.VMEM((past, KV_W), BF16)] if past else [],
        compiler_params=_params(("parallel", "arbitrary")),
        name="win" if window or use_sink else "att",
    )(*args)


def _pair_table(fwd):
    import numpy as np
    C = HG_CHUNK
    t = np.arange(C)[:, None]
    s = np.arange(C)[None, :]
    seen = (s <= t) if fwd else (s >= t)
    tab = np.full((C, C), -1, np.int32)
    hh, idx = C // 2, (C // HG_SUB).bit_length() - 2
    while hh >= HG_SUB:
        tab = np.where((t // (2 * hh) == s // (2 * hh)) & seen, idx, tab)
        hh, idx = hh // 2, idx - 1
    tab = np.where((t // HG_SUB == s // HG_SUB), np.where(seen, PAIR_SUB + s % HG_SUB, -1), tab)
    return tab.astype(np.int32)


def _hgrn_unit(q, k, b, v, st, sub_masks, level_masks, fwd):
    C = HG_CHUNK
    G = C // HG_SUB
    vb = v.astype(BF16)

    b3 = b.reshape(G, HG_SUB, HG_DK)
    q3 = q.reshape(G, HG_SUB, HG_DK)
    c3 = (jnp.log2(k) - b).reshape(G, HG_SUB, HG_DK)
    a3 = jnp.zeros((G, HG_SUB, C), F32)
    for jj in range(HG_SUB):
        kdec = jnp.exp2(b3 + c3[:, jj:jj + 1, :])
        score = jnp.sum(q3 * kdec, axis=-1, keepdims=True)
        a3 = jnp.where(sub_masks[jj], score, a3)
    a_mat = a3.reshape(C, C)

    tcol = lax.broadcasted_iota(jnp.int32, (C, 1), 0)
    hh, idx = HG_SUB, 0
    while hh < C:
        grp = 2 * hh
        later = (tcol & (grp - 1)) >= hh
        is_q = later if fwd else jnp.logical_not(later)
        bg = b.reshape(C // grp, grp, HG_DK)
        bnd = bg[:, hh - 1:hh, :] if fwd else bg[:, hh:hh + 1, :]
        bnd = jnp.broadcast_to(bnd, (C // grp, grp, HG_DK)).reshape(C, HG_DK)
        w = (jnp.where(is_q, q, k) * jnp.exp2(jnp.where(is_q, b - bnd, bnd - b))).astype(BF16)
        a_mat = jnp.where(level_masks[idx], _dot_nt(w, w), a_mat)
        hh, idx = grp, idx + 1
    out = _dot(a_mat.astype(BF16), vb)

    out = out + _dot_nt((q * jnp.exp2(b)).astype(BF16), st.astype(BF16))
    blast = b[C - 1:C, :] if fwd else b[0:1, :]
    kh = (k * jnp.exp2(blast - b)).astype(BF16)
    st_new = st * jnp.exp2(blast) + _dot_tn(vb, kh)
    return out, st_new


def _hgrn_kernel(*refs, nsteps, cps, has_init, emit_state):
    it = iter(refs)
    io = [(next(it), next(it), next(it)) for _ in range(2)]
    pair_ref = next(it)
    s0_ref = next(it) if has_init else None
    o_refs = (next(it), next(it))
    sfin_ref = next(it) if emit_state else None
    st_scr = next(it)

    j = pl.program_id(1)
    C = HG_CHUNK

    @pl.when(j == 0)
    def _():
        for dirn in range(2):
            for h in range(HG_HEADS):
                if has_init:
                    st_scr[dirn, h] = s0_ref[dirn, h].T
                else:
                    st_scr[dirn, h] = jnp.zeros((HG_DK, HG_DK), F32)

    ti = lax.broadcasted_iota(jnp.int32, (C, C), 0)
    si = lax.broadcasted_iota(jnp.int32, (C, C), 1)
    tris, masks = [], []
    for dirn in range(2):
        tris.append(jnp.where((ti >= si) if dirn == 0 else (ti <= si), 1.0, 0.0).astype(BF16))
        pair = pair_ref[dirn]
        pair3 = pair.reshape(C // HG_SUB, HG_SUB, C)
        masks.append(([pair3 == PAIR_SUB + jj for jj in range(HG_SUB)],
                      [pair == idx for idx in range((C // HG_SUB).bit_length() - 1)]))
    states = [[st_scr[dirn, h] for h in range(HG_HEADS)] for dirn in range(2)]
    for cc in range(cps):
        for dirn in range(2):
            fwd = dirn == 0
            q_ref, f_ref, v_ref = io[dirn]
            r0 = (cc if fwd else cps - 1 - cc) * C
            rows = slice(r0, r0 + C)
            f = f_ref[rows, :]
            l2 = jnp.log2(jnp.maximum(f, TINY))
            l_hi = l2.astype(BF16)
            r1 = l2 - l_hi.astype(F32)
            l_mid = r1.astype(BF16)
            l_lo = (r1 - l_mid.astype(F32)).astype(BF16)
            b_all = _dot(tris[dirn], l_hi) + _dot(tris[dirn], l_mid) + _dot(tris[dirn], l_lo)
            for h in range(HG_HEADS):
                sl = slice(h * HG_DK, (h + 1) * HG_DK)
                out, st_new = _hgrn_unit(q_ref[rows, sl], jnp.maximum(1.0 - f[:, sl], 0.0), b_all[:, sl],
                                         v_ref[rows, sl], states[dirn][h], masks[dirn][0], masks[dirn][1], fwd)
                o_refs[dirn][rows, sl] = out
                states[dirn][h] = st_new
    for dirn in range(2):
        for h in range(HG_HEADS):
            st_scr[dirn, h] = states[dirn][h]
            if emit_state:
                @pl.when(j == nsteps - 1)
                def _():
                    sfin_ref[dirn, h] = states[dirn][h].T


def _hgrn_call(y, nb, seq, layer, state0=None, emit_state=False):
    cps = HG_STEP_CHUNKS
    rows = cps * HG_CHUNK
    nsteps = seq // rows
    has_init = state0 is not None
    in_specs, args = [], []
    for dirn in range(2):
        rowblk = (lambda n, j: n * nsteps + j) if dirn == 0 else (lambda n, j: n * nsteps + nsteps - 1 - j)
        for col in (COL_HQ, COL_HF + dirn * HG_W, COL_HI):
            in_specs.append(pl.BlockSpec((rows, HG_W), lambda n, j, rowblk=rowblk, col=col: (rowblk(n, j), col // HG_W)))
            args.append(y)
    in_specs.append(pl.BlockSpec((2, HG_CHUNK, HG_CHUNK), lambda n, j: (0, 0, 0)))
    args.append(jnp.asarray([_pair_table(True), _pair_table(False)]))
    if has_init:
        in_specs.append(pl.BlockSpec((None, None, 2, HG_HEADS, HG_DK, HG_DK), lambda n, j: (n, layer, 0, 0, 0, 0)))
        args.append(state0)
    out_specs = [pl.BlockSpec((rows, HG_W), lambda n, j: (n * nsteps + j, 0)),
                 pl.BlockSpec((rows, HG_W), lambda n, j: (n * nsteps + nsteps - 1 - j, 0))]
    out_shape = [jax.ShapeDtypeStruct((nb * seq, HG_W), F32)] * 2
    if emit_state:
        out_specs.append(pl.BlockSpec((None, 2, HG_HEADS, HG_DK, HG_DK), lambda n, j: (n, 0, 0, 0, 0)))
        out_shape.append(jax.ShapeDtypeStruct((nb, 2, HG_HEADS, HG_DK, HG_DK), F32))
    res = pl.pallas_call(
        functools.partial(_hgrn_kernel, nsteps=nsteps, cps=cps, has_init=has_init, emit_state=emit_state),
        grid=(nb, nsteps),
        in_specs=in_specs,
        out_specs=out_specs,
        out_shape=out_shape,
        scratch_shapes=[pltpu.VMEM((2, HG_HEADS, HG_DK, HG_DK), F32)],
        compiler_params=_params(("parallel", "arbitrary")),
        name="hgrn",
    )(*args)
    return (res[0], res[1], res[2]) if emit_state else (res[0], res[1], None)


def _merge_kernel(*refs, final):
    (oa_ref, ow_ref, of_ref, ob_ref, hg_ref, gates_ref, x_ref, gate_ref, hgn_ref, wb_ref, wo_ref,
     shift2_ref, scale2_ref, gate2_ref, g2_ref, wgu_ref, wd_ref) = refs[:17]
    fg_ref = refs[17] if final else None
    o_ref = refs[-1]
    o = of_ref[...] + ob_ref[...]
    hg = hg_ref[...]
    hgn = hgn_ref[...]
    parts = []
    for h in range(HG_HEADS):
        sl = slice(h * HG_DK, (h + 1) * HG_DK)
        g = hg[:, sl]
        parts.append(_rms_rows(o[:, sl], hgn) * (g * _sigmoid(g)))
    o_hg = jnp.concatenate(parts, axis=-1).astype(BF16)
    branch = (oa_ref[...], o_hg, ow_ref[...])
    merged = None
    for k in range(3):
        term = gates_ref[:, k * D_MODEL:(k + 1) * D_MODEL].astype(F32) * _dot(branch[k], wb_ref[k])
        merged = term if merged is None else merged + term
    yv = _dot(merged.astype(BF16), wo_ref[...])
    x_mid = x_ref[...] + gate_ref[...] * yv
    _ffn_block(x_mid, shift2_ref[...], scale2_ref[...], gate2_ref[...], g2_ref[...], wgu_ref, wd_ref, o_ref,
               fg_ref[...] if final else None)


def _merge_call(x, y, gates, o_att, o_win, o_f, o_b, modv, normv, hg_norm_g, w_branch_b, w_out_b,
                w_ffn_in_b, w_ffn_out_b, layer, row_fn, tm, final_g=None):
    n = x.shape[0]
    final = final_g is not None
    row = lambda i: (i, 0)
    in_specs = [
        pl.BlockSpec((tm, BRANCH_W), row),
        pl.BlockSpec((tm, BRANCH_W), row),
        pl.BlockSpec((tm, HG_W), row),
        pl.BlockSpec((tm, HG_W), row),
        pl.BlockSpec((tm, HG_W), lambda i: (i, COL_HG // HG_W)),
        pl.BlockSpec((tm, GATES_W), row),
        pl.BlockSpec((tm, D_MODEL), row),
        _mod_spec(layer, 5, row_fn),
        pl.BlockSpec((1, HG_DK), lambda i: (0, 0)),
        _resident((None, 3, BRANCH_W, D_MODEL), lambda i: (layer, 0, 0, 0)),
        _resident((None, D_MODEL, D_MODEL), lambda i: (layer, 0, 0)),
    ] + _ffn_specs(layer, 1, row_fn)
    args = [o_att, o_win, o_f, o_b, y, gates, x, modv, hg_norm_g[layer].reshape(1, HG_DK).astype(F32),
            w_branch_b, w_out_b, modv, modv, modv, normv, w_ffn_in_b, w_ffn_out_b]
    if final:
        in_specs.append(pl.BlockSpec((1, D_MODEL), lambda i: (0, 0)))
        args.append(final_g.reshape(1, D_MODEL))
    return pl.pallas_call(
        functools.partial(_merge_kernel, final=final),
        grid=(n // tm,),
        in_specs=in_specs,
        out_specs=pl.BlockSpec((tm, D_MODEL), row),
        out_shape=jax.ShapeDtypeStruct((n, D_MODEL), F32),
        compiler_params=_params(("parallel",)),
        name="merge_ffn",
    )(*args)


def _rope_tables(n_lat):
    t = jnp.arange(n_lat, dtype=jnp.int32)
    row = (t // GRID_W).astype(F32)
    col = (t % GRID_W).astype(F32)
    axis_dim = HEAD_DIM // 2
    inv = ROPE_THETA ** (-jnp.arange(0, axis_dim, 2, dtype=F32) / axis_dim)
    ang_r = row[:, None] * inv
    ang_c = col[:, None] * inv
    cos64 = jnp.concatenate([jnp.cos(ang_r), jnp.cos(ang_r), jnp.cos(ang_c), jnp.cos(ang_c)], axis=-1)
    sin64 = jnp.concatenate([-jnp.sin(ang_r), jnp.sin(ang_r), -jnp.sin(ang_c), jnp.sin(ang_c)], axis=-1)
    return jnp.tile(cos64, (1, 2)), jnp.tile(sin64, (1, 2))


def kernel(x_prompt, x_sample, cache_k_attn, cache_v_attn, cache_k_win, cache_v_win, state_hgrn, c, c_ctx,
           w_mod, b_mod, norm_g, w_ffn_in, w_ffn_out, w_in, qk_norm_g, lower_bounds, hg_norm_g, sink_logit,
           w_branch, w_out, final_norm_g):
    batch, seq, _ = x_prompt.shape
    dec_batch, dec_seq, _ = x_sample.shape
    depth = w_mod.shape[0]

    cond = jnp.zeros((COND_ROWS, D_MODEL), F32).at[0].set(c_ctx).at[1:1 + dec_batch].set(c)
    modv = _mod_call(cond, w_mod, b_mod).reshape(depth * COND_ROWS * N_MOD, 1, D_MODEL)
    lb_all = _lb_call(lower_bounds).reshape(depth, 1, 2 * HG_W)
    normv = norm_g.astype(F32).reshape(depth * 3, 1, D_MODEL)
    w_ffn_in_b = w_ffn_in.astype(BF16)
    w_ffn_out_b = w_ffn_out.astype(BF16)
    w_in_b = w_in.astype(BF16)
    w_branch_b = w_branch.astype(BF16)
    w_out_b = w_out.astype(BF16)
    rope = _rope_tables(dec_seq)

    def run(x, nb, s, latent):
        n = nb * s

        def rows_of(tm):
            return (lambda i: 1 + i // (s // tm)) if latent else (lambda i: 0)

        tm = min(DENSE_TM, s if latent else n)
        tm_ffn = min(FFN_TM, s if latent else n)
        x = x.reshape(n, D_MODEL)
        ctx_out = []
        for l in range(depth):
            x = _ffn_call(x, modv, normv, w_ffn_in_b, w_ffn_out_b, l, rows_of(tm_ffn), tm_ffn)
            y, gates, kv_att, kv_win = _inproj_call(x, modv, normv, w_in_b, qk_norm_g, lb_all, l, rows_of(tm), tm,
                                                    rope=rope if latent else None, seq=s)
            if latent:
                o_att = _attn_call(y, kv_att, nb, s, COL_AQ, cache_k=cache_k_attn, cache_v=cache_v_attn,
                                   layer=l, rope=rope)
                o_win = _attn_call(y, kv_win, nb, s, COL_WQ, cache_k=cache_k_win, cache_v=cache_v_win,
                                   layer=l, rope=rope, sink=sink_logit, window=True)
                o_f, o_b, _ = _hgrn_call(y, nb, s, l, state0=state_hgrn)
            else:
                o_att = _attn_call(y, kv_att, nb, s, COL_AQ)
                o_win = _attn_call(y, kv_win, nb, s, COL_WQ, layer=l, sink=sink_logit)
                o_f, o_b, s_fin = _hgrn_call(y, nb, s, l, emit_state=True)
                kv = lambda col: y[:, col:col + LANES].reshape(nb, s, ATT_KV, HEAD_DIM)
                ctx_out.append((kv(COL_AK), kv(COL_AV), kv(COL_WK), kv(COL_WV), s_fin))
            x = _merge_call(x, y, gates, o_att, o_win, o_f, o_b, modv, normv, hg_norm_g, w_branch_b, w_out_b,
                            w_ffn_in_b, w_ffn_out_b, l, rows_of(tm), tm,
                            final_g=final_norm_g.astype(F32) if l == depth - 1 else None)
        return x.reshape(nb, s, D_MODEL), ctx_out

    y_prompt, ctx_out = run(x_prompt, batch, seq, False)
    y_sample, _ = run(x_sample, dec_batch, dec_seq, True)
    stack = lambda k: jnp.stack([cx[k] for cx in ctx_out], axis=1)
    return (y_prompt, y_sample, stack(0), stack(1), stack(2), stack(3), stack(4))
```

```python
import functools

import jax
import jax.numpy as jnp
import numpy as np
from jax import lax
from jax.experimental import pallas as pl
from jax.experimental.pallas import tpu as pltpu

F32 = jnp.float32
BF16 = jnp.bfloat16

D_MODEL = 1024
GRID_W = 64
HEAD_DIM = 64
ATT_KV = 2
ATT_HEADS = 8
GROUP_W = ATT_HEADS // ATT_KV * HEAD_DIM
WINDOW = 128
HG_HEADS = 4
HG_DK = 128
HG_W = HG_HEADS * HG_DK
BRANCH_W = 512
D_FF = 2816
ROPE_THETA = 10000.0
EPS = 1e-6
NEG_INF = -1e30
TINY = 1e-30
LOG2E = 1.4426950408889634
N_MOD = 9
IN_W = 7168

LANES = 128
SUBLANES = 8

COND_ROWS = 16
DENSE_TM = 512
FFN_TM = 1024
FFN_TF = 256
INPROJ_TN = 256
ATT_TQ = 256
ATT_SUB = 128
HG_CHUNK = 128
HG_STEP_CHUNKS = 2
HG_SUB = SUBLANES
PAIR_SUB = 16
VMEM_LIMIT = 56 * 1024 * 1024

COL_AQ, COL_AK, COL_AV = 0, 512, 640
COL_WQ, COL_WK, COL_WV = 768, 1280, 1408
COL_HQ, COL_HF, COL_HI, COL_HG = 1536, 2048, 3072, 3584
COL_GATES = 4096
Y_W = COL_GATES
KV_W = 6 * LANES
GATES_W = IN_W - COL_GATES


def _params(sem):
    return pltpu.CompilerParams(dimension_semantics=sem, vmem_limit_bytes=VMEM_LIMIT)


def _resident(block_shape, index_map):
    return pl.BlockSpec(block_shape, index_map, pipeline_mode=pl.Buffered(1))


def _sigmoid(x):
    return 1.0 / (1.0 + jnp.exp(-x))


def _dot(a, b):
    return jnp.dot(a, b, preferred_element_type=F32)


def _dot_nt(a, b):
    return lax.dot_general(a, b, (((1,), (1,)), ((), ())), preferred_element_type=F32)


def _dot_tn(a, b):
    return lax.dot_general(a, b, (((0,), (0,)), ((), ())), preferred_element_type=F32)


def _rms_rows(x, gain):
    return x * lax.rsqrt(jnp.mean(x * x, axis=-1, keepdims=True) + EPS) * gain


def _mod_kernel(c_ref, w_ref, b_ref, o_ref):
    c = c_ref[...]
    h = (c * _sigmoid(c)).astype(BF16)
    o_ref[...] = _dot(h, w_ref[...].astype(BF16)) + b_ref[...]


def _mod_call(cond, w_mod, b_mod):
    depth, d, nm = w_mod.shape
    tn = nm // 8
    return pl.pallas_call(
        _mod_kernel,
        grid=(depth, nm // tn),
        in_specs=[pl.BlockSpec((COND_ROWS, d), lambda l, j: (0, 0)),
                  pl.BlockSpec((None, d, tn), lambda l, j: (l, 0, j)),
                  pl.BlockSpec((None, 1, tn), lambda l, j: (l, 0, j))],
        out_specs=pl.BlockSpec((None, COND_ROWS, tn), lambda l, j: (l, 0, j)),
        out_shape=jax.ShapeDtypeStruct((depth, COND_ROWS, nm), F32),
        compiler_params=_params(("parallel", "parallel")),
        name="mod",
    )(cond, w_mod, b_mod.reshape(depth, 1, nm))


def _lb_kernel(x_ref, o_ref):
    x = x_ref[...]
    m = jnp.max(x, axis=0, keepdims=True)
    e = jnp.exp(x - m)
    s = e / jnp.sum(e, axis=0, keepdims=True)
    acc = jnp.zeros_like(s[0:1])
    for l in range(x.shape[0]):
        acc = acc + s[l:l + 1]
        o_ref[l:l + 1, :] = acc - s[0:1]


def _lb_call(lower_bounds):
    depth = lower_bounds.shape[0]
    x = lower_bounds.reshape(depth, -1).astype(F32)
    return pl.pallas_call(_lb_kernel, out_shape=jax.ShapeDtypeStruct(x.shape, F32), name="lb")(x)


def _ffn_block(x, shift, scale, gate, g, wgu_ref, wd_ref, o_ref, final_g):
    h = (_rms_rows(x, g) * (1.0 + scale) + shift).astype(BF16)
    for c in range(D_FF // FFN_TF):
        a = _dot(h, wgu_ref[:, c * FFN_TF:(c + 1) * FFN_TF])
        u = _dot(h, wgu_ref[:, D_FF + c * FFN_TF:D_FF + (c + 1) * FFN_TF])
        act = (a * _sigmoid(a) * u).astype(BF16)
        part = _dot(act, wd_ref[c * FFN_TF:(c + 1) * FFN_TF, :])
        if c == 0:
            o_ref[...] = part
        else:
            o_ref[...] += part
    out = x + 0.5 * gate * o_ref[...]
    if final_g is not None:
        out = _rms_rows(out, final_g)
    o_ref[...] = out


def _ffn_kernel(x_ref, shift_ref, scale_ref, gate_ref, g_ref, wgu_ref, wd_ref, o_ref):
    _ffn_block(x_ref[...], shift_ref[...], scale_ref[...], gate_ref[...], g_ref[...], wgu_ref, wd_ref, o_ref, None)


def _mod_spec(layer, k, row_fn):
    return pl.BlockSpec((None, 1, D_MODEL), lambda i: ((layer * COND_ROWS + row_fn(i)) * N_MOD + k, 0, 0))


def _ffn_specs(layer, which, row_fn):
    k0 = 3 * (2 * which)
    return [
        _mod_spec(layer, k0, row_fn), _mod_spec(layer, k0 + 1, row_fn), _mod_spec(layer, k0 + 2, row_fn),
        pl.BlockSpec((None, 1, D_MODEL), lambda i: (layer * 3 + 2 * which, 0, 0)),
        _resident((None, None, D_MODEL, 2 * D_FF), lambda i: (layer, which, 0, 0)),
        _resident((None, None, D_FF, D_MODEL), lambda i: (layer, which, 0, 0)),
    ]


def _ffn_call(x, modv, normv, w_in_b, w_out_b, layer, row_fn, tm):
    n = x.shape[0]
    return pl.pallas_call(
        _ffn_kernel,
        grid=(n // tm,),
        in_specs=[pl.BlockSpec((tm, D_MODEL), lambda i: (i, 0))] + _ffn_specs(layer, 0, row_fn),
        out_specs=pl.BlockSpec((tm, D_MODEL), lambda i: (i, 0)),
        out_shape=jax.ShapeDtypeStruct((n, D_MODEL), F32),
        compiler_params=_params(("parallel",)),
        name="ffn",
    )(x, modv, modv, modv, normv, w_in_b, w_out_b)


def _head_rms(y, gain):
    lane = lax.broadcasted_iota(jnp.int32, (1, LANES), 1)
    lo = lane < HEAD_DIM
    outs = []
    for c in range(y.shape[1] // LANES):
        blk = y[:, c * LANES:(c + 1) * LANES]
        sq = blk * blk
        s_lo = jnp.sum(jnp.where(lo, sq, 0.0), axis=-1, keepdims=True)
        s_hi = jnp.sum(jnp.where(lo, 0.0, sq), axis=-1, keepdims=True)
        ms = jnp.where(lo, s_lo, s_hi) * (1.0 / HEAD_DIM)
        outs.append(blk * lax.rsqrt(ms + EPS) * gain[:, c * LANES:(c + 1) * LANES])
    return outs[0] if len(outs) == 1 else jnp.concatenate(outs, axis=-1)


def _rope(x, cos, sin_signed):
    lane = lax.broadcasted_iota(jnp.int32, (1, LANES), 1)
    first = (lane & 31) < 16
    swapped = jnp.where(first, pltpu.roll(x, LANES - 16, 1), pltpu.roll(x, 16, 1))
    return x * cos + swapped * sin_signed


def _kv_operands(k, v):
    lane = lax.broadcasted_iota(jnp.int32, (1, LANES), 1)
    lo = lane < HEAD_DIM
    one_lo = jnp.where(lane == 0, 1.0, 0.0)
    one_hi = jnp.where(lane == HEAD_DIM, 1.0, 0.0)
    kr = pltpu.roll(k, HEAD_DIM, 1)
    vr = pltpu.roll(v, HEAD_DIM, 1)
    parts = [jnp.where(lo, k, kr), jnp.where(lo, kr, k),
             jnp.where(lo, v, one_hi), jnp.where(lo, one_lo, vr), jnp.where(lo, vr, one_hi), jnp.where(lo, one_lo, v)]
    return jnp.concatenate(parts, axis=-1).astype(BF16)


def _inproj_kernel(*refs, use_rope):
    x_ref, shift_ref, scale_ref, g_ref, w_ref, qg_ref, kg_ref, lb_ref = refs[:8]
    cos_ref, sin_ref = (refs[8], refs[9]) if use_rope else (None, None)
    y_ref, gates_ref, kva_ref, kvw_ref = refs[-4:]
    h = (_rms_rows(x_ref[...], g_ref[...]) * (1.0 + scale_ref[...]) + shift_ref[...]).astype(BF16)

    def proj(c0, c1):
        return _dot(h, w_ref[:, c0:c1])

    tn = INPROJ_TN
    for c in range(COL_GATES, IN_W, tn):
        gates_ref[:, c - COL_GATES:c - COL_GATES + tn] = _sigmoid(proj(c, c + tn)).astype(BF16)
    for c in range(COL_HQ, COL_HF, tn):
        t = proj(c, c + tn)
        y_ref[:, c:c + tn] = t * _sigmoid(t)
    for c in range(COL_HF, COL_HI, tn):
        lb = lb_ref[:, c - COL_HF:c - COL_HF + tn]
        y_ref[:, c:c + tn] = lb + (1.0 - lb) * _sigmoid(proj(c, c + tn))
    for c in range(COL_AQ, COL_AK, tn):
        y_ref[:, c:c + tn] = _head_rms(proj(c, c + tn), qg_ref[:, c:c + tn])
    for ck, out_ref in ((COL_AK, kva_ref), (COL_WK, kvw_ref)):
        t = proj(ck, ck + 2 * LANES)
        k = t[:, :LANES]
        if ck == COL_AK:
            k = _head_rms(k, kg_ref[...])
        y_ref[:, ck:ck + LANES] = k
        y_ref[:, ck + LANES:ck + 2 * LANES] = t[:, LANES:]
        if use_rope:
            k = _rope(k, cos_ref[...], sin_ref[...])
        out_ref[...] = _kv_operands(k, t[:, LANES:])
    for c in list(range(COL_WQ, COL_WK, tn)) + list(range(COL_HI, COL_GATES, tn)):
        y_ref[:, c:c + tn] = proj(c, c + tn)


def _inproj_call(x, modv, normv, w_in_b, qk_norm_g, lb_all, layer, row_fn, tm, rope=None, seq=None):
    n = x.shape[0]
    use_rope = rope is not None
    qg = jnp.tile(qk_norm_g[layer, 0].astype(F32), (COL_AK - COL_AQ) // HEAD_DIM).reshape(1, COL_AK - COL_AQ)
    kg = jnp.tile(qk_norm_g[layer, 1].astype(F32), LANES // HEAD_DIM).reshape(1, LANES)
    in_specs = [
        pl.BlockSpec((tm, D_MODEL), lambda i: (i, 0)),
        _mod_spec(layer, 3, row_fn), _mod_spec(layer, 4, row_fn),
        pl.BlockSpec((None, 1, D_MODEL), lambda i: (layer * 3 + 1, 0, 0)),
        _resident((None, D_MODEL, IN_W), lambda i: (layer, 0, 0)),
        pl.BlockSpec((1, COL_AK - COL_AQ), lambda i: (0, 0)),
        pl.BlockSpec((1, LANES), lambda i: (0, 0)),
        pl.BlockSpec((None, 1, 2 * HG_W), lambda i: (layer, 0, 0)),
    ]
    args = [x, modv, modv, normv, w_in_b, qg, kg, lb_all]
    if use_rope:
        in_specs += [pl.BlockSpec((tm, LANES), lambda i: (i % (seq // tm), 0))] * 2
        args += list(rope)
    widths = (Y_W, GATES_W, KV_W, KV_W)
    dtypes = (F32, BF16, BF16, BF16)
    return pl.pallas_call(
        functools.partial(_inproj_kernel, use_rope=use_rope),
        grid=(n // tm,),
        in_specs=in_specs,
        out_specs=[pl.BlockSpec((tm, w), lambda i: (i, 0)) for w in widths],
        out_shape=[jax.ShapeDtypeStruct((n, w), dt) for w, dt in zip(widths, dtypes)],
        compiler_params=_params(("parallel",)),
        name="inproj",
    )(*args)


def _attn_kernel(*refs, tq, ts, seq, past, use_rope, use_sink, window, sink_off):
    it = iter(refs)
    sink_ref = next(it) if use_sink else None
    q_refs = (next(it), next(it))
    kv_ref = next(it)
    ck_ref = cv_ref = cos_ref = sin_ref = None
    if past:
        ck_ref, cv_ref = next(it), next(it)
    if use_rope:
        cos_ref, sin_ref = next(it), next(it)
    o_ref = next(it)
    ctx_scr = next(it) if past else None

    qi = pl.program_id(1)
    lane = lax.broadcasted_iota(jnp.int32, (1, LANES), 1)
    lo = lane < HEAD_DIM

    if past:
        @pl.when(qi == 0)
        def _build():
            ctx_scr[...] = _kv_operands(ck_ref[...], cv_ref[...])

    q0 = pl.multiple_of(qi * tq, tq)
    q_scale = (HEAD_DIM ** -0.5) * LOG2E

    def unit_segments(h):
        ctx = [(ctx_scr, slice(0, past), None)] if past else []
        if not window:
            return ctx + [(kv_ref, slice(0, seq), None)]
        span = ts + 2 * WINDOW
        t0 = q0 + h * ts
        start = pl.multiple_of(jnp.clip(t0 - WINDOW, 0, seq - span), WINDOW)
        t_pos = t0 + lax.broadcasted_iota(jnp.int32, (ts, 1), 0)
        s_pos = start + lax.broadcasted_iota(jnp.int32, (1, span), 1)
        band_ok = jnp.abs(t_pos - s_pos) <= WINDOW
        return ctx + [(kv_ref, pl.ds(start, span), band_ok)]

    def keys(g, seg):
        ref, rsl, _ = seg
        return ref[rsl, g * LANES:(g + 1) * LANES]

    def values(g, par, seg):
        ref, rsl, _ = seg
        c0 = (2 + 2 * g + par) * LANES
        return ref[rsl, c0:c0 + LANES]

    def score_phase(g, h, segments):
        xs = []
        for p in range(2):
            q = q_refs[g][h * ts:(h + 1) * ts, p * LANES:(p + 1) * LANES]
            if use_rope:
                tsl = pl.ds(q0 + h * ts, ts)
                q = _rope(q, cos_ref[tsl, :], sin_ref[tsl, :])
            xs.append(q * q_scale)
        q4 = jnp.concatenate([jnp.where(lo, xs[0], 0.0), jnp.where(lo, xs[1], 0.0),
                              jnp.where(lo, 0.0, xs[0]), jnp.where(lo, 0.0, xs[1])], axis=0).astype(BF16)
        return [_dot_nt(q4, keys(g, seg)) for seg in segments]

    def softmax_phase(g, scores, segments):
        probs = [[] for _ in scores]
        extra = []
        for blk, head in enumerate((0, 2, 1, 3)):
            rsl = slice(blk * ts, (blk + 1) * ts)
            rows = [s[rsl] if seg[2] is None else jnp.where(seg[2], s[rsl], NEG_INF) for s, seg in zip(scores, segments)]
            m = None
            for s in rows:
                ms = jnp.max(s, axis=-1, keepdims=True)
                m = ms if m is None else jnp.maximum(m, ms)
            if use_sink:
                sk = sink_ref[sink_off + g * (ATT_HEADS // ATT_KV) + head] * LOG2E
                m = jnp.maximum(m, sk)
                extra.append(jnp.exp2(sk - m))
            for i, s in enumerate(rows):
                probs[i].append(jnp.exp2(s - m).astype(BF16))
        return probs, extra

    def value_phase(g, probs, segments):
        acc = [None, None]
        for i, seg in enumerate(segments):
            for par in range(2):
                e = jnp.concatenate(probs[i][2 * par:2 * par + 2], axis=0)
                pv = _dot(e, values(g, par, seg))
                acc[par] = pv if acc[par] is None else acc[par] + pv
        return acc

    def output_phase(g, h, acc, extra):
        outs = []
        for blk in range(4):
            par = blk // 2
            a = acc[par][(blk % 2) * ts:(blk % 2 + 1) * ts]
            den = jnp.sum(jnp.where(lane == (HEAD_DIM if par == 0 else 0), a, 0.0), axis=-1, keepdims=True)
            if use_sink:
                den = den + extra[blk]
            outs.append(a * (1.0 / den))
        for p in range(2):
            o_pair = jnp.where(lo, outs[p], outs[2 + p])
            c0 = g * GROUP_W + p * LANES
            o_ref[h * ts:(h + 1) * ts, c0:c0 + LANES] = o_pair.astype(o_ref.dtype)

    units = [(g, h) for h in range(tq // ts) for g in range(ATT_KV)]
    segs = {h: unit_segments(h) for h in range(tq // ts)}
    scores = [score_phase(g, h, segs[h]) for g, h in units]
    accs, extras = [], []
    for (g, h), sc in zip(units, scores):
        probs, extra = softmax_phase(g, sc, segs[h])
        accs.append(value_phase(g, probs, segs[h]))
        extras.append(extra)
    for (g, h), acc, extra in zip(units, accs, extras):
        output_phase(g, h, acc, extra)


def _attn_call(y, kv, nb, seq, qcol, *, cache_k=None, cache_v=None, layer=0, rope=None, sink=None, window=False):
    tq = min(ATT_TQ, seq)
    ts = min(ATT_SUB, tq)
    nq = seq // tq
    past = 0 if cache_k is None else cache_k.shape[2]
    use_rope = rope is not None
    use_sink = sink is not None
    in_specs, args = [], []
    if use_sink:
        in_specs.append(pl.BlockSpec(memory_space=pltpu.SMEM))
        args.append(sink.reshape(-1).astype(F32))
    for g in range(ATT_KV):
        in_specs.append(pl.BlockSpec((tq, GROUP_W), lambda b, qi, g=g: (b * nq + qi, qcol // GROUP_W + g)))
        args.append(y)
    in_specs.append(pl.BlockSpec((seq, KV_W), lambda b, qi: (b, 0)))
    args.append(kv)
    if past:
        cshape = cache_k.shape[:3] + (LANES,)
        in_specs += [pl.BlockSpec((None, None, past, LANES), lambda b, qi: (b, layer, 0, 0))] * 2
        args += [cache_k.reshape(cshape), cache_v.reshape(cshape)]
    if use_rope:
        in_specs += [pl.BlockSpec((seq, LANES), lambda b, qi: (0, 0))] * 2
        args += list(rope)
    kern = functools.partial(_attn_kernel, tq=tq, ts=ts, seq=seq, past=past, use_rope=use_rope,
                             use_sink=use_sink, window=window, sink_off=layer * ATT_HEADS)
    return pl.pallas_call(
        kern,
        grid=(nb, nq),
        in_specs=in_specs,
        out_specs=pl.BlockSpec((tq, BRANCH_W), lambda b, qi: (b * nq + qi, 0)),
        out_shape=jax.ShapeDtypeStruct((nb * seq, BRANCH_W), BF16),
        scratch_shapes=[pltpu.VMEM((past, KV_W), BF16)] if past else [],
        compiler_params=_params(("parallel", "arbitrary")),
        name="win" if window or use_sink else "att",
    )(*args)


def _pair_table(fwd):
    C = HG_CHUNK
    t = np.arange(C)[:, None]
    s = np.arange(C)[None, :]
    seen = (s <= t) if fwd else (s >= t)
    tab = np.full((C, C), -1, np.int32)
    hh, idx = C // 2, (C // HG_SUB).bit_length() - 2
    while hh >= HG_SUB:
        tab = np.where((t // (2 * hh) == s // (2 * hh)) & seen, idx, tab)
        hh, idx = hh // 2, idx - 1
    tab = np.where((t // HG_SUB == s // HG_SUB), np.where(seen, PAIR_SUB + s % HG_SUB, -1), tab)
    return tab.astype(np.int32)


def _hgrn_unit(q, k, b, v, st, sub_masks, level_masks, fwd):
    C = HG_CHUNK
    G = C // HG_SUB
    vb = v.astype(BF16)

    b3 = b.reshape(G, HG_SUB, HG_DK)
    q3 = q.reshape(G, HG_SUB, HG_DK)
    c3 = (jnp.log2(k) - b).reshape(G, HG_SUB, HG_DK)
    a3 = jnp.zeros((G, HG_SUB, C), F32)
    for jj in range(HG_SUB):
        kdec = jnp.exp2(b3 + c3[:, jj:jj + 1, :])
        score = jnp.sum(q3 * kdec, axis=-1, keepdims=True)
        a3 = jnp.where(sub_masks[jj], score, a3)
    a_mat = a3.reshape(C, C)

    tcol = lax.broadcasted_iota(jnp.int32, (C, 1), 0)
    hh, idx = HG_SUB, 0
    while hh < C:
        grp = 2 * hh
        later = (tcol & (grp - 1)) >= hh
        is_q = later if fwd else jnp.logical_not(later)
        bg = b.reshape(C // grp, grp, HG_DK)
        bnd = bg[:, hh - 1:hh, :] if fwd else bg[:, hh:hh + 1, :]
        bnd = jnp.broadcast_to(bnd, (C // grp, grp, HG_DK)).reshape(C, HG_DK)
        w = (jnp.where(is_q, q, k) * jnp.exp2(jnp.where(is_q, b - bnd, bnd - b))).astype(BF16)
        a_mat = jnp.where(level_masks[idx], _dot_nt(w, w), a_mat)
        hh, idx = grp, idx + 1
    out = _dot(a_mat.astype(BF16), vb)

    out = out + _dot_nt((q * jnp.exp2(b)).astype(BF16), st.astype(BF16))
    blast = b[C - 1:C, :] if fwd else b[0:1, :]
    kh = (k * jnp.exp2(blast - b)).astype(BF16)
    st_new = st * jnp.exp2(blast) + _dot_tn(vb, kh)
    return out, st_new


def _hgrn_kernel(*refs, nsteps, cps, has_init, emit_state):
    it = iter(refs)
    io = [(next(it), next(it), next(it)) for _ in range(2)]
    pair_ref = next(it)
    s0_ref = next(it) if has_init else None
    o_refs = (next(it), next(it))
    sfin_ref = next(it) if emit_state else None
    st_scr = next(it)

    j = pl.program_id(1)
    C = HG_CHUNK

    @pl.when(j == 0)
    def _():
        for dirn in range(2):
            for h in range(HG_HEADS):
                if has_init:
                    st_scr[dirn, h] = s0_ref[dirn, h].T
                else:
                    st_scr[dirn, h] = jnp.zeros((HG_DK, HG_DK), F32)

    ti = lax.broadcasted_iota(jnp.int32, (C, C), 0)
    si = lax.broadcasted_iota(jnp.int32, (C, C), 1)
    tris, masks = [], []
    for dirn in range(2):
        tris.append(jnp.where((ti >= si) if dirn == 0 else (ti <= si), 1.0, 0.0).astype(BF16))
        pair = pair_ref[dirn]
        pair3 = pair.reshape(C // HG_SUB, HG_SUB, C)
        masks.append(([pair3 == PAIR_SUB + jj for jj in range(HG_SUB)],
                      [pair == idx for idx in range((C // HG_SUB).bit_length() - 1)]))
    states = [[st_scr[dirn, h] for h in range(HG_HEADS)] for dirn in range(2)]
    for cc in range(cps):
        for dirn in range(2):
            fwd = dirn == 0
            q_ref, f_ref, v_ref = io[dirn]
            r0 = (cc if fwd else cps - 1 - cc) * C
            rows = slice(r0, r0 + C)
            f = f_ref[rows, :]
            l2 = jnp.log2(jnp.maximum(f, TINY))
            l_hi = l2.astype(BF16)
            r1 = l2 - l_hi.astype(F32)
            l_mid = r1.astype(BF16)
            l_lo = (r1 - l_mid.astype(F32)).astype(BF16)
            b_all = _dot(tris[dirn], l_hi) + _dot(tris[dirn], l_mid) + _dot(tris[dirn], l_lo)
            for h in range(HG_HEADS):
                sl = slice(h * HG_DK, (h + 1) * HG_DK)
                out, st_new = _hgrn_unit(q_ref[rows, sl], jnp.maximum(1.0 - f[:, sl], 0.0), b_all[:, sl],
                                         v_ref[rows, sl], states[dirn][h], masks[dirn][0], masks[dirn][1], fwd)
                o_refs[dirn][rows, sl] = out
                states[dirn][h] = st_new
    for dirn in range(2):
        for h in range(HG_HEADS):
            st_scr[dirn, h] = states[dirn][h]
            if emit_state:
                @pl.when(j == nsteps - 1)
                def _():
                    sfin_ref[dirn, h] = states[dirn][h].T


def _hgrn_call(y, nb, seq, layer, state0=None, emit_state=False):
    cps = HG_STEP_CHUNKS
    rows = cps * HG_CHUNK
    nsteps = seq // rows
    has_init = state0 is not None
    in_specs, args = [], []
    for dirn in range(2):
        rowblk = (lambda n, j: n * nsteps + j) if dirn == 0 else (lambda n, j: n * nsteps + nsteps - 1 - j)
        for col in (COL_HQ, COL_HF + dirn * HG_W, COL_HI):
            in_specs.append(pl.BlockSpec((rows, HG_W), lambda n, j, rowblk=rowblk, col=col: (rowblk(n, j), col // HG_W)))
            args.append(y)
    in_specs.append(pl.BlockSpec((2, HG_CHUNK, HG_CHUNK), lambda n, j: (0, 0, 0)))
    args.append(jnp.asarray([_pair_table(True), _pair_table(False)]))
    if has_init:
        in_specs.append(pl.BlockSpec((None, None, 2, HG_HEADS, HG_DK, HG_DK), lambda n, j: (n, layer, 0, 0, 0, 0)))
        args.append(state0)
    out_specs = [pl.BlockSpec((rows, HG_W), lambda n, j: (n * nsteps + j, 0)),
                 pl.BlockSpec((rows, HG_W), lambda n, j: (n * nsteps + nsteps - 1 - j, 0))]
    out_shape = [jax.ShapeDtypeStruct((nb * seq, HG_W), F32)] * 2
    if emit_state:
        out_specs.append(pl.BlockSpec((None, 2, HG_HEADS, HG_DK, HG_DK), lambda n, j: (n, 0, 0, 0, 0)))
        out_shape.append(jax.ShapeDtypeStruct((nb, 2, HG_HEADS, HG_DK, HG_DK), F32))
    res = pl.pallas_call(
        functools.partial(_hgrn_kernel, nsteps=nsteps, cps=cps, has_init=has_init, emit_state=emit_state),
        grid=(nb, nsteps),
        in_specs=in_specs,
        out_specs=out_specs,
        out_shape=out_shape,
        scratch_shapes=[pltpu.VMEM((2, HG_HEADS, HG_DK, HG_DK), F32)],
        compiler_params=_params(("parallel", "arbitrary")),
        name="hgrn",
    )(*args)
    return (res[0], res[1], res[2]) if emit_state else (res[0], res[1], None)


def _merge_kernel(*refs, final):
    (oa_ref, ow_ref, of_ref, ob_ref, hg_ref, gates_ref, x_ref, gate_ref, hgn_ref, wb_ref, wo_ref,
     shift2_ref, scale2_ref, gate2_ref, g2_ref, wgu_ref, wd_ref) = refs[:17]
    fg_ref = refs[17] if final else None
    o_ref = refs[-1]
    o = of_ref[...] + ob_ref[...]
    hg = hg_ref[...]
    hgn = hgn_ref[...]
    parts = []
    for h in range(HG_HEADS):
        sl = slice(h * HG_DK, (h + 1) * HG_DK)
        g = hg[:, sl]
        parts.append(_rms_rows(o[:, sl], hgn) * (g * _sigmoid(g)))
    o_hg = jnp.concatenate(parts, axis=-1).astype(BF16)
    branch = (oa_ref[...], o_hg, ow_ref[...])
    merged = None
    for k in range(3):
        term = gates_ref[:, k * D_MODEL:(k + 1) * D_MODEL].astype(F32) * _dot(branch[k], wb_ref[k])
        merged = term if merged is None else merged + term
    yv = _dot(merged.astype(BF16), wo_ref[...])
    x_mid = x_ref[...] + gate_ref[...] * yv
    _ffn_block(x_mid, shift2_ref[...], scale2_ref[...], gate2_ref[...], g2_ref[...], wgu_ref, wd_ref, o_ref,
               fg_ref[...] if final else None)


def _merge_call(x, y, gates, o_att, o_win, o_f, o_b, modv, normv, hg_norm_g, w_branch_b, w_out_b,
                w_ffn_in_b, w_ffn_out_b, layer, row_fn, tm, final_g=None):
    n = x.shape[0]
    final = final_g is not None
    row = lambda i: (i, 0)
    in_specs = [
        pl.BlockSpec((tm, BRANCH_W), row),
        pl.BlockSpec((tm, BRANCH_W), row),
        pl.BlockSpec((tm, HG_W), row),
        pl.BlockSpec((tm, HG_W), row),
        pl.BlockSpec((tm, HG_W), lambda i: (i, COL_HG // HG_W)),
        pl.BlockSpec((tm, GATES_W), row),
        pl.BlockSpec((tm, D_MODEL), row),
        _mod_spec(layer, 5, row_fn),
        pl.BlockSpec((1, HG_DK), lambda i: (0, 0)),
        _resident((None, 3, BRANCH_W, D_MODEL), lambda i: (layer, 0, 0, 0)),
        _resident((None, D_MODEL, D_MODEL), lambda i: (layer, 0, 0)),
    ] + _ffn_specs(layer, 1, row_fn)
    args = [o_att, o_win, o_f, o_b, y, gates, x, modv, hg_norm_g[layer].reshape(1, HG_DK).astype(F32),
            w_branch_b, w_out_b, modv, modv, modv, normv, w_ffn_in_b, w_ffn_out_b]
    if final:
        in_specs.append(pl.BlockSpec((1, D_MODEL), lambda i: (0, 0)))
        args.append(final_g.reshape(1, D_MODEL))
    return pl.pallas_call(
        functools.partial(_merge_kernel, final=final),
        grid=(n // tm,),
        in_specs=in_specs,
        out_specs=pl.BlockSpec((tm, D_MODEL), row),
        out_shape=jax.ShapeDtypeStruct((n, D_MODEL), F32),
        compiler_params=_params(("parallel",)),
        name="merge_ffn",
    )(*args)


def _rope_tables(n_lat):
    t = jnp.arange(n_lat, dtype=jnp.int32)
    row = (t // GRID_W).astype(F32)
    col = (t % GRID_W).astype(F32)
    axis_dim = HEAD_DIM // 2
    inv = ROPE_THETA ** (-jnp.arange(0, axis_dim, 2, dtype=F32) / axis_dim)
    ang_r = row[:, None] * inv
    ang_c = col[:, None] * inv
    cos64 = jnp.concatenate([jnp.cos(ang_r), jnp.cos(ang_r), jnp.cos(ang_c), jnp.cos(ang_c)], axis=-1)
    sin64 = jnp.concatenate([-jnp.sin(ang_r), jnp.sin(ang_r), -jnp.sin(ang_c), jnp.sin(ang_c)], axis=-1)
    return jnp.tile(cos64, (1, 2)), jnp.tile(sin64, (1, 2))


def kernel(x_prompt, x_sample, cache_k_attn, cache_v_attn, cache_k_win, cache_v_win, state_hgrn, c, c_ctx,
           w_mod, b_mod, norm_g, w_ffn_in, w_ffn_out, w_in, qk_norm_g, lower_bounds, hg_norm_g, sink_logit,
           w_branch, w_out, final_norm_g):
    batch, seq, _ = x_prompt.shape
    dec_batch, dec_seq, _ = x_sample.shape
    depth = w_mod.shape[0]

    cond = jnp.zeros((COND_ROWS, D_MODEL), F32).at[0].set(c_ctx).at[1:1 + dec_batch].set(c)
    modv = _mod_call(cond, w_mod, b_mod).reshape(depth * COND_ROWS * N_MOD, 1, D_MODEL)
    lb_all = _lb_call(lower_bounds).reshape(depth, 1, 2 * HG_W)
    normv = norm_g.astype(F32).reshape(depth * 3, 1, D_MODEL)
    w_ffn_in_b = w_ffn_in.astype(BF16)
    w_ffn_out_b = w_ffn_out.astype(BF16)
    w_in_b = w_in.astype(BF16)
    w_branch_b = w_branch.astype(BF16)
    w_out_b = w_out.astype(BF16)
    rope = _rope_tables(dec_seq)

    def run(x, nb, s, latent):
        n = nb * s

        def rows_of(tm):
            return (lambda i: 1 + i // (s // tm)) if latent else (lambda i: 0)

        tm = min(DENSE_TM, s if latent else n)
        tm_ffn = min(FFN_TM, s if latent else n)
        x = x.reshape(n, D_MODEL)
        ctx_out = []
        for l in range(depth):
            x = _ffn_call(x, modv, normv, w_ffn_in_b, w_ffn_out_b, l, rows_of(tm_ffn), tm_ffn)
            y, gates, kv_att, kv_win = _inproj_call(x, modv, normv, w_in_b, qk_norm_g, lb_all, l, rows_of(tm), tm,
                                                    rope=rope if latent else None, seq=s)
            if latent:
                o_att = _attn_call(y, kv_att, nb, s, COL_AQ, cache_k=cache_k_attn, cache_v=cache_v_attn,
                                   layer=l, rope=rope)
                o_win = _attn_call(y, kv_win, nb, s, COL_WQ, cache_k=cache_k_win, cache_v=cache_v_win,
                                   layer=l, rope=rope, sink=sink_logit, window=True)
                o_f, o_b, _ = _hgrn_call(y, nb, s, l, state0=state_hgrn)
            else:
                o_att = _attn_call(y, kv_att, nb, s, COL_AQ)
                o_win = _attn_call(y, kv_win, nb, s, COL_WQ, layer=l, sink=sink_logit)
                o_f, o_b, s_fin = _hgrn_call(y, nb, s, l, emit_state=True)
                kv = lambda col: y[:, col:col + LANES].reshape(nb, s, ATT_KV, HEAD_DIM)
                ctx_out.append((kv(COL_AK), kv(COL_AV), kv(COL_WK), kv(COL_WV), s_fin))
            x = _merge_call(x, y, gates, o_att, o_win, o_f, o_b, modv, normv, hg_norm_g, w_branch_b, w_out_b,
                            w_ffn_in_b, w_ffn_out_b, l, rows_of(tm), tm,
                            final_g=final_norm_g.astype(F32) if l == depth - 1 else None)
        return x.reshape(nb, s, D_MODEL), ctx_out

    y_prompt, ctx_out = run(x_prompt, batch, seq, False)
    y_sample, _ = run(x_sample, dec_batch, dec_seq, True)
    stack = lambda k: jnp.stack([cx[k] for cx in ctx_out], axis=1)
    return (y_prompt, y_sample, stack(0), stack(1), stack(2), stack(3), stack(4))
```

```python
import functools

import jax
import jax.numpy as jnp
import numpy as np
from jax import lax
from jax.experimental import pallas as pl
from jax.experimental.pallas import tpu as pltpu

F32 = jnp.float32
BF16 = jnp.bfloat16

D_MODEL = 1024
GRID_W = 64
HEAD_DIM = 64
ATT_KV = 2
ATT_HEADS = 8
GROUP_W = ATT_HEADS // ATT_KV * HEAD_DIM
WINDOW = 128
HG_HEADS = 4
HG_DK = 128
HG_W = HG_HEADS * HG_DK
BRANCH_W = 512
D_FF = 2816
ROPE_THETA = 10000.0
EPS = 1e-6
NEG_INF = -1e30
TINY = 1e-30
LOG2E = 1.4426950408889634
N_MOD = 9
IN_W = 7168

LANES = 128
SUBLANES = 8

COND_ROWS = 16
DENSE_TM = 512
FFN_TM = 1024
FFN_TF = 256
INPROJ_TN = 256
ATT_TQ = 512
ATT_SUB = 128
HG_CHUNK = 128
HG_STEP_CHUNKS = 4
HG_SUB = SUBLANES
PAIR_SUB = 16
VMEM_LIMIT = 56 * 1024 * 1024

COL_AQ, COL_AK, COL_AV = 0, 512, 640
COL_WQ, COL_WK, COL_WV = 768, 1280, 1408
COL_HQ, COL_HF, COL_HI, COL_HG = 1536, 2048, 3072, 3584
COL_GATES = 4096
Y_W = COL_GATES
KV_W = 6 * LANES
GATES_W = IN_W - COL_GATES


def _params(sem):
    return pltpu.CompilerParams(dimension_semantics=sem, vmem_limit_bytes=VMEM_LIMIT)


def _resident(block_shape, index_map):
    return pl.BlockSpec(block_shape, index_map, pipeline_mode=pl.Buffered(1))


def _sigmoid(x):
    return 1.0 / (1.0 + jnp.exp(-x))


def _dot(a, b):
    return jnp.dot(a, b, preferred_element_type=F32)


def _dot_nt(a, b):
    return lax.dot_general(a, b, (((1,), (1,)), ((), ())), preferred_element_type=F32)


def _dot_tn(a, b):
    return lax.dot_general(a, b, (((0,), (0,)), ((), ())), preferred_element_type=F32)


def _rms_rows(x, gain):
    return x * lax.rsqrt(jnp.mean(x * x, axis=-1, keepdims=True) + EPS) * gain


def _mod_kernel(c_ref, w_ref, b_ref, o_ref):
    c = c_ref[...]
    h = (c * _sigmoid(c)).astype(BF16)
    o_ref[...] = _dot(h, w_ref[...].astype(BF16)) + b_ref[...]


def _mod_call(cond, w_mod, b_mod):
    depth, d, nm = w_mod.shape
    tn = nm // 8
    return pl.pallas_call(
        _mod_kernel,
        grid=(depth, nm // tn),
        in_specs=[pl.BlockSpec((COND_ROWS, d), lambda l, j: (0, 0)),
                  pl.BlockSpec((None, d, tn), lambda l, j: (l, 0, j)),
                  pl.BlockSpec((None, 1, tn), lambda l, j: (l, 0, j))],
        out_specs=pl.BlockSpec((None, COND_ROWS, tn), lambda l, j: (l, 0, j)),
        out_shape=jax.ShapeDtypeStruct((depth, COND_ROWS, nm), F32),
        compiler_params=_params(("parallel", "parallel")),
        name="mod",
    )(cond, w_mod, b_mod.reshape(depth, 1, nm))


def _lb_kernel(x_ref, o_ref):
    x = x_ref[...]
    m = jnp.max(x, axis=0, keepdims=True)
    e = jnp.exp(x - m)
    s = e / jnp.sum(e, axis=0, keepdims=True)
    acc = jnp.zeros_like(s[0:1])
    for l in range(x.shape[0]):
        acc = acc + s[l:l + 1]
        o_ref[l:l + 1, :] = acc - s[0:1]


def _lb_call(lower_bounds):
    depth = lower_bounds.shape[0]
    x = lower_bounds.reshape(depth, -1).astype(F32)
    return pl.pallas_call(_lb_kernel, out_shape=jax.ShapeDtypeStruct(x.shape, F32), name="lb")(x)


def _ffn_block(x, shift, scale, gate, g, wgu_ref, wd_ref, o_ref, final_g):
    h = (_rms_rows(x, g) * (1.0 + scale) + shift).astype(BF16)
    for c in range(D_FF // FFN_TF):
        a = _dot(h, wgu_ref[:, c * FFN_TF:(c + 1) * FFN_TF])
        u = _dot(h, wgu_ref[:, D_FF + c * FFN_TF:D_FF + (c + 1) * FFN_TF])
        act = (a * _sigmoid(a) * u).astype(BF16)
        part = _dot(act, wd_ref[c * FFN_TF:(c + 1) * FFN_TF, :])
        if c == 0:
            o_ref[...] = part
        else:
            o_ref[...] += part
    out = x + 0.5 * gate * o_ref[...]
    if final_g is not None:
        out = _rms_rows(out, final_g)
    o_ref[...] = out


def _ffn_kernel(x_ref, shift_ref, scale_ref, gate_ref, g_ref, wgu_ref, wd_ref, o_ref):
    _ffn_block(x_ref[...], shift_ref[...], scale_ref[...], gate_ref[...], g_ref[...], wgu_ref, wd_ref, o_ref, None)


def _mod_spec(layer, k, row_fn):
    return pl.BlockSpec((None, 1, D_MODEL), lambda i: ((layer * COND_ROWS + row_fn(i)) * N_MOD + k, 0, 0))


def _ffn_specs(layer, which, row_fn):
    k0 = 3 * (2 * which)
    return [
        _mod_spec(layer, k0, row_fn), _mod_spec(layer, k0 + 1, row_fn), _mod_spec(layer, k0 + 2, row_fn),
        pl.BlockSpec((None, 1, D_MODEL), lambda i: (layer * 3 + 2 * which, 0, 0)),
        _resident((None, None, D_MODEL, 2 * D_FF), lambda i: (layer, which, 0, 0)),
        _resident((None, None, D_FF, D_MODEL), lambda i: (layer, which, 0, 0)),
    ]


def _ffn_call(x, modv, normv, w_in_b, w_out_b, layer, row_fn, tm):
    n = x.shape[0]
    return pl.pallas_call(
        _ffn_kernel,
        grid=(n // tm,),
        in_specs=[pl.BlockSpec((tm, D_MODEL), lambda i: (i, 0))] + _ffn_specs(layer, 0, row_fn),
        out_specs=pl.BlockSpec((tm, D_MODEL), lambda i: (i, 0)),
        out_shape=jax.ShapeDtypeStruct((n, D_MODEL), F32),
        compiler_params=_params(("parallel",)),
        name="ffn",
    )(x, modv, modv, modv, normv, w_in_b, w_out_b)


def _head_rms(y, gain):
    lane = lax.broadcasted_iota(jnp.int32, (1, LANES), 1)
    lo = lane < HEAD_DIM
    outs = []
    for c in range(y.shape[1] // LANES):
        blk = y[:, c * LANES:(c + 1) * LANES]
        sq = blk * blk
        s_lo = jnp.sum(jnp.where(lo, sq, 0.0), axis=-1, keepdims=True)
        s_hi = jnp.sum(jnp.where(lo, 0.0, sq), axis=-1, keepdims=True)
        ms = jnp.where(lo, s_lo, s_hi) * (1.0 / HEAD_DIM)
        outs.append(blk * lax.rsqrt(ms + EPS) * gain[:, c * LANES:(c + 1) * LANES])
    return outs[0] if len(outs) == 1 else jnp.concatenate(outs, axis=-1)


def _rope(x, cos, sin_signed):
    lane = lax.broadcasted_iota(jnp.int32, (1, LANES), 1)
    first = (lane & 31) < 16
    swapped = jnp.where(first, pltpu.roll(x, LANES - 16, 1), pltpu.roll(x, 16, 1))
    return x * cos + swapped * sin_signed


def _kv_operands(k, v):
    lane = lax.broadcasted_iota(jnp.int32, (1, LANES), 1)
    lo = lane < HEAD_DIM
    one_lo = jnp.where(lane == 0, 1.0, 0.0)
    one_hi = jnp.where(lane == HEAD_DIM, 1.0, 0.0)
    kr = pltpu.roll(k, HEAD_DIM, 1)
    vr = pltpu.roll(v, HEAD_DIM, 1)
    parts = [jnp.where(lo, k, kr), jnp.where(lo, kr, k),
             jnp.where(lo, v, one_hi), jnp.where(lo, one_lo, vr), jnp.where(lo, vr, one_hi), jnp.where(lo, one_lo, v)]
    return jnp.concatenate(parts, axis=-1).astype(BF16)


def _inproj_kernel(*refs, use_rope):
    x_ref, shift_ref, scale_ref, g_ref, w_ref, qg_ref, kg_ref, lb_ref = refs[:8]
    cos_ref, sin_ref = (refs[8], refs[9]) if use_rope else (None, None)
    y_ref, gates_ref, kva_ref, kvw_ref = refs[-4:]
    h = (_rms_rows(x_ref[...], g_ref[...]) * (1.0 + scale_ref[...]) + shift_ref[...]).astype(BF16)

    def proj(c0, c1):
        return _dot(h, w_ref[:, c0:c1])

    tn = INPROJ_TN
    for c in range(COL_GATES, IN_W, tn):
        gates_ref[:, c - COL_GATES:c - COL_GATES + tn] = _sigmoid(proj(c, c + tn)).astype(BF16)
    for c in range(COL_HQ, COL_HF, tn):
        t = proj(c, c + tn)
        y_ref[:, c:c + tn] = t * _sigmoid(t)
    for c in range(COL_HF, COL_HI, tn):
        lb = lb_ref[:, c - COL_HF:c - COL_HF + tn]
        y_ref[:, c:c + tn] = lb + (1.0 - lb) * _sigmoid(proj(c, c + tn))
    for c in range(COL_AQ, COL_AK, tn):
        y_ref[:, c:c + tn] = _head_rms(proj(c, c + tn), qg_ref[:, c:c + tn])
    for ck, out_ref in ((COL_AK, kva_ref), (COL_WK, kvw_ref)):
        t = proj(ck, ck + 2 * LANES)
        k = t[:, :LANES]
        if ck == COL_AK:
            k = _head_rms(k, kg_ref[...])
        y_ref[:, ck:ck + LANES] = k
        y_ref[:, ck + LANES:ck + 2 * LANES] = t[:, LANES:]
        if use_rope:
            k = _rope(k, cos_ref[...], sin_ref[...])
        out_ref[...] = _kv_operands(k, t[:, LANES:])
    for c in list(range(COL_WQ, COL_WK, tn)) + list(range(COL_HI, COL_GATES, tn)):
        y_ref[:, c:c + tn] = proj(c, c + tn)


def _inproj_call(x, modv, normv, w_in_b, qk_norm_g, lb_all, layer, row_fn, tm, rope=None, seq=None):
    n = x.shape[0]
    use_rope = rope is not None
    qg = jnp.tile(qk_norm_g[layer, 0].astype(F32), (COL_AK - COL_AQ) // HEAD_DIM).reshape(1, COL_AK - COL_AQ)
    kg = jnp.tile(qk_norm_g[layer, 1].astype(F32), LANES // HEAD_DIM).reshape(1, LANES)
    in_specs = [
        pl.BlockSpec((tm, D_MODEL), lambda i: (i, 0)),
        _mod_spec(layer, 3, row_fn), _mod_spec(layer, 4, row_fn),
        pl.BlockSpec((None, 1, D_MODEL), lambda i: (layer * 3 + 1, 0, 0)),
        _resident((None, D_MODEL, IN_W), lambda i: (layer, 0, 0)),
        pl.BlockSpec((1, COL_AK - COL_AQ), lambda i: (0, 0)),
        pl.BlockSpec((1, LANES), lambda i: (0, 0)),
        pl.BlockSpec((None, 1, 2 * HG_W), lambda i: (layer, 0, 0)),
    ]
    args = [x, modv, modv, normv, w_in_b, qg, kg, lb_all]
    if use_rope:
        in_specs += [pl.BlockSpec((tm, LANES), lambda i: (i % (seq // tm), 0))] * 2
        args += list(rope)
    widths = (Y_W, GATES_W, KV_W, KV_W)
    dtypes = (F32, BF16, BF16, BF16)
    return pl.pallas_call(
        functools.partial(_inproj_kernel, use_rope=use_rope),
        grid=(n // tm,),
        in_specs=in_specs,
        out_specs=[pl.BlockSpec((tm, w), lambda i: (i, 0)) for w in widths],
        out_shape=[jax.ShapeDtypeStruct((n, w), dt) for w, dt in zip(widths, dtypes)],
        compiler_params=_params(("parallel",)),
        name="inproj",
    )(*args)


def _attn_kernel(*refs, tq, ts, seq, past, use_rope, use_sink, window, sink_off):
    it = iter(refs)
    sink_ref = next(it) if use_sink else None
    q_refs = (next(it), next(it))
    kv_ref = next(it)
    ck_ref = cv_ref = cos_ref = sin_ref = None
    if past:
        ck_ref, cv_ref = next(it), next(it)
    if use_rope:
        cos_ref, sin_ref = next(it), next(it)
    o_ref = next(it)
    ctx_scr = next(it) if past else None

    qi = pl.program_id(1)
    lane = lax.broadcasted_iota(jnp.int32, (1, LANES), 1)
    lo = lane < HEAD_DIM

    if past:
        @pl.when(qi == 0)
        def _build():
            ctx_scr[...] = _kv_operands(ck_ref[...], cv_ref[...])

    q0 = pl.multiple_of(qi * tq, tq)
    q_scale = (HEAD_DIM ** -0.5) * LOG2E

    def unit_segments(h):
        ctx = [(ctx_scr, slice(0, past), None)] if past else []
        if not window:
            return ctx + [(kv_ref, slice(0, seq), None)]
        span = ts + 2 * WINDOW
        t0 = q0 + h * ts
        start = pl.multiple_of(jnp.clip(t0 - WINDOW, 0, seq - span), WINDOW)
        t_pos = t0 + lax.broadcasted_iota(jnp.int32, (ts, 1), 0)
        s_pos = start + lax.broadcasted_iota(jnp.int32, (1, span), 1)
        band_ok = jnp.abs(t_pos - s_pos) <= WINDOW
        return ctx + [(kv_ref, pl.ds(start, span), band_ok)]

    def keys(g, seg):
        ref, rsl, _ = seg
        return ref[rsl, g * LANES:(g + 1) * LANES]

    def values(g, par, seg):
        ref, rsl, _ = seg
        c0 = (2 + 2 * g + par) * LANES
        return ref[rsl, c0:c0 + LANES]

    def score_phase(g, h, segments):
        xs = []
        for p in range(2):
            q = q_refs[g][h * ts:(h + 1) * ts, p * LANES:(p + 1) * LANES]
            if use_rope:
                tsl = pl.ds(q0 + h * ts, ts)
                q = _rope(q, cos_ref[tsl, :], sin_ref[tsl, :])
            xs.append(q * q_scale)
        q4 = jnp.concatenate([jnp.where(lo, xs[0], 0.0), jnp.where(lo, xs[1], 0.0),
                              jnp.where(lo, 0.0, xs[0]), jnp.where(lo, 0.0, xs[1])], axis=0).astype(BF16)
        return [_dot_nt(q4, keys(g, seg)) for seg in segments]

    def softmax_phase(g, scores, segments):
        probs = [[] for _ in scores]
        extra = []
        for blk, head in enumerate((0, 2, 1, 3)):
            rsl = slice(blk * ts, (blk + 1) * ts)
            rows = [s[rsl] if seg[2] is None else jnp.where(seg[2], s[rsl], NEG_INF) for s, seg in zip(scores, segments)]
            m = None
            for s in rows:
                ms = jnp.max(s, axis=-1, keepdims=True)
                m = ms if m is None else jnp.maximum(m, ms)
            if use_sink:
                sk = sink_ref[sink_off + g * (ATT_HEADS // ATT_KV) + head] * LOG2E
                m = jnp.maximum(m, sk)
                extra.append(jnp.exp2(sk - m))
            for i, s in enumerate(rows):
                probs[i].append(jnp.exp2(s - m).astype(BF16))
        return probs, extra

    def value_phase(g, probs, segments):
        acc = [None, None]
        for i, seg in enumerate(segments):
            for par in range(2):
                e = jnp.concatenate(probs[i][2 * par:2 * par + 2], axis=0)
                pv = _dot(e, values(g, par, seg))
                acc[par] = pv if acc[par] is None else acc[par] + pv
        return acc

    def output_phase(g, h, acc, extra):
        outs = []
        for blk in range(4):
            par = blk // 2
            a = acc[par][(blk % 2) * ts:(blk % 2 + 1) * ts]
            den = jnp.sum(jnp.where(lane == (HEAD_DIM if par == 0 else 0), a, 0.0), axis=-1, keepdims=True)
            if use_sink:
                den = den + extra[blk]
            outs.append(a * (1.0 / den))
        for p in range(2):
            o_pair = jnp.where(lo, outs[p], outs[2 + p])
            c0 = g * GROUP_W + p * LANES
            o_ref[h * ts:(h + 1) * ts, c0:c0 + LANES] = o_pair.astype(o_ref.dtype)

    units = [(g, h) for h in range(tq // ts) for g in range(ATT_KV)]
    segs = {h: unit_segments(h) for h in range(tq // ts)}
    scores = [score_phase(g, h, segs[h]) for g, h in units]
    accs, extras = [], []
    for (g, h), sc in zip(units, scores):
        probs, extra = softmax_phase(g, sc, segs[h])
        accs.append(value_phase(g, probs, segs[h]))
        extras.append(extra)
    for (g, h), acc, extra in zip(units, accs, extras):
        output_phase(g, h, acc, extra)


def _attn_call(y, kv, nb, seq, qcol, *, cache_k=None, cache_v=None, layer=0, rope=None, sink=None, window=False):
    tq = min(ATT_TQ, seq)
    ts = min(ATT_SUB, tq)
    nq = seq // tq
    past = 0 if cache_k is None else cache_k.shape[2]
    use_rope = rope is not None
    use_sink = sink is not None
    in_specs, args = [], []
    if use_sink:
        in_specs.append(pl.BlockSpec(memory_space=pltpu.SMEM))
        args.append(sink.reshape(-1).astype(F32))
    for g in range(ATT_KV):
        in_specs.append(pl.BlockSpec((tq, GROUP_W), lambda b, qi, g=g: (b * nq + qi, qcol // GROUP_W + g)))
        args.append(y)
    in_specs.append(pl.BlockSpec((seq, KV_W), lambda b, qi: (b, 0)))
    args.append(kv)
    if past:
        cshape = cache_k.shape[:3] + (LANES,)
        in_specs += [pl.BlockSpec((None, None, past, LANES), lambda b, qi: (b, layer, 0, 0))] * 2
        args += [cache_k.reshape(cshape), cache_v.reshape(cshape)]
    if use_rope:
        in_specs += [pl.BlockSpec((seq, LANES), lambda b, qi: (0, 0))] * 2
        args += list(rope)
    kern = functools.partial(_attn_kernel, tq=tq, ts=ts, seq=seq, past=past, use_rope=use_rope,
                             use_sink=use_sink, window=window, sink_off=layer * ATT_HEADS)
    return pl.pallas_call(
        kern,
        grid=(nb, nq),
        in_specs=in_specs,
        out_specs=pl.BlockSpec((tq, BRANCH_W), lambda b, qi: (b * nq + qi, 0)),
        out_shape=jax.ShapeDtypeStruct((nb * seq, BRANCH_W), BF16),
        scratch_shapes=[pltpu.VMEM((past, KV_W), BF16)] if past else [],
        compiler_params=_params(("parallel", "arbitrary")),
        name="win" if window or use_sink else "att",
    )(*args)


def _pair_table(fwd):
    C = HG_CHUNK
    t = np.arange(C)[:, None]
    s = np.arange(C)[None, :]
    seen = (s <= t) if fwd else (s >= t)
    tab = np.full((C, C), -1, np.int32)
    hh, idx = C // 2, (C // HG_SUB).bit_length() - 2
    while hh >= HG_SUB:
        tab = np.where((t // (2 * hh) == s // (2 * hh)) & seen, idx, tab)
        hh, idx = hh // 2, idx - 1
    tab = np.where((t // HG_SUB == s // HG_SUB), np.where(seen, PAIR_SUB + s % HG_SUB, -1), tab)
    return tab.astype(np.int32)


def _hgrn_unit(q, k, b, v, st, sub_masks, level_masks, fwd):
    C = HG_CHUNK
    G = C // HG_SUB
    vb = v.astype(BF16)

    b3 = b.reshape(G, HG_SUB, HG_DK)
    q3 = q.reshape(G, HG_SUB, HG_DK)
    c3 = (jnp.log2(k) - b).reshape(G, HG_SUB, HG_DK)
    a3 = jnp.zeros((G, HG_SUB, C), F32)
    for jj in range(HG_SUB):
        kdec = jnp.exp2(b3 + c3[:, jj:jj + 1, :])
        score = jnp.sum(q3 * kdec, axis=-1, keepdims=True)
        a3 = jnp.where(sub_masks[jj], score, a3)
    a_mat = a3.reshape(C, C)

    tcol = lax.broadcasted_iota(jnp.int32, (C, 1), 0)
    hh, idx = HG_SUB, 0
    while hh < C:
        grp = 2 * hh
        later = (tcol & (grp - 1)) >= hh
        is_q = later if fwd else jnp.logical_not(later)
        bg = b.reshape(C // grp, grp, HG_DK)
        bnd = bg[:, hh - 1:hh, :] if fwd else bg[:, hh:hh + 1, :]
        bnd = jnp.broadcast_to(bnd, (C // grp, grp, HG_DK)).reshape(C, HG_DK)
        w = (jnp.where(is_q, q, k) * jnp.exp2(jnp.where(is_q, b - bnd, bnd - b))).astype(BF16)
        a_mat = jnp.where(level_masks[idx], _dot_nt(w, w), a_mat)
        hh, idx = grp, idx + 1
    out = _dot(a_mat.astype(BF16), vb)

    out = out + _dot_nt((q * jnp.exp2(b)).astype(BF16), st.astype(BF16))
    blast = b[C - 1:C, :] if fwd else b[0:1, :]
    kh = (k * jnp.exp2(blast - b)).astype(BF16)
    st_new = st * jnp.exp2(blast) + _dot_tn(vb, kh)
    return out, st_new


def _hgrn_kernel(*refs, nsteps, cps, has_init, emit_state):
    it = iter(refs)
    io = [(next(it), next(it), next(it)) for _ in range(2)]
    pair_ref = next(it)
    s0_ref = next(it) if has_init else None
    o_refs = (next(it), next(it))
    sfin_ref = next(it) if emit_state else None
    st_scr = next(it)

    j = pl.program_id(1)
    C = HG_CHUNK

    @pl.when(j == 0)
    def _():
        for dirn in range(2):
            for h in range(HG_HEADS):
                if has_init:
                    st_scr[dirn, h] = s0_ref[dirn, h].T
                else:
                    st_scr[dirn, h] = jnp.zeros((HG_DK, HG_DK), F32)

    ti = lax.broadcasted_iota(jnp.int32, (C, C), 0)
    si = lax.broadcasted_iota(jnp.int32, (C, C), 1)
    tris, masks = [], []
    for dirn in range(2):
        tris.append(jnp.where((ti >= si) if dirn == 0 else (ti <= si), 1.0, 0.0).astype(BF16))
        pair = pair_ref[dirn]
        pair3 = pair.reshape(C // HG_SUB, HG_SUB, C)
        masks.append(([pair3 == PAIR_SUB + jj for jj in range(HG_SUB)],
                      [pair == idx for idx in range((C // HG_SUB).bit_length() - 1)]))
    states = [[st_scr[dirn, h] for h in range(HG_HEADS)] for dirn in range(2)]
    for cc in range(cps):
        for dirn in range(2):
            fwd = dirn == 0
            q_ref, f_ref, v_ref = io[dirn]
            r0 = (cc if fwd else cps - 1 - cc) * C
            rows = slice(r0, r0 + C)
            f = f_ref[rows, :]
            l2 = jnp.log2(jnp.maximum(f, TINY))
            l_hi = l2.astype(BF16)
            r1 = l2 - l_hi.astype(F32)
            l_mid = r1.astype(BF16)
            l_lo = (r1 - l_mid.astype(F32)).astype(BF16)
            b_all = _dot(tris[dirn], l_hi) + _dot(tris[dirn], l_mid) + _dot(tris[dirn], l_lo)
            for h in range(HG_HEADS):
                sl = slice(h * HG_DK, (h + 1) * HG_DK)
                out, st_new = _hgrn_unit(q_ref[rows, sl], jnp.maximum(1.0 - f[:, sl], 0.0), b_all[:, sl],
                                         v_ref[rows, sl], states[dirn][h], masks[dirn][0], masks[dirn][1], fwd)
                o_refs[dirn][rows, sl] = out
                states[dirn][h] = st_new
    for dirn in range(2):
        for h in range(HG_HEADS):
            st_scr[dirn, h] = states[dirn][h]
            if emit_state:
                @pl.when(j == nsteps - 1)
                def _():
                    sfin_ref[dirn, h] = states[dirn][h].T


def _hgrn_call(y, nb, seq, layer, state0=None, emit_state=False):
    cps = min(HG_STEP_CHUNKS, seq // HG_CHUNK)
    rows = cps * HG_CHUNK
    nsteps = seq // rows
    has_init = state0 is not None
    in_specs, args = [], []
    for dirn in range(2):
        rowblk = (lambda n, j: n * nsteps + j) if dirn == 0 else (lambda n, j: n * nsteps + nsteps - 1 - j)
        for col in (COL_HQ, COL_HF + dirn * HG_W, COL_HI):
            in_specs.append(pl.BlockSpec((rows, HG_W), lambda n, j, rowblk=rowblk, col=col: (rowblk(n, j), col // HG_W)))
            args.append(y)
    in_specs.append(pl.BlockSpec((2, HG_CHUNK, HG_CHUNK), lambda n, j: (0, 0, 0)))
    args.append(jnp.asarray([_pair_table(True), _pair_table(False)]))
    if has_init:
        in_specs.append(pl.BlockSpec((None, None, 2, HG_HEADS, HG_DK, HG_DK), lambda n, j: (n, layer, 0, 0, 0, 0)))
        args.append(state0)
    out_specs = [pl.BlockSpec((rows, HG_W), lambda n, j: (n * nsteps + j, 0)),
                 pl.BlockSpec((rows, HG_W), lambda n, j: (n * nsteps + nsteps - 1 - j, 0))]
    out_shape = [jax.ShapeDtypeStruct((nb * seq, HG_W), F32)] * 2
    if emit_state:
        out_specs.append(pl.BlockSpec((None, 2, HG_HEADS, HG_DK, HG_DK), lambda n, j: (n, 0, 0, 0, 0)))
        out_shape.append(jax.ShapeDtypeStruct((nb, 2, HG_HEADS, HG_DK, HG_DK), F32))
    res = pl.pallas_call(
        functools.partial(_hgrn_kernel, nsteps=nsteps, cps=cps, has_init=has_init, emit_state=emit_state),
        grid=(nb, nsteps),
        in_specs=in_specs,
        out_specs=out_specs,
        out_shape=out_shape,
        scratch_shapes=[pltpu.VMEM((2, HG_HEADS, HG_DK, HG_DK), F32)],
        compiler_params=_params(("parallel", "arbitrary")),
        name="hgrn",
    )(*args)
    return (res[0], res[1], res[2]) if emit_state else (res[0], res[1], None)


def _merge_kernel(*refs, final):
    (oa_ref, ow_ref, of_ref, ob_ref, hg_ref, gates_ref, x_ref, gate_ref, hgn_ref, wb_ref, wo_ref,
     shift2_ref, scale2_ref, gate2_ref, g2_ref, wgu_ref, wd_ref) = refs[:17]
    fg_ref = refs[17] if final else None
    o_ref = refs[-1]
    o = of_ref[...] + ob_ref[...]
    hg = hg_ref[...]
    hgn = hgn_ref[...]
    parts = []
    for h in range(HG_HEADS):
        sl = slice(h * HG_DK, (h + 1) * HG_DK)
        g = hg[:, sl]
        parts.append(_rms_rows(o[:, sl], hgn) * (g * _sigmoid(g)))
    o_hg = jnp.concatenate(parts, axis=-1).astype(BF16)
    branch = (oa_ref[...], o_hg, ow_ref[...])
    merged = None
    for k in range(3):
        term = gates_ref[:, k * D_MODEL:(k + 1) * D_MODEL].astype(F32) * _dot(branch[k], wb_ref[k])
        merged = term if merged is None else merged + term
    yv = _dot(merged.astype(BF16), wo_ref[...])
    x_mid = x_ref[...] + gate_ref[...] * yv
    _ffn_block(x_mid, shift2_ref[...], scale2_ref[...], gate2_ref[...], g2_ref[...], wgu_ref, wd_ref, o_ref,
               fg_ref[...] if final else None)


def _merge_call(x, y, gates, o_att, o_win, o_f, o_b, modv, normv, hg_norm_g, w_branch_b, w_out_b,
                w_ffn_in_b, w_ffn_out_b, layer, row_fn, tm, final_g=None):
    n = x.shape[0]
    final = final_g is not None
    row = lambda i: (i, 0)
    in_specs = [
        pl.BlockSpec((tm, BRANCH_W), row),
        pl.BlockSpec((tm, BRANCH_W), row),
        pl.BlockSpec((tm, HG_W), row),
        pl.BlockSpec((tm, HG_W), row),
        pl.BlockSpec((tm, HG_W), lambda i: (i, COL_HG // HG_W)),
        pl.BlockSpec((tm, GATES_W), row),
        pl.BlockSpec((tm, D_MODEL), row),
        _mod_spec(layer, 5, row_fn),
        pl.BlockSpec((1, HG_DK), lambda i: (0, 0)),
        _resident((None, 3, BRANCH_W, D_MODEL), lambda i: (layer, 0, 0, 0)),
        _resident((None, D_MODEL, D_MODEL), lambda i: (layer, 0, 0)),
    ] + _ffn_specs(layer, 1, row_fn)
    args = [o_att, o_win, o_f, o_b, y, gates, x, modv, hg_norm_g[layer].reshape(1, HG_DK).astype(F32),
            w_branch_b, w_out_b, modv, modv, modv, normv, w_ffn_in_b, w_ffn_out_b]
    if final:
        in_specs.append(pl.BlockSpec((1, D_MODEL), lambda i: (0, 0)))
        args.append(final_g.reshape(1, D_MODEL))
    return pl.pallas_call(
        functools.partial(_merge_kernel, final=final),
        grid=(n // tm,),
        in_specs=in_specs,
        out_specs=pl.BlockSpec((tm, D_MODEL), row),
        out_shape=jax.ShapeDtypeStruct((n, D_MODEL), F32),
        compiler_params=_params(("parallel",)),
        name="merge_ffn",
    )(*args)


def _rope_tables(n_lat):
    t = jnp.arange(n_lat, dtype=jnp.int32)
    row = (t // GRID_W).astype(F32)
    col = (t % GRID_W).astype(F32)
    axis_dim = HEAD_DIM // 2
    inv = ROPE_THETA ** (-jnp.arange(0, axis_dim, 2, dtype=F32) / axis_dim)
    ang_r = row[:, None] * inv
    ang_c = col[:, None] * inv
    cos64 = jnp.concatenate([jnp.cos(ang_r), jnp.cos(ang_r), jnp.cos(ang_c), jnp.cos(ang_c)], axis=-1)
    sin64 = jnp.concatenate([-jnp.sin(ang_r), jnp.sin(ang_r), -jnp.sin(ang_c), jnp.sin(ang_c)], axis=-1)
    return jnp.tile(cos64, (1, 2)), jnp.tile(sin64, (1, 2))


def kernel(x_prompt, x_sample, cache_k_attn, cache_v_attn, cache_k_win, cache_v_win, state_hgrn, c, c_ctx,
           w_mod, b_mod, norm_g, w_ffn_in, w_ffn_out, w_in, qk_norm_g, lower_bounds, hg_norm_g, sink_logit,
           w_branch, w_out, final_norm_g):
    batch, seq, _ = x_prompt.shape
    dec_batch, dec_seq, _ = x_sample.shape
    depth = w_mod.shape[0]

    cond = jnp.zeros((COND_ROWS, D_MODEL), F32).at[0].set(c_ctx).at[1:1 + dec_batch].set(c)
    modv = _mod_call(cond, w_mod, b_mod).reshape(depth * COND_ROWS * N_MOD, 1, D_MODEL)
    lb_all = _lb_call(lower_bounds).reshape(depth, 1, 2 * HG_W)
    normv = norm_g.astype(F32).reshape(depth * 3, 1, D_MODEL)
    w_ffn_in_b = w_ffn_in.astype(BF16)
    w_ffn_out_b = w_ffn_out.astype(BF16)
    w_in_b = w_in.astype(BF16)
    w_branch_b = w_branch.astype(BF16)
    w_out_b = w_out.astype(BF16)
    rope = _rope_tables(dec_seq)

    def run(x, nb, s, latent):
        n = nb * s

        def rows_of(tm):
            return (lambda i: 1 + i // (s // tm)) if latent else (lambda i: 0)

        tm = min(DENSE_TM, s if latent else n)
        tm_ffn = min(FFN_TM, s if latent else n)
        x = x.reshape(n, D_MODEL)
        ctx_out = []
        for l in range(depth):
            x = _ffn_call(x, modv, normv, w_ffn_in_b, w_ffn_out_b, l, rows_of(tm_ffn), tm_ffn)
            y, gates, kv_att, kv_win = _inproj_call(x, modv, normv, w_in_b, qk_norm_g, lb_all, l, rows_of(tm), tm,
                                                    rope=rope if latent else None, seq=s)
            if latent:
                o_att = _attn_call(y, kv_att, nb, s, COL_AQ, cache_k=cache_k_attn, cache_v=cache_v_attn,
                                   layer=l, rope=rope)
                o_win = _attn_call(y, kv_win, nb, s, COL_WQ, cache_k=cache_k_win, cache_v=cache_v_win,
                                   layer=l, rope=rope, sink=sink_logit, window=True)
                o_f, o_b, _ = _hgrn_call(y, nb, s, l, state0=state_hgrn)
            else:
                o_att = _attn_call(y, kv_att, nb, s, COL_AQ)
                o_win = _attn_call(y, kv_win, nb, s, COL_WQ, layer=l, sink=sink_logit)
                o_f, o_b, s_fin = _hgrn_call(y, nb, s, l, emit_state=True)
                kv = lambda col: y[:, col:col + LANES].reshape(nb, s, ATT_KV, HEAD_DIM)
                ctx_out.append((kv(COL_AK), kv(COL_AV), kv(COL_WK), kv(COL_WV), s_fin))
            x = _merge_call(x, y, gates, o_att, o_win, o_f, o_b, modv, normv, hg_norm_g, w_branch_b, w_out_b,
                            w_ffn_in_b, w_ffn_out_b, l, rows_of(tm), tm,
                            final_g=final_norm_g.astype(F32) if l == depth - 1 else None)
        return x.reshape(nb, s, D_MODEL), ctx_out

    y_prompt, ctx_out = run(x_prompt, batch, seq, False)
    y_sample, _ = run(x_sample, dec_batch, dec_seq, True)
    stack = lambda k: jnp.stack([cx[k] for cx in ctx_out], axis=1)
    return (y_prompt, y_sample, stack(0), stack(1), stack(2), stack(3), stack(4))
```

```python
import functools

import jax
import jax.numpy as jnp
import numpy as np
from jax import lax
from jax.experimental import pallas as pl
from jax.experimental.pallas import tpu as pltpu

F32 = jnp.float32
BF16 = jnp.bfloat16

D_MODEL = 1024
GRID_W = 64
HEAD_DIM = 64
ATT_KV = 2
ATT_HEADS = 8
GROUP_W = ATT_HEADS // ATT_KV * HEAD_DIM
WINDOW = 128
HG_HEADS = 4
HG_DK = 128
HG_W = HG_HEADS * HG_DK
BRANCH_W = 512
D_FF = 2816
ROPE_THETA = 10000.0
EPS = 1e-6
NEG_INF = -1e30
TINY = 1e-30
LOG2E = 1.4426950408889634
N_MOD = 9
IN_W = 7168

LANES = 128
SUBLANES = 8

COND_ROWS = 16
DENSE_TM = 512
FFN_TM = 1024
FFN_TF = 256
INPROJ_TN = 256
ATT_TQ = 512
ATT_SUB = 128
HG_CHUNK = 128
HG_STEP_CHUNKS = 8
HG_SUB = SUBLANES
PAIR_SUB = 16
VMEM_LIMIT = 56 * 1024 * 1024

COL_AQ, COL_AK, COL_AV = 0, 512, 640
COL_WQ, COL_WK, COL_WV = 768, 1280, 1408
COL_HQ, COL_HF, COL_HI, COL_HG = 1536, 2048, 3072, 3584
COL_GATES = 4096
Y_W = COL_GATES
KV_W = 6 * LANES
GATES_W = IN_W - COL_GATES


def _params(sem):
    return pltpu.CompilerParams(dimension_semantics=sem, vmem_limit_bytes=VMEM_LIMIT)


def _resident(block_shape, index_map):
    return pl.BlockSpec(block_shape, index_map, pipeline_mode=pl.Buffered(1))


def _sigmoid(x):
    return 1.0 / (1.0 + jnp.exp(-x))


def _dot(a, b):
    return jnp.dot(a, b, preferred_element_type=F32)


def _dot_nt(a, b):
    return lax.dot_general(a, b, (((1,), (1,)), ((), ())), preferred_element_type=F32)


def _dot_tn(a, b):
    return lax.dot_general(a, b, (((0,), (0,)), ((), ())), preferred_element_type=F32)


def _rms_rows(x, gain):
    return x * lax.rsqrt(jnp.mean(x * x, axis=-1, keepdims=True) + EPS) * gain


def _mod_kernel(c_ref, w_ref, b_ref, o_ref):
    c = c_ref[...]
    h = (c * _sigmoid(c)).astype(BF16)
    o_ref[...] = _dot(h, w_ref[...].astype(BF16)) + b_ref[...]


def _mod_call(cond, w_mod, b_mod):
    depth, d, nm = w_mod.shape
    tn = nm // 8
    return pl.pallas_call(
        _mod_kernel,
        grid=(depth, nm // tn),
        in_specs=[pl.BlockSpec((COND_ROWS, d), lambda l, j: (0, 0)),
                  pl.BlockSpec((None, d, tn), lambda l, j: (l, 0, j)),
                  pl.BlockSpec((None, 1, tn), lambda l, j: (l, 0, j))],
        out_specs=pl.BlockSpec((None, COND_ROWS, tn), lambda l, j: (l, 0, j)),
        out_shape=jax.ShapeDtypeStruct((depth, COND_ROWS, nm), F32),
        compiler_params=_params(("parallel", "parallel")),
        name="mod",
    )(cond, w_mod, b_mod.reshape(depth, 1, nm))


def _lb_kernel(x_ref, o_ref):
    x = x_ref[...]
    m = jnp.max(x, axis=0, keepdims=True)
    e = jnp.exp(x - m)
    s = e / jnp.sum(e, axis=0, keepdims=True)
    acc = jnp.zeros_like(s[0:1])
    for l in range(x.shape[0]):
        acc = acc + s[l:l + 1]
        o_ref[l:l + 1, :] = acc - s[0:1]


def _lb_call(lower_bounds):
    depth = lower_bounds.shape[0]
    x = lower_bounds.reshape(depth, -1).astype(F32)
    return pl.pallas_call(_lb_kernel, out_shape=jax.ShapeDtypeStruct(x.shape, F32), name="lb")(x)


def _ffn_block(x, shift, scale, gate, g, wgu_ref, wd_ref, o_ref, final_g):
    h = (_rms_rows(x, g) * (1.0 + scale) + shift).astype(BF16)
    for c in range(D_FF // FFN_TF):
        a = _dot(h, wgu_ref[:, c * FFN_TF:(c + 1) * FFN_TF])
        u = _dot(h, wgu_ref[:, D_FF + c * FFN_TF:D_FF + (c + 1) * FFN_TF])
        act = (a * _sigmoid(a) * u).astype(BF16)
        part = _dot(act, wd_ref[c * FFN_TF:(c + 1) * FFN_TF, :])
        if c == 0:
            o_ref[...] = part
        else:
            o_ref[...] += part
    out = x + 0.5 * gate * o_ref[...]
    if final_g is not None:
        out = _rms_rows(out, final_g)
    o_ref[...] = out


def _ffn_kernel(x_ref, shift_ref, scale_ref, gate_ref, g_ref, wgu_ref, wd_ref, o_ref):
    _ffn_block(x_ref[...], shift_ref[...], scale_ref[...], gate_ref[...], g_ref[...], wgu_ref, wd_ref, o_ref, None)


def _mod_spec(layer, k, row_fn):
    return pl.BlockSpec((None, 1, D_MODEL), lambda i: ((layer * COND_ROWS + row_fn(i)) * N_MOD + k, 0, 0))


def _ffn_specs(layer, which, row_fn):
    k0 = 3 * (2 * which)
    return [
        _mod_spec(layer, k0, row_fn), _mod_spec(layer, k0 + 1, row_fn), _mod_spec(layer, k0 + 2, row_fn),
        pl.BlockSpec((None, 1, D_MODEL), lambda i: (layer * 3 + 2 * which, 0, 0)),
        _resident((None, None, D_MODEL, 2 * D_FF), lambda i: (layer, which, 0, 0)),
        _resident((None, None, D_FF, D_MODEL), lambda i: (layer, which, 0, 0)),
    ]


def _ffn_call(x, modv, normv, w_in_b, w_out_b, layer, row_fn, tm):
    n = x.shape[0]
    return pl.pallas_call(
        _ffn_kernel,
        grid=(n // tm,),
        in_specs=[pl.BlockSpec((tm, D_MODEL), lambda i: (i, 0))] + _ffn_specs(layer, 0, row_fn),
        out_specs=pl.BlockSpec((tm, D_MODEL), lambda i: (i, 0)),
        out_shape=jax.ShapeDtypeStruct((n, D_MODEL), F32),
        compiler_params=_params(("parallel",)),
        name="ffn",
    )(x, modv, modv, modv, normv, w_in_b, w_out_b)


def _head_rms(y, gain):
    lane = lax.broadcasted_iota(jnp.int32, (1, LANES), 1)
    lo = lane < HEAD_DIM
    outs = []
    for c in range(y.shape[1] // LANES):
        blk = y[:, c * LANES:(c + 1) * LANES]
        sq = blk * blk
        s_lo = jnp.sum(jnp.where(lo, sq, 0.0), axis=-1, keepdims=True)
        s_hi = jnp.sum(jnp.where(lo, 0.0, sq), axis=-1, keepdims=True)
        ms = jnp.where(lo, s_lo, s_hi) * (1.0 / HEAD_DIM)
        outs.append(blk * lax.rsqrt(ms + EPS) * gain[:, c * LANES:(c + 1) * LANES])
    return outs[0] if len(outs) == 1 else jnp.concatenate(outs, axis=-1)


def _rope(x, cos, sin_signed):
    lane = lax.broadcasted_iota(jnp.int32, (1, LANES), 1)
    first = (lane & 31) < 16
    swapped = jnp.where(first, pltpu.roll(x, LANES - 16, 1), pltpu.roll(x, 16, 1))
    return x * cos + swapped * sin_signed


def _kv_operands(k, v):
    lane = lax.broadcasted_iota(jnp.int32, (1, LANES), 1)
    lo = lane < HEAD_DIM
    one_lo = jnp.where(lane == 0, 1.0, 0.0)
    one_hi = jnp.where(lane == HEAD_DIM, 1.0, 0.0)
    kr = pltpu.roll(k, HEAD_DIM, 1)
    vr = pltpu.roll(v, HEAD_DIM, 1)
    parts = [jnp.where(lo, k, kr), jnp.where(lo, kr, k),
             jnp.where(lo, v, one_hi), jnp.where(lo, one_lo, vr), jnp.where(lo, vr, one_hi), jnp.where(lo, one_lo, v)]
    return jnp.concatenate(parts, axis=-1).astype(BF16)


def _inproj_kernel(*refs, use_rope):
    x_ref, shift_ref, scale_ref, g_ref, w_ref, qg_ref, kg_ref, lb_ref = refs[:8]
    cos_ref, sin_ref = (refs[8], refs[9]) if use_rope else (None, None)
    y_ref, gates_ref, kva_ref, kvw_ref = refs[-4:]
    h = (_rms_rows(x_ref[...], g_ref[...]) * (1.0 + scale_ref[...]) + shift_ref[...]).astype(BF16)

    def proj(c0, c1):
        return _dot(h, w_ref[:, c0:c1])

    tn = INPROJ_TN
    for c in range(COL_GATES, IN_W, tn):
        gates_ref[:, c - COL_GATES:c - COL_GATES + tn] = _sigmoid(proj(c, c + tn)).astype(BF16)
    for c in range(COL_HQ, COL_HF, tn):
        t = proj(c, c + tn)
        y_ref[:, c:c + tn] = t * _sigmoid(t)
    for c in range(COL_HF, COL_HI, tn):
        lb = lb_ref[:, c - COL_HF:c - COL_HF + tn]
        y_ref[:, c:c + tn] = lb + (1.0 - lb) * _sigmoid(proj(c, c + tn))
    for c in range(COL_AQ, COL_AK, tn):
        y_ref[:, c:c + tn] = _head_rms(proj(c, c + tn), qg_ref[:, c:c + tn])
    for ck, out_ref in ((COL_AK, kva_ref), (COL_WK, kvw_ref)):
        t = proj(ck, ck + 2 * LANES)
        k = t[:, :LANES]
        if ck == COL_AK:
            k = _head_rms(k, kg_ref[...])
        y_ref[:, ck:ck + LANES] = k
        y_ref[:, ck + LANES:ck + 2 * LANES] = t[:, LANES:]
        if use_rope:
            k = _rope(k, cos_ref[...], sin_ref[...])
        out_ref[...] = _kv_operands(k, t[:, LANES:])
    for c in list(range(COL_WQ, COL_WK, tn)) + list(range(COL_HI, COL_GATES, tn)):
        y_ref[:, c:c + tn] = proj(c, c + tn)


def _inproj_call(x, modv, normv, w_in_b, qk_norm_g, lb_all, layer, row_fn, tm, rope=None, seq=None):
    n = x.shape[0]
    use_rope = rope is not None
    qg = jnp.tile(qk_norm_g[layer, 0].astype(F32), (COL_AK - COL_AQ) // HEAD_DIM).reshape(1, COL_AK - COL_AQ)
    kg = jnp.tile(qk_norm_g[layer, 1].astype(F32), LANES // HEAD_DIM).reshape(1, LANES)
    in_specs = [
        pl.BlockSpec((tm, D_MODEL), lambda i: (i, 0)),
        _mod_spec(layer, 3, row_fn), _mod_spec(layer, 4, row_fn),
        pl.BlockSpec((None, 1, D_MODEL), lambda i: (layer * 3 + 1, 0, 0)),
        _resident((None, D_MODEL, IN_W), lambda i: (layer, 0, 0)),
        pl.BlockSpec((1, COL_AK - COL_AQ), lambda i: (0, 0)),
        pl.BlockSpec((1, LANES), lambda i: (0, 0)),
        pl.BlockSpec((None, 1, 2 * HG_W), lambda i: (layer, 0, 0)),
    ]
    args = [x, modv, modv, normv, w_in_b, qg, kg, lb_all]
    if use_rope:
        in_specs += [pl.BlockSpec((tm, LANES), lambda i: (i % (seq // tm), 0))] * 2
        args += list(rope)
    widths = (Y_W, GATES_W, KV_W, KV_W)
    dtypes = (F32, BF16, BF16, BF16)
    return pl.pallas_call(
        functools.partial(_inproj_kernel, use_rope=use_rope),
        grid=(n // tm,),
        in_specs=in_specs,
        out_specs=[pl.BlockSpec((tm, w), lambda i: (i, 0)) for w in widths],
        out_shape=[jax.ShapeDtypeStruct((n, w), dt) for w, dt in zip(widths, dtypes)],
        compiler_params=_params(("parallel",)),
        name="inproj",
    )(*args)


def _attn_kernel(*refs, tq, ts, seq, past, use_rope, use_sink, window, sink_off):
    it = iter(refs)
    sink_ref = next(it) if use_sink else None
    q_refs = (next(it), next(it))
    kv_ref = next(it)
    ck_ref = cv_ref = cos_ref = sin_ref = None
    if past:
        ck_ref, cv_ref = next(it), next(it)
    if use_rope:
        cos_ref, sin_ref = next(it), next(it)
    o_ref = next(it)
    ctx_scr = next(it) if past else None

    qi = pl.program_id(1)
    lane = lax.broadcasted_iota(jnp.int32, (1, LANES), 1)
    lo = lane < HEAD_DIM

    if past:
        @pl.when(qi == 0)
        def _build():
            ctx_scr[...] = _kv_operands(ck_ref[...], cv_ref[...])

    q0 = pl.multiple_of(qi * tq, tq)
    q_scale = (HEAD_DIM ** -0.5) * LOG2E

    def unit_segments(h):
        ctx = [(ctx_scr, slice(0, past), None)] if past else []
        if not window:
            return ctx + [(kv_ref, slice(0, seq), None)]
        span = ts + 2 * WINDOW
        t0 = q0 + h * ts
        start = pl.multiple_of(jnp.clip(t0 - WINDOW, 0, seq - span), WINDOW)
        t_pos = t0 + lax.broadcasted_iota(jnp.int32, (ts, 1), 0)
        s_pos = start + lax.broadcasted_iota(jnp.int32, (1, span), 1)
        band_ok = jnp.abs(t_pos - s_pos) <= WINDOW
        return ctx + [(kv_ref, pl.ds(start, span), band_ok)]

    def keys(g, seg):
        ref, rsl, _ = seg
        return ref[rsl, g * LANES:(g + 1) * LANES]

    def values(g, par, seg):
        ref, rsl, _ = seg
        c0 = (2 + 2 * g + par) * LANES
        return ref[rsl, c0:c0 + LANES]

    def score_phase(g, h, segments):
        xs = []
        for p in range(2):
            q = q_refs[g][h * ts:(h + 1) * ts, p * LANES:(p + 1) * LANES]
            if use_rope:
                tsl = pl.ds(q0 + h * ts, ts)
                q = _rope(q, cos_ref[tsl, :], sin_ref[tsl, :])
            xs.append(q * q_scale)
        q4 = jnp.concatenate([jnp.where(lo, xs[0], 0.0), jnp.where(lo, xs[1], 0.0),
                              jnp.where(lo, 0.0, xs[0]), jnp.where(lo, 0.0, xs[1])], axis=0).astype(BF16)
        return [_dot_nt(q4, keys(g, seg)) for seg in segments]

    def softmax_phase(g, scores, segments):
        probs = [[] for _ in scores]
        extra = []
        for blk, head in enumerate((0, 2, 1, 3)):
            rsl = slice(blk * ts, (blk + 1) * ts)
            rows = [s[rsl] if seg[2] is None else jnp.where(seg[2], s[rsl], NEG_INF) for s, seg in zip(scores, segments)]
            m = None
            for s in rows:
                ms = jnp.max(s, axis=-1, keepdims=True)
                m = ms if m is None else jnp.maximum(m, ms)
            if use_sink:
                sk = sink_ref[sink_off + g * (ATT_HEADS // ATT_KV) + head] * LOG2E
                m = jnp.maximum(m, sk)
                extra.append(jnp.exp2(sk - m))
            for i, s in enumerate(rows):
                probs[i].append(jnp.exp2(s - m).astype(BF16))
        return probs, extra

    def value_phase(g, probs, segments):
        acc = [None, None]
        for i, seg in enumerate(segments):
            for par in range(2):
                e = jnp.concatenate(probs[i][2 * par:2 * par + 2], axis=0)
                pv = _dot(e, values(g, par, seg))
                acc[par] = pv if acc[par] is None else acc[par] + pv
        return acc

    def output_phase(g, h, acc, extra):
        outs = []
        for blk in range(4):
            par = blk // 2
            a = acc[par][(blk % 2) * ts:(blk % 2 + 1) * ts]
            den = jnp.sum(jnp.where(lane == (HEAD_DIM if par == 0 else 0), a, 0.0), axis=-1, keepdims=True)
            if use_sink:
                den = den + extra[blk]
            outs.append(a * (1.0 / den))
        for p in range(2):
            o_pair = jnp.where(lo, outs[p], outs[2 + p])
            c0 = g * GROUP_W + p * LANES
            o_ref[h * ts:(h + 1) * ts, c0:c0 + LANES] = o_pair.astype(o_ref.dtype)

    units = [(g, h) for h in range(tq // ts) for g in range(ATT_KV)]
    segs = {h: unit_segments(h) for h in range(tq // ts)}
    scores = [score_phase(g, h, segs[h]) for g, h in units]
    accs, extras = [], []
    for (g, h), sc in zip(units, scores):
        probs, extra = softmax_phase(g, sc, segs[h])
        accs.append(value_phase(g, probs, segs[h]))
        extras.append(extra)
    for (g, h), acc, extra in zip(units, accs, extras):
        output_phase(g, h, acc, extra)


def _attn_call(y, kv, nb, seq, qcol, *, cache_k=None, cache_v=None, layer=0, rope=None, sink=None, window=False):
    tq = min(ATT_TQ, seq)
    ts = min(ATT_SUB, tq)
    nq = seq // tq
    past = 0 if cache_k is None else cache_k.shape[2]
    use_rope = rope is not None
    use_sink = sink is not None
    in_specs, args = [], []
    if use_sink:
        in_specs.append(pl.BlockSpec(memory_space=pltpu.SMEM))
        args.append(sink.reshape(-1).astype(F32))
    for g in range(ATT_KV):
        in_specs.append(pl.BlockSpec((tq, GROUP_W), lambda b, qi, g=g: (b * nq + qi, qcol // GROUP_W + g)))
        args.append(y)
    in_specs.append(pl.BlockSpec((seq, KV_W), lambda b, qi: (b, 0)))
    args.append(kv)
    if past:
        cshape = cache_k.shape[:3] + (LANES,)
        in_specs += [pl.BlockSpec((None, None, past, LANES), lambda b, qi: (b, layer, 0, 0))] * 2
        args += [cache_k.reshape(cshape), cache_v.reshape(cshape)]
    if use_rope:
        in_specs += [pl.BlockSpec((seq, LANES), lambda b, qi: (0, 0))] * 2
        args += list(rope)
    kern = functools.partial(_attn_kernel, tq=tq, ts=ts, seq=seq, past=past, use_rope=use_rope,
                             use_sink=use_sink, window=window, sink_off=layer * ATT_HEADS)
    return pl.pallas_call(
        kern,
        grid=(nb, nq),
        in_specs=in_specs,
        out_specs=pl.BlockSpec((tq, BRANCH_W), lambda b, qi: (b * nq + qi, 0)),
        out_shape=jax.ShapeDtypeStruct((nb * seq, BRANCH_W), BF16),
        scratch_shapes=[pltpu.VMEM((past, KV_W), BF16)] if past else [],
        compiler_params=_params(("parallel", "arbitrary")),
        name="win" if window or use_sink else "att",
    )(*args)


def _pair_table(fwd):
    C = HG_CHUNK
    t = np.arange(C)[:, None]
    s = np.arange(C)[None, :]
    seen = (s <= t) if fwd else (s >= t)
    tab = np.full((C, C), -1, np.int32)
    hh, idx = C // 2, (C // HG_SUB).bit_length() - 2
    while hh >= HG_SUB:
        tab = np.where((t // (2 * hh) == s // (2 * hh)) & seen, idx, tab)
        hh, idx = hh // 2, idx - 1
    tab = np.where((t // HG_SUB == s // HG_SUB), np.where(seen, PAIR_SUB + s % HG_SUB, -1), tab)
    return tab.astype(np.int32)


def _hgrn_heads(qs, ks, bs, vs, sts, sub_masks, level_masks, fwd):
    C = HG_CHUNK
    G = C // HG_SUB
    heads = range(len(qs))
    vbs = [v.astype(BF16) for v in vs]

    b3 = [b.reshape(G, HG_SUB, HG_DK) for b in bs]
    q3 = [q.reshape(G, HG_SUB, HG_DK) for q in qs]
    c3 = [(jnp.log2(k) - b).reshape(G, HG_SUB, HG_DK) for k, b in zip(ks, bs)]
    a3 = [jnp.zeros((G, HG_SUB, C), F32) for _ in heads]
    for jj in range(HG_SUB):
        for h in heads:
            kdec = jnp.exp2(b3[h] + c3[h][:, jj:jj + 1, :])
            score = jnp.sum(q3[h] * kdec, axis=-1, keepdims=True)
            a3[h] = jnp.where(sub_masks[jj], score, a3[h])
    a_mat = [a.reshape(C, C) for a in a3]

    tcol = lax.broadcasted_iota(jnp.int32, (C, 1), 0)
    hh, idx = HG_SUB, 0
    while hh < C:
        grp = 2 * hh
        later = (tcol & (grp - 1)) >= hh
        is_q = later if fwd else jnp.logical_not(later)
        for h in heads:
            b = bs[h]
            bg = b.reshape(C // grp, grp, HG_DK)
            bnd = bg[:, hh - 1:hh, :] if fwd else bg[:, hh:hh + 1, :]
            bnd = jnp.broadcast_to(bnd, (C // grp, grp, HG_DK)).reshape(C, HG_DK)
            w = (jnp.where(is_q, qs[h], ks[h]) * jnp.exp2(jnp.where(is_q, b - bnd, bnd - b))).astype(BF16)
            a_mat[h] = jnp.where(level_masks[idx], _dot_nt(w, w), a_mat[h])
        hh, idx = grp, idx + 1

    outs, new_states = [], []
    for h in heads:
        q, k, b, st = qs[h], ks[h], bs[h], sts[h]
        out = _dot(a_mat[h].astype(BF16), vbs[h])
        out = out + _dot_nt((q * jnp.exp2(b)).astype(BF16), st.astype(BF16))
        blast = b[C - 1:C, :] if fwd else b[0:1, :]
        kh = (k * jnp.exp2(blast - b)).astype(BF16)
        outs.append(out)
        new_states.append(st * jnp.exp2(blast) + _dot_tn(vbs[h], kh))
    return outs, new_states


def _hgrn_kernel(*refs, nsteps, cps, has_init, emit_state):
    it = iter(refs)
    io = [(next(it), next(it), next(it)) for _ in range(2)]
    pair_ref = next(it)
    s0_ref = next(it) if has_init else None
    o_refs = (next(it), next(it))
    sfin_ref = next(it) if emit_state else None
    st_scr = next(it)

    j = pl.program_id(1)
    C = HG_CHUNK

    @pl.when(j == 0)
    def _():
        for dirn in range(2):
            for h in range(HG_HEADS):
                if has_init:
                    st_scr[dirn, h] = s0_ref[dirn, h].T
                else:
                    st_scr[dirn, h] = jnp.zeros((HG_DK, HG_DK), F32)

    ti = lax.broadcasted_iota(jnp.int32, (C, C), 0)
    si = lax.broadcasted_iota(jnp.int32, (C, C), 1)
    tris, masks = [], []
    for dirn in range(2):
        tris.append(jnp.where((ti >= si) if dirn == 0 else (ti <= si), 1.0, 0.0).astype(BF16))
        pair = pair_ref[dirn]
        pair3 = pair.reshape(C // HG_SUB, HG_SUB, C)
        masks.append(([pair3 == PAIR_SUB + jj for jj in range(HG_SUB)],
                      [pair == idx for idx in range((C // HG_SUB).bit_length() - 1)]))
    states = [[st_scr[dirn, h] for h in range(HG_HEADS)] for dirn in range(2)]
    for cc in range(cps):
        for dirn in range(2):
            fwd = dirn == 0
            q_ref, f_ref, v_ref = io[dirn]
            r0 = (cc if fwd else cps - 1 - cc) * C
            rows = slice(r0, r0 + C)
            f = f_ref[rows, :]
            l2 = jnp.log2(jnp.maximum(f, TINY))
            l_hi = l2.astype(BF16)
            r1 = l2 - l_hi.astype(F32)
            l_mid = r1.astype(BF16)
            l_lo = (r1 - l_mid.astype(F32)).astype(BF16)
            b_all = _dot(tris[dirn], l_hi) + _dot(tris[dirn], l_mid) + _dot(tris[dirn], l_lo)
            cols = [slice(h * HG_DK, (h + 1) * HG_DK) for h in range(HG_HEADS)]
            outs, states[dirn] = _hgrn_heads(
                [q_ref[rows, sl] for sl in cols], [jnp.maximum(1.0 - f[:, sl], 0.0) for sl in cols],
                [b_all[:, sl] for sl in cols], [v_ref[rows, sl] for sl in cols], states[dirn],
                masks[dirn][0], masks[dirn][1], fwd)
            for sl, out in zip(cols, outs):
                o_refs[dirn][rows, sl] = out
    for dirn in range(2):
        for h in range(HG_HEADS):
            st_scr[dirn, h] = states[dirn][h]
            if emit_state:
                @pl.when(j == nsteps - 1)
                def _():
                    sfin_ref[dirn, h] = states[dirn][h].T


def _hgrn_call(y, nb, seq, layer, state0=None, emit_state=False):
    cps = min(HG_STEP_CHUNKS, seq // HG_CHUNK)
    rows = cps * HG_CHUNK
    nsteps = seq // rows
    has_init = state0 is not None
    in_specs, args = [], []
    for dirn in range(2):
        rowblk = (lambda n, j: n * nsteps + j) if dirn == 0 else (lambda n, j: n * nsteps + nsteps - 1 - j)
        for col in (COL_HQ, COL_HF + dirn * HG_W, COL_HI):
            in_specs.append(pl.BlockSpec((rows, HG_W), lambda n, j, rowblk=rowblk, col=col: (rowblk(n, j), col // HG_W)))
            args.append(y)
    in_specs.append(pl.BlockSpec((2, HG_CHUNK, HG_CHUNK), lambda n, j: (0, 0, 0)))
    args.append(jnp.asarray([_pair_table(True), _pair_table(False)]))
    if has_init:
        in_specs.append(pl.BlockSpec((None, None, 2, HG_HEADS, HG_DK, HG_DK), lambda n, j: (n, layer, 0, 0, 0, 0)))
        args.append(state0)
    out_specs = [pl.BlockSpec((rows, HG_W), lambda n, j: (n * nsteps + j, 0)),
                 pl.BlockSpec((rows, HG_W), lambda n, j: (n * nsteps + nsteps - 1 - j, 0))]
    out_shape = [jax.ShapeDtypeStruct((nb * seq, HG_W), F32)] * 2
    if emit_state:
        out_specs.append(pl.BlockSpec((None, 2, HG_HEADS, HG_DK, HG_DK), lambda n, j: (n, 0, 0, 0, 0)))
        out_shape.append(jax.ShapeDtypeStruct((nb, 2, HG_HEADS, HG_DK, HG_DK), F32))
    res = pl.pallas_call(
        functools.partial(_hgrn_kernel, nsteps=nsteps, cps=cps, has_init=has_init, emit_state=emit_state),
        grid=(nb, nsteps),
        in_specs=in_specs,
        out_specs=out_specs,
        out_shape=out_shape,
        scratch_shapes=[pltpu.VMEM((2, HG_HEADS, HG_DK, HG_DK), F32)],
        compiler_params=_params(("parallel", "arbitrary")),
        name="hgrn",
    )(*args)
    return (res[0], res[1], res[2]) if emit_state else (res[0], res[1], None)


def _merge_kernel(*refs, final):
    (oa_ref, ow_ref, of_ref, ob_ref, hg_ref, gates_ref, x_ref, gate_ref, hgn_ref, wb_ref, wo_ref,
     shift2_ref, scale2_ref, gate2_ref, g2_ref, wgu_ref, wd_ref) = refs[:17]
    fg_ref = refs[17] if final else None
    o_ref = refs[-1]
    o = of_ref[...] + ob_ref[...]
    hg = hg_ref[...]
    hgn = hgn_ref[...]
    parts = []
    for h in range(HG_HEADS):
        sl = slice(h * HG_DK, (h + 1) * HG_DK)
        g = hg[:, sl]
        parts.append(_rms_rows(o[:, sl], hgn) * (g * _sigmoid(g)))
    o_hg = jnp.concatenate(parts, axis=-1).astype(BF16)
    branch = (oa_ref[...], o_hg, ow_ref[...])
    merged = None
    for k in range(3):
        term = gates_ref[:, k * D_MODEL:(k + 1) * D_MODEL].astype(F32) * _dot(branch[k], wb_ref[k])
        merged = term if merged is None else merged + term
    yv = _dot(merged.astype(BF16), wo_ref[...])
    x_mid = x_ref[...] + gate_ref[...] * yv
    _ffn_block(x_mid, shift2_ref[...], scale2_ref[...], gate2_ref[...], g2_ref[...], wgu_ref, wd_ref, o_ref,
               fg_ref[...] if final else None)


def _merge_call(x, y, gates, o_att, o_win, o_f, o_b, modv, normv, hg_norm_g, w_branch_b, w_out_b,
                w_ffn_in_b, w_ffn_out_b, layer, row_fn, tm, final_g=None):
    n = x.shape[0]
    final = final_g is not None
    row = lambda i: (i, 0)
    in_specs = [
        pl.BlockSpec((tm, BRANCH_W), row),
        pl.BlockSpec((tm, BRANCH_W), row),
        pl.BlockSpec((tm, HG_W), row),
        pl.BlockSpec((tm, HG_W), row),
        pl.BlockSpec((tm, HG_W), lambda i: (i, COL_HG // HG_W)),
        pl.BlockSpec((tm, GATES_W), row),
        pl.BlockSpec((tm, D_MODEL), row),
        _mod_spec(layer, 5, row_fn),
        pl.BlockSpec((1, HG_DK), lambda i: (0, 0)),
        _resident((None, 3, BRANCH_W, D_MODEL), lambda i: (layer, 0, 0, 0)),
        _resident((None, D_MODEL, D_MODEL), lambda i: (layer, 0, 0)),
    ] + _ffn_specs(layer, 1, row_fn)
    args = [o_att, o_win, o_f, o_b, y, gates, x, modv, hg_norm_g[layer].reshape(1, HG_DK).astype(F32),
            w_branch_b, w_out_b, modv, modv, modv, normv, w_ffn_in_b, w_ffn_out_b]
    if final:
        in_specs.append(pl.BlockSpec((1, D_MODEL), lambda i: (0, 0)))
        args.append(final_g.reshape(1, D_MODEL))
    return pl.pallas_call(
        functools.partial(_merge_kernel, final=final),
        grid=(n // tm,),
        in_specs=in_specs,
        out_specs=pl.BlockSpec((tm, D_MODEL), row),
        out_shape=jax.ShapeDtypeStruct((n, D_MODEL), F32),
        compiler_params=_params(("parallel",)),
        name="merge_ffn",
    )(*args)


def _rope_tables(n_lat):
    t = jnp.arange(n_lat, dtype=jnp.int32)
    row = (t // GRID_W).astype(F32)
    col = (t % GRID_W).astype(F32)
    axis_dim = HEAD_DIM // 2
    inv = ROPE_THETA ** (-jnp.arange(0, axis_dim, 2, dtype=F32) / axis_dim)
    ang_r = row[:, None] * inv
    ang_c = col[:, None] * inv
    cos64 = jnp.concatenate([jnp.cos(ang_r), jnp.cos(ang_r), jnp.cos(ang_c), jnp.cos(ang_c)], axis=-1)
    sin64 = jnp.concatenate([-jnp.sin(ang_r), jnp.sin(ang_r), -jnp.sin(ang_c), jnp.sin(ang_c)], axis=-1)
    return jnp.tile(cos64, (1, 2)), jnp.tile(sin64, (1, 2))


def kernel(x_prompt, x_sample, cache_k_attn, cache_v_attn, cache_k_win, cache_v_win, state_hgrn, c, c_ctx,
           w_mod, b_mod, norm_g, w_ffn_in, w_ffn_out, w_in, qk_norm_g, lower_bounds, hg_norm_g, sink_logit,
           w_branch, w_out, final_norm_g):
    batch, seq, _ = x_prompt.shape
    dec_batch, dec_seq, _ = x_sample.shape
    depth = w_mod.shape[0]

    cond = jnp.zeros((COND_ROWS, D_MODEL), F32).at[0].set(c_ctx).at[1:1 + dec_batch].set(c)
    modv = _mod_call(cond, w_mod, b_mod).reshape(depth * COND_ROWS * N_MOD, 1, D_MODEL)
    lb_all = _lb_call(lower_bounds).reshape(depth, 1, 2 * HG_W)
    normv = norm_g.astype(F32).reshape(depth * 3, 1, D_MODEL)
    w_ffn_in_b = w_ffn_in.astype(BF16)
    w_ffn_out_b = w_ffn_out.astype(BF16)
    w_in_b = w_in.astype(BF16)
    w_branch_b = w_branch.astype(BF16)
    w_out_b = w_out.astype(BF16)
    rope = _rope_tables(dec_seq)

    def run(x, nb, s, latent):
        n = nb * s

        def rows_of(tm):
            return (lambda i: 1 + i // (s // tm)) if latent else (lambda i: 0)

        tm = min(DENSE_TM, s if latent else n)
        tm_ffn = min(FFN_TM, s if latent else n)
        x = x.reshape(n, D_MODEL)
        ctx_out = []
        for l in range(depth):
            x = _ffn_call(x, modv, normv, w_ffn_in_b, w_ffn_out_b, l, rows_of(tm_ffn), tm_ffn)
            y, gates, kv_att, kv_win = _inproj_call(x, modv, normv, w_in_b, qk_norm_g, lb_all, l, rows_of(tm), tm,
                                                    rope=rope if latent else None, seq=s)
            if latent:
                o_att = _attn_call(y, kv_att, nb, s, COL_AQ, cache_k=cache_k_attn, cache_v=cache_v_attn,
                                   layer=l, rope=rope)
                o_win = _attn_call(y, kv_win, nb, s, COL_WQ, cache_k=cache_k_win, cache_v=cache_v_win,
                                   layer=l, rope=rope, sink=sink_logit, window=True)
                o_f, o_b, _ = _hgrn_call(y, nb, s, l, state0=state_hgrn)
            else:
                o_att = _attn_call(y, kv_att, nb, s, COL_AQ)
                o_win = _attn_call(y, kv_win, nb, s, COL_WQ, layer=l, sink=sink_logit)
                o_f, o_b, s_fin = _hgrn_call(y, nb, s, l, emit_state=True)
                kv = lambda col: y[:, col:col + LANES].reshape(nb, s, ATT_KV, HEAD_DIM)
                ctx_out.append((kv(COL_AK), kv(COL_AV), kv(COL_WK), kv(COL_WV), s_fin))
            x = _merge_call(x, y, gates, o_att, o_win, o_f, o_b, modv, normv, hg_norm_g, w_branch_b, w_out_b,
                            w_ffn_in_b, w_ffn_out_b, l, rows_of(tm), tm,
                            final_g=final_norm_g.astype(F32) if l == depth - 1 else None)
        return x.reshape(nb, s, D_MODEL), ctx_out

    y_prompt, ctx_out = run(x_prompt, batch, seq, False)
    y_sample, _ = run(x_sample, dec_batch, dec_seq, True)
    stack = lambda k: jnp.stack([cx[k] for cx in ctx_out], axis=1)
    return (y_prompt, y_sample, stack(0), stack(1), stack(2), stack(3), stack(4))
```

```python
import functools

import jax
import jax.numpy as jnp
import numpy as np
from jax import lax
from jax.experimental import pallas as pl
from jax.experimental.pallas import tpu as pltpu

F32 = jnp.float32
BF16 = jnp.bfloat16

D_MODEL = 1024
GRID_W = 64
HEAD_DIM = 64
ATT_KV = 2
ATT_HEADS = 8
GROUP_W = ATT_HEADS // ATT_KV * HEAD_DIM
WINDOW = 128
HG_HEADS = 4
HG_DK = 128
HG_W = HG_HEADS * HG_DK
BRANCH_W = 512
D_FF = 2816
ROPE_THETA = 10000.0
EPS = 1e-6
NEG_INF = -1e30
TINY = 1e-30
LOG2E = 1.4426950408889634
N_MOD = 9
IN_W = 7168

LANES = 128
SUBLANES = 8

COND_ROWS = 16
DENSE_TM = 512
FFN_TM = 1024
FFN_TF = 256
INPROJ_TN = 256
ATT_TQ = 512
ATT_SUB = 128
HG_CHUNK = 128
HG_STEP_CHUNKS = 8
HG_SUB = SUBLANES
PAIR_SUB = 16
VMEM_LIMIT = 56 * 1024 * 1024

COL_AQ, COL_AK, COL_AV = 0, 512, 640
COL_WQ, COL_WK, COL_WV = 768, 1280, 1408
COL_HQ, COL_HF, COL_HI, COL_HG = 1536, 2048, 3072, 3584
COL_GATES = 4096
Y_W = COL_GATES
KV_W = 6 * LANES
GATES_W = IN_W - COL_GATES


def _params(sem):
    return pltpu.CompilerParams(dimension_semantics=sem, vmem_limit_bytes=VMEM_LIMIT)


def _resident(block_shape, index_map):
    return pl.BlockSpec(block_shape, index_map, pipeline_mode=pl.Buffered(1))


def _sigmoid(x):
    return 1.0 / (1.0 + jnp.exp(-x))


def _dot(a, b):
    return jnp.dot(a, b, preferred_element_type=F32)


def _dot_nt(a, b):
    return lax.dot_general(a, b, (((1,), (1,)), ((), ())), preferred_element_type=F32)


def _dot_tn(a, b):
    return lax.dot_general(a, b, (((0,), (0,)), ((), ())), preferred_element_type=F32)


def _rms_rows(x, gain):
    return x * lax.rsqrt(jnp.mean(x * x, axis=-1, keepdims=True) + EPS) * gain


def _mod_kernel(c_ref, w_ref, b_ref, o_ref):
    c = c_ref[...]
    h = (c * _sigmoid(c)).astype(BF16)
    o_ref[...] = _dot(h, w_ref[...].astype(BF16)) + b_ref[...]


def _mod_call(cond, w_mod, b_mod):
    depth, d, nm = w_mod.shape
    tn = nm // 8
    return pl.pallas_call(
        _mod_kernel,
        grid=(depth, nm // tn),
        in_specs=[pl.BlockSpec((COND_ROWS, d), lambda l, j: (0, 0)),
                  pl.BlockSpec((None, d, tn), lambda l, j: (l, 0, j)),
                  pl.BlockSpec((None, 1, tn), lambda l, j: (l, 0, j))],
        out_specs=pl.BlockSpec((None, COND_ROWS, tn), lambda l, j: (l, 0, j)),
        out_shape=jax.ShapeDtypeStruct((depth, COND_ROWS, nm), F32),
        compiler_params=_params(("parallel", "parallel")),
        name="mod",
    )(cond, w_mod, b_mod.reshape(depth, 1, nm))


def _lb_kernel(x_ref, o_ref):
    x = x_ref[...]
    m = jnp.max(x, axis=0, keepdims=True)
    e = jnp.exp(x - m)
    s = e / jnp.sum(e, axis=0, keepdims=True)
    acc = jnp.zeros_like(s[0:1])
    for l in range(x.shape[0]):
        acc = acc + s[l:l + 1]
        o_ref[l:l + 1, :] = acc - s[0:1]


def _lb_call(lower_bounds):
    depth = lower_bounds.shape[0]
    x = lower_bounds.reshape(depth, -1).astype(F32)
    return pl.pallas_call(_lb_kernel, out_shape=jax.ShapeDtypeStruct(x.shape, F32), name="lb")(x)


def _ffn_block(x, shift, scale, gate, g, wgu_ref, wd_ref, o_ref, final_g):
    h = (_rms_rows(x, g) * (1.0 + scale) + shift).astype(BF16)
    for c in range(D_FF // FFN_TF):
        a = _dot(h, wgu_ref[:, c * FFN_TF:(c + 1) * FFN_TF])
        u = _dot(h, wgu_ref[:, D_FF + c * FFN_TF:D_FF + (c + 1) * FFN_TF])
        act = (a * _sigmoid(a) * u).astype(BF16)
        part = _dot(act, wd_ref[c * FFN_TF:(c + 1) * FFN_TF, :])
        if c == 0:
            o_ref[...] = part
        else:
            o_ref[...] += part
    out = x + 0.5 * gate * o_ref[...]
    if final_g is not None:
        out = _rms_rows(out, final_g)
    o_ref[...] = out


def _ffn_kernel(x_ref, shift_ref, scale_ref, gate_ref, g_ref, wgu_ref, wd_ref, o_ref):
    _ffn_block(x_ref[...], shift_ref[...], scale_ref[...], gate_ref[...], g_ref[...], wgu_ref, wd_ref, o_ref, None)


def _mod_spec(layer, k, row_fn):
    return pl.BlockSpec((None, 1, D_MODEL), lambda i: ((layer * COND_ROWS + row_fn(i)) * N_MOD + k, 0, 0))


def _ffn_specs(layer, which, row_fn):
    k0 = 3 * (2 * which)
    return [
        _mod_spec(layer, k0, row_fn), _mod_spec(layer, k0 + 1, row_fn), _mod_spec(layer, k0 + 2, row_fn),
        pl.BlockSpec((None, 1, D_MODEL), lambda i: (layer * 3 + 2 * which, 0, 0)),
        _resident((None, None, D_MODEL, 2 * D_FF), lambda i: (layer, which, 0, 0)),
        _resident((None, None, D_FF, D_MODEL), lambda i: (layer, which, 0, 0)),
    ]


def _ffn_call(x, modv, normv, w_in_b, w_out_b, layer, row_fn, tm):
    n = x.shape[0]
    return pl.pallas_call(
        _ffn_kernel,
        grid=(n // tm,),
        in_specs=[pl.BlockSpec((tm, D_MODEL), lambda i: (i, 0))] + _ffn_specs(layer, 0, row_fn),
        out_specs=pl.BlockSpec((tm, D_MODEL), lambda i: (i, 0)),
        out_shape=jax.ShapeDtypeStruct((n, D_MODEL), F32),
        compiler_params=_params(("parallel",)),
        name="ffn",
    )(x, modv, modv, modv, normv, w_in_b, w_out_b)


def _head_rms(y, gain):
    lane = lax.broadcasted_iota(jnp.int32, (1, LANES), 1)
    lo = lane < HEAD_DIM
    outs = []
    for c in range(y.shape[1] // LANES):
        blk = y[:, c * LANES:(c + 1) * LANES]
        sq = blk * blk
        s_lo = jnp.sum(jnp.where(lo, sq, 0.0), axis=-1, keepdims=True)
        s_hi = jnp.sum(jnp.where(lo, 0.0, sq), axis=-1, keepdims=True)
        ms = jnp.where(lo, s_lo, s_hi) * (1.0 / HEAD_DIM)
        outs.append(blk * lax.rsqrt(ms + EPS) * gain[:, c * LANES:(c + 1) * LANES])
    return outs[0] if len(outs) == 1 else jnp.concatenate(outs, axis=-1)


def _rope(x, cos, sin_signed):
    lane = lax.broadcasted_iota(jnp.int32, (1, LANES), 1)
    first = (lane & 31) < 16
    swapped = jnp.where(first, pltpu.roll(x, LANES - 16, 1), pltpu.roll(x, 16, 1))
    return x * cos + swapped * sin_signed


def _kv_operands(k, v):
    lane = lax.broadcasted_iota(jnp.int32, (1, LANES), 1)
    lo = lane < HEAD_DIM
    one_lo = jnp.where(lane == 0, 1.0, 0.0)
    one_hi = jnp.where(lane == HEAD_DIM, 1.0, 0.0)
    kr = pltpu.roll(k, HEAD_DIM, 1)
    vr = pltpu.roll(v, HEAD_DIM, 1)
    parts = [jnp.where(lo, k, kr), jnp.where(lo, kr, k),
             jnp.where(lo, v, one_hi), jnp.where(lo, one_lo, vr), jnp.where(lo, vr, one_hi), jnp.where(lo, one_lo, v)]
    return jnp.concatenate(parts, axis=-1).astype(BF16)


def _inproj_kernel(*refs, use_rope):
    x_ref, shift_ref, scale_ref, g_ref, w_ref, qg_ref, kg_ref, lb_ref = refs[:8]
    cos_ref, sin_ref = (refs[8], refs[9]) if use_rope else (None, None)
    y_ref, gates_ref, kva_ref, kvw_ref = refs[-4:]
    h = (_rms_rows(x_ref[...], g_ref[...]) * (1.0 + scale_ref[...]) + shift_ref[...]).astype(BF16)

    def proj(c0, c1):
        return _dot(h, w_ref[:, c0:c1])

    tn = INPROJ_TN
    for c in range(COL_GATES, IN_W, tn):
        gates_ref[:, c - COL_GATES:c - COL_GATES + tn] = _sigmoid(proj(c, c + tn)).astype(BF16)
    for c in range(COL_HQ, COL_HF, tn):
        t = proj(c, c + tn)
        y_ref[:, c:c + tn] = t * _sigmoid(t)
    for c in range(COL_HF, COL_HI, tn):
        lb = lb_ref[:, c - COL_HF:c - COL_HF + tn]
        y_ref[:, c:c + tn] = lb + (1.0 - lb) * _sigmoid(proj(c, c + tn))
    for c in range(COL_AQ, COL_AK, tn):
        y_ref[:, c:c + tn] = _head_rms(proj(c, c + tn), qg_ref[:, c:c + tn])
    for ck, out_ref in ((COL_AK, kva_ref), (COL_WK, kvw_ref)):
        t = proj(ck, ck + 2 * LANES)
        k = t[:, :LANES]
        if ck == COL_AK:
            k = _head_rms(k, kg_ref[...])
        y_ref[:, ck:ck + LANES] = k
        y_ref[:, ck + LANES:ck + 2 * LANES] = t[:, LANES:]
        if use_rope:
            k = _rope(k, cos_ref[...], sin_ref[...])
        out_ref[...] = _kv_operands(k, t[:, LANES:])
    for c in list(range(COL_WQ, COL_WK, tn)) + list(range(COL_HI, COL_GATES, tn)):
        y_ref[:, c:c + tn] = proj(c, c + tn)


def _inproj_call(x, modv, normv, w_in_b, qk_norm_g, lb_all, layer, row_fn, tm, rope=None, seq=None):
    n = x.shape[0]
    use_rope = rope is not None
    qg = jnp.tile(qk_norm_g[layer, 0].astype(F32), (COL_AK - COL_AQ) // HEAD_DIM).reshape(1, COL_AK - COL_AQ)
    kg = jnp.tile(qk_norm_g[layer, 1].astype(F32), LANES // HEAD_DIM).reshape(1, LANES)
    in_specs = [
        pl.BlockSpec((tm, D_MODEL), lambda i: (i, 0)),
        _mod_spec(layer, 3, row_fn), _mod_spec(layer, 4, row_fn),
        pl.BlockSpec((None, 1, D_MODEL), lambda i: (layer * 3 + 1, 0, 0)),
        _resident((None, D_MODEL, IN_W), lambda i: (layer, 0, 0)),
        pl.BlockSpec((1, COL_AK - COL_AQ), lambda i: (0, 0)),
        pl.BlockSpec((1, LANES), lambda i: (0, 0)),
        pl.BlockSpec((None, 1, 2 * HG_W), lambda i: (layer, 0, 0)),
    ]
    args = [x, modv, modv, normv, w_in_b, qg, kg, lb_all]
    if use_rope:
        in_specs += [pl.BlockSpec((tm, LANES), lambda i: (i % (seq // tm), 0))] * 2
        args += list(rope)
    widths = (Y_W, GATES_W, KV_W, KV_W)
    dtypes = (F32, BF16, BF16, BF16)
    return pl.pallas_call(
        functools.partial(_inproj_kernel, use_rope=use_rope),
        grid=(n // tm,),
        in_specs=in_specs,
        out_specs=[pl.BlockSpec((tm, w), lambda i: (i, 0)) for w in widths],
        out_shape=[jax.ShapeDtypeStruct((n, w), dt) for w, dt in zip(widths, dtypes)],
        compiler_params=_params(("parallel",)),
        name="inproj",
    )(*args)


def _attn_kernel(*refs, tq, ts, seq, past, use_rope, use_sink, window, sink_off):
    it = iter(refs)
    sink_ref = next(it) if use_sink else None
    q_refs = (next(it), next(it))
    kv_ref = next(it)
    ck_ref = cv_ref = cos_ref = sin_ref = None
    if past:
        ck_ref, cv_ref = next(it), next(it)
    if use_rope:
        cos_ref, sin_ref = next(it), next(it)
    o_ref = next(it)
    ctx_scr = next(it) if past else None

    qi = pl.program_id(1)
    lane = lax.broadcasted_iota(jnp.int32, (1, LANES), 1)
    lo = lane < HEAD_DIM

    if past:
        @pl.when(qi == 0)
        def _build():
            ctx_scr[...] = _kv_operands(ck_ref[...], cv_ref[...])

    q0 = pl.multiple_of(qi * tq, tq)
    q_scale = (HEAD_DIM ** -0.5) * LOG2E

    def unit_segments(h):
        ctx = [(ctx_scr, slice(0, past), None)] if past else []
        if not window:
            return ctx + [(kv_ref, slice(0, seq), None)]
        span = ts + 2 * WINDOW
        t0 = q0 + h * ts
        start = pl.multiple_of(jnp.clip(t0 - WINDOW, 0, seq - span), WINDOW)
        t_pos = t0 + lax.broadcasted_iota(jnp.int32, (ts, 1), 0)
        s_pos = start + lax.broadcasted_iota(jnp.int32, (1, span), 1)
        band_ok = jnp.abs(t_pos - s_pos) <= WINDOW
        return ctx + [(kv_ref, pl.ds(start, span), band_ok)]

    def keys(g, seg):
        ref, rsl, _ = seg
        return ref[rsl, g * LANES:(g + 1) * LANES]

    def values(g, par, seg):
        ref, rsl, _ = seg
        c0 = (2 + 2 * g + par) * LANES
        return ref[rsl, c0:c0 + LANES]

    def score_phase(g, h, segments):
        xs = []
        for p in range(2):
            q = q_refs[g][h * ts:(h + 1) * ts, p * LANES:(p + 1) * LANES]
            if use_rope:
                tsl = pl.ds(q0 + h * ts, ts)
                q = _rope(q, cos_ref[tsl, :], sin_ref[tsl, :])
            xs.append(q * q_scale)
        q4 = jnp.concatenate([jnp.where(lo, xs[0], 0.0), jnp.where(lo, xs[1], 0.0),
                              jnp.where(lo, 0.0, xs[0]), jnp.where(lo, 0.0, xs[1])], axis=0).astype(BF16)
        return [_dot_nt(q4, keys(g, seg)) for seg in segments]

    def softmax_phase(g, scores, segments):
        probs = [[] for _ in scores]
        extra = []
        for blk, head in enumerate((0, 2, 1, 3)):
            rsl = slice(blk * ts, (blk + 1) * ts)
            rows = [s[rsl] if seg[2] is None else jnp.where(seg[2], s[rsl], NEG_INF) for s, seg in zip(scores, segments)]
            m = None
            for s in rows:
                ms = jnp.max(s, axis=-1, keepdims=True)
                m = ms if m is None else jnp.maximum(m, ms)
            if use_sink:
                sk = sink_ref[sink_off + g * (ATT_HEADS // ATT_KV) + head] * LOG2E
                m = jnp.maximum(m, sk)
                extra.append(jnp.exp2(sk - m))
            for i, s in enumerate(rows):
                probs[i].append(jnp.exp2(s - m).astype(BF16))
        return probs, extra

    def value_phase(g, probs, segments):
        acc = [None, None]
        for i, seg in enumerate(segments):
            for par in range(2):
                e = jnp.concatenate(probs[i][2 * par:2 * par + 2], axis=0)
                pv = _dot(e, values(g, par, seg))
                acc[par] = pv if acc[par] is None else acc[par] + pv
        return acc

    def output_phase(g, h, acc, extra):
        outs = []
        for blk in range(4):
            par = blk // 2
            a = acc[par][(blk % 2) * ts:(blk % 2 + 1) * ts]
            den = jnp.sum(jnp.where(lane == (HEAD_DIM if par == 0 else 0), a, 0.0), axis=-1, keepdims=True)
            if use_sink:
                den = den + extra[blk]
            outs.append(a * (1.0 / den))
        for p in range(2):
            o_pair = jnp.where(lo, outs[p], outs[2 + p])
            c0 = g * GROUP_W + p * LANES
            o_ref[h * ts:(h + 1) * ts, c0:c0 + LANES] = o_pair.astype(o_ref.dtype)

    units = [(g, h) for h in range(tq // ts) for g in range(ATT_KV)]
    segs = {h: unit_segments(h) for h in range(tq // ts)}
    scores = [score_phase(g, h, segs[h]) for g, h in units]
    accs, extras = [], []
    for (g, h), sc in zip(units, scores):
        probs, extra = softmax_phase(g, sc, segs[h])
        accs.append(value_phase(g, probs, segs[h]))
        extras.append(extra)
    for (g, h), acc, extra in zip(units, accs, extras):
        output_phase(g, h, acc, extra)


def _attn_call(y, kv, nb, seq, qcol, *, cache_k=None, cache_v=None, layer=0, rope=None, sink=None, window=False):
    tq = min(ATT_TQ, seq)
    ts = min(ATT_SUB, tq)
    nq = seq // tq
    past = 0 if cache_k is None else cache_k.shape[2]
    use_rope = rope is not None
    use_sink = sink is not None
    in_specs, args = [], []
    if use_sink:
        in_specs.append(pl.BlockSpec(memory_space=pltpu.SMEM))
        args.append(sink.reshape(-1).astype(F32))
    for g in range(ATT_KV):
        in_specs.append(pl.BlockSpec((tq, GROUP_W), lambda b, qi, g=g: (b * nq + qi, qcol // GROUP_W + g)))
        args.append(y)
    in_specs.append(pl.BlockSpec((seq, KV_W), lambda b, qi: (b, 0)))
    args.append(kv)
    if past:
        cshape = cache_k.shape[:3] + (LANES,)
        in_specs += [pl.BlockSpec((None, None, past, LANES), lambda b, qi: (b, layer, 0, 0))] * 2
        args += [cache_k.reshape(cshape), cache_v.reshape(cshape)]
    if use_rope:
        in_specs += [pl.BlockSpec((seq, LANES), lambda b, qi: (0, 0))] * 2
        args += list(rope)
    kern = functools.partial(_attn_kernel, tq=tq, ts=ts, seq=seq, past=past, use_rope=use_rope,
                             use_sink=use_sink, window=window, sink_off=layer * ATT_HEADS)
    return pl.pallas_call(
        kern,
        grid=(nb, nq),
        in_specs=in_specs,
        out_specs=pl.BlockSpec((tq, BRANCH_W), lambda b, qi: (b * nq + qi, 0)),
        out_shape=jax.ShapeDtypeStruct((nb * seq, BRANCH_W), BF16),
        scratch_shapes=[pltpu.VMEM((past, KV_W), BF16)] if past else [],
        compiler_params=_params(("parallel", "arbitrary")),
        name="win" if window or use_sink else "att",
    )(*args)


def _pair_table(fwd):
    C = HG_CHUNK
    t = np.arange(C)[:, None]
    s = np.arange(C)[None, :]
    seen = (s <= t) if fwd else (s >= t)
    tab = np.full((C, C), -1, np.int32)
    hh, idx = C // 2, (C // HG_SUB).bit_length() - 2
    while hh >= HG_SUB:
        tab = np.where((t // (2 * hh) == s // (2 * hh)) & seen, idx, tab)
        hh, idx = hh // 2, idx - 1
    tab = np.where((t // HG_SUB == s // HG_SUB), np.where(seen, PAIR_SUB + s % HG_SUB, -1), tab)
    return tab.astype(np.int32)


def _hgrn_unit(q, k, b, v, st, sub_masks, level_masks, fwd):
    C = HG_CHUNK
    G = C // HG_SUB
    vb = v.astype(BF16)

    b3 = b.reshape(G, HG_SUB, HG_DK)
    q3 = q.reshape(G, HG_SUB, HG_DK)
    c3 = (jnp.log2(k) - b).reshape(G, HG_SUB, HG_DK)
    a3 = jnp.zeros((G, HG_SUB, C), F32)
    for jj in range(HG_SUB):
        kdec = jnp.exp2(b3 + c3[:, jj:jj + 1, :])
        score = jnp.sum(q3 * kdec, axis=-1, keepdims=True)
        a3 = jnp.where(sub_masks[jj], score, a3)
    a_mat = a3.reshape(C, C)

    tcol = lax.broadcasted_iota(jnp.int32, (C, 1), 0)
    hh, idx = HG_SUB, 0
    while hh < C:
        grp = 2 * hh
        later = (tcol & (grp - 1)) >= hh
        is_q = later if fwd else jnp.logical_not(later)
        bg = b.reshape(C // grp, grp, HG_DK)
        bnd = bg[:, hh - 1:hh, :] if fwd else bg[:, hh:hh + 1, :]
        bnd = jnp.broadcast_to(bnd, (C // grp, grp, HG_DK)).reshape(C, HG_DK)
        w = (jnp.where(is_q, q, k) * jnp.exp2(jnp.where(is_q, b - bnd, bnd - b))).astype(BF16)
        a_mat = jnp.where(level_masks[idx], _dot_nt(w, w), a_mat)
        hh, idx = grp, idx + 1
    out = _dot(a_mat.astype(BF16), vb)

    out = out + _dot_nt((q * jnp.exp2(b)).astype(BF16), st.astype(BF16))
    blast = b[C - 1:C, :] if fwd else b[0:1, :]
    kh = (k * jnp.exp2(blast - b)).astype(BF16)
    st_new = st * jnp.exp2(blast) + _dot_tn(vb, kh)
    return out, st_new


def _hgrn_kernel(*refs, nsteps, cps, has_init, emit_state):
    it = iter(refs)
    io = [(next(it), next(it), next(it)) for _ in range(2)]
    pair_ref = next(it)
    s0_ref = next(it) if has_init else None
    o_refs = (next(it), next(it))
    sfin_ref = next(it) if emit_state else None
    st_scr = next(it)

    j = pl.program_id(1)
    C = HG_CHUNK

    @pl.when(j == 0)
    def _():
        for dirn in range(2):
            for h in range(HG_HEADS):
                if has_init:
                    st_scr[dirn, h] = s0_ref[dirn, h].T
                else:
                    st_scr[dirn, h] = jnp.zeros((HG_DK, HG_DK), F32)

    ti = lax.broadcasted_iota(jnp.int32, (C, C), 0)
    si = lax.broadcasted_iota(jnp.int32, (C, C), 1)
    tris, masks = [], []
    for dirn in range(2):
        tris.append(jnp.where((ti >= si) if dirn == 0 else (ti <= si), 1.0, 0.0).astype(BF16))
        pair = pair_ref[dirn]
        pair3 = pair.reshape(C // HG_SUB, HG_SUB, C)
        masks.append(([pair3 == PAIR_SUB + jj for jj in range(HG_SUB)],
                      [pair == idx for idx in range((C // HG_SUB).bit_length() - 1)]))
    states = [[st_scr[dirn, h] for h in range(HG_HEADS)] for dirn in range(2)]
    for cc in range(cps):
        for dirn in range(2):
            fwd = dirn == 0
            q_ref, f_ref, v_ref = io[dirn]
            r0 = (cc if fwd else cps - 1 - cc) * C
            rows = slice(r0, r0 + C)
            f = f_ref[rows, :]
            l2 = jnp.log2(jnp.maximum(f, TINY))
            l_hi = l2.astype(BF16)
            r1 = l2 - l_hi.astype(F32)
            l_mid = r1.astype(BF16)
            l_lo = (r1 - l_mid.astype(F32)).astype(BF16)
            b_all = _dot(tris[dirn], l_hi) + _dot(tris[dirn], l_mid) + _dot(tris[dirn], l_lo)
            for h in range(HG_HEADS):
                sl = slice(h * HG_DK, (h + 1) * HG_DK)
                out, st_new = _hgrn_unit(q_ref[rows, sl], jnp.maximum(1.0 - f[:, sl], 0.0), b_all[:, sl],
                                         v_ref[rows, sl], states[dirn][h], masks[dirn][0], masks[dirn][1], fwd)
                o_refs[dirn][rows, sl] = out
                states[dirn][h] = st_new
    for dirn in range(2):
        for h in range(HG_HEADS):
            st_scr[dirn, h] = states[dirn][h]
            if emit_state:
                @pl.when(j == nsteps - 1)
                def _():
                    sfin_ref[dirn, h] = states[dirn][h].T


def _hgrn_call(y, nb, seq, layer, state0=None, emit_state=False):
    cps = min(HG_STEP_CHUNKS, seq // HG_CHUNK)
    rows = cps * HG_CHUNK
    nsteps = seq // rows
    has_init = state0 is not None
    in_specs, args = [], []
    for dirn in range(2):
        rowblk = (lambda n, j: n * nsteps + j) if dirn == 0 else (lambda n, j: n * nsteps + nsteps - 1 - j)
        for col in (COL_HQ, COL_HF + dirn * HG_W, COL_HI):
            in_specs.append(pl.BlockSpec((rows, HG_W), lambda n, j, rowblk=rowblk, col=col: (rowblk(n, j), col // HG_W)))
            args.append(y)
    in_specs.append(pl.BlockSpec((2, HG_CHUNK, HG_CHUNK), lambda n, j: (0, 0, 0)))
    args.append(jnp.asarray([_pair_table(True), _pair_table(False)]))
    if has_init:
        in_specs.append(pl.BlockSpec((None, None, 2, HG_HEADS, HG_DK, HG_DK), lambda n, j: (n, layer, 0, 0, 0, 0)))
        args.append(state0)
    out_specs = [pl.BlockSpec((rows, HG_W), lambda n, j: (n * nsteps + j, 0)),
                 pl.BlockSpec((rows, HG_W), lambda n, j: (n * nsteps + nsteps - 1 - j, 0))]
    out_shape = [jax.ShapeDtypeStruct((nb * seq, HG_W), F32)] * 2
    if emit_state:
        out_specs.append(pl.BlockSpec((None, 2, HG_HEADS, HG_DK, HG_DK), lambda n, j: (n, 0, 0, 0, 0)))
        out_shape.append(jax.ShapeDtypeStruct((nb, 2, HG_HEADS, HG_DK, HG_DK), F32))
    res = pl.pallas_call(
        functools.partial(_hgrn_kernel, nsteps=nsteps, cps=cps, has_init=has_init, emit_state=emit_state),
        grid=(nb, nsteps),
        in_specs=in_specs,
        out_specs=out_specs,
        out_shape=out_shape,
        scratch_shapes=[pltpu.VMEM((2, HG_HEADS, HG_DK, HG_DK), F32)],
        compiler_params=_params(("parallel", "arbitrary")),
        name="hgrn",
    )(*args)
    return (res[0], res[1], res[2]) if emit_state else (res[0], res[1], None)


def _merge_kernel(*refs, final):
    (oa_ref, ow_ref, of_ref, ob_ref, hg_ref, gates_ref, x_ref, gate_ref, hgn_ref, wb_ref, wo_ref,
     shift2_ref, scale2_ref, gate2_ref, g2_ref, wgu_ref, wd_ref) = refs[:17]
    fg_ref = refs[17] if final else None
    o_ref = refs[-1]
    o = of_ref[...] + ob_ref[...]
    hg = hg_ref[...]
    hgn = hgn_ref[...]
    parts = []
    for h in range(HG_HEADS):
        sl = slice(h * HG_DK, (h + 1) * HG_DK)
        g = hg[:, sl]
        parts.append(_rms_rows(o[:, sl], hgn) * (g * _sigmoid(g)))
    o_hg = jnp.concatenate(parts, axis=-1).astype(BF16)
    branch = (oa_ref[...], o_hg, ow_ref[...])
    merged = None
    for k in range(3):
        term = gates_ref[:, k * D_MODEL:(k + 1) * D_MODEL].astype(F32) * _dot(branch[k], wb_ref[k])
        merged = term if merged is None else merged + term
    yv = _dot(merged.astype(BF16), wo_ref[...])
    x_mid = x_ref[...] + gate_ref[...] * yv
    _ffn_block(x_mid, shift2_ref[...], scale2_ref[...], gate2_ref[...], g2_ref[...], wgu_ref, wd_ref, o_ref,
               fg_ref[...] if final else None)


def _merge_call(x, y, gates, o_att, o_win, o_f, o_b, modv, normv, hg_norm_g, w_branch_b, w_out_b,
                w_ffn_in_b, w_ffn_out_b, layer, row_fn, tm, final_g=None):
    n = x.shape[0]
    final = final_g is not None
    row = lambda i: (i, 0)
    in_specs = [
        pl.BlockSpec((tm, BRANCH_W), row),
        pl.BlockSpec((tm, BRANCH_W), row),
        pl.BlockSpec((tm, HG_W), row),
        pl.BlockSpec((tm, HG_W), row),
        pl.BlockSpec((tm, HG_W), lambda i: (i, COL_HG // HG_W)),
        pl.BlockSpec((tm, GATES_W), row),
        pl.BlockSpec((tm, D_MODEL), row),
        _mod_spec(layer, 5, row_fn),
        pl.BlockSpec((1, HG_DK), lambda i: (0, 0)),
        _resident((None, 3, BRANCH_W, D_MODEL), lambda i: (layer, 0, 0, 0)),
        _resident((None, D_MODEL, D_MODEL), lambda i: (layer, 0, 0)),
    ] + _ffn_specs(layer, 1, row_fn)
    args = [o_att, o_win, o_f, o_b, y, gates, x, modv, hg_norm_g[layer].reshape(1, HG_DK).astype(F32),
            w_branch_b, w_out_b, modv, modv, modv, normv, w_ffn_in_b, w_ffn_out_b]
    if final:
        in_specs.append(pl.BlockSpec((1, D_MODEL), lambda i: (0, 0)))
        args.append(final_g.reshape(1, D_MODEL))
    return pl.pallas_call(
        functools.partial(_merge_kernel, final=final),
        grid=(n // tm,),
        in_specs=in_specs,
        out_specs=pl.BlockSpec((tm, D_MODEL), row),
        out_shape=jax.ShapeDtypeStruct((n, D_MODEL), F32),
        compiler_params=_params(("parallel",)),
        name="merge_ffn",
    )(*args)


def _rope_tables(n_lat):
    t = jnp.arange(n_lat, dtype=jnp.int32)
    row = (t // GRID_W).astype(F32)
    col = (t % GRID_W).astype(F32)
    axis_dim = HEAD_DIM // 2
    inv = ROPE_THETA ** (-jnp.arange(0, axis_dim, 2, dtype=F32) / axis_dim)
    ang_r = row[:, None] * inv
    ang_c = col[:, None] * inv
    cos64 = jnp.concatenate([jnp.cos(ang_r), jnp.cos(ang_r), jnp.cos(ang_c), jnp.cos(ang_c)], axis=-1)
    sin64 = jnp.concatenate([-jnp.sin(ang_r), jnp.sin(ang_r), -jnp.sin(ang_c), jnp.sin(ang_c)], axis=-1)
    return jnp.tile(cos64, (1, 2)), jnp.tile(sin64, (1, 2))


def kernel(x_prompt, x_sample, cache_k_attn, cache_v_attn, cache_k_win, cache_v_win, state_hgrn, c, c_ctx,
           w_mod, b_mod, norm_g, w_ffn_in, w_ffn_out, w_in, qk_norm_g, lower_bounds, hg_norm_g, sink_logit,
           w_branch, w_out, final_norm_g):
    batch, seq, _ = x_prompt.shape
    dec_batch, dec_seq, _ = x_sample.shape
    depth = w_mod.shape[0]

    cond = jnp.zeros((COND_ROWS, D_MODEL), F32).at[0].set(c_ctx).at[1:1 + dec_batch].set(c)
    modv = _mod_call(cond, w_mod, b_mod).reshape(depth * COND_ROWS * N_MOD, 1, D_MODEL)
    lb_all = _lb_call(lower_bounds).reshape(depth, 1, 2 * HG_W)
    normv = norm_g.astype(F32).reshape(depth * 3, 1, D_MODEL)
    w_ffn_in_b = w_ffn_in.astype(BF16)
    w_ffn_out_b = w_ffn_out.astype(BF16)
    w_in_b = w_in.astype(BF16)
    w_branch_b = w_branch.astype(BF16)
    w_out_b = w_out.astype(BF16)
    rope = _rope_tables(dec_seq)

    def run(x, nb, s, latent):
        n = nb * s

        def rows_of(tm):
            return (lambda i: 1 + i // (s // tm)) if latent else (lambda i: 0)

        tm = min(DENSE_TM, s if latent else n)
        tm_ffn = min(FFN_TM, s if latent else n)
        x = x.reshape(n, D_MODEL)
        ctx_out = []
        for l in range(depth):
            x = _ffn_call(x, modv, normv, w_ffn_in_b, w_ffn_out_b, l, rows_of(tm_ffn), tm_ffn)
            y, gates, kv_att, kv_win = _inproj_call(x, modv, normv, w_in_b, qk_norm_g, lb_all, l, rows_of(tm), tm,
                                                    rope=rope if latent else None, seq=s)
            if latent:
                o_att = _attn_call(y, kv_att, nb, s, COL_AQ, cache_k=cache_k_attn, cache_v=cache_v_attn,
                                   layer=l, rope=rope)
                o_win = _attn_call(y, kv_win, nb, s, COL_WQ, cache_k=cache_k_win, cache_v=cache_v_win,
                                   layer=l, rope=rope, sink=sink_logit, window=True)
                o_f, o_b, _ = _hgrn_call(y, nb, s, l, state0=state_hgrn)
            else:
                o_att = _attn_call(y, kv_att, nb, s, COL_AQ)
                o_win = _attn_call(y, kv_win, nb, s, COL_WQ, layer=l, sink=sink_logit)
                o_f, o_b, s_fin = _hgrn_call(y, nb, s, l, emit_state=True)
                kv = lambda col: y[:, col:col + LANES].reshape(nb, s, ATT_KV, HEAD_DIM)
                ctx_out.append((kv(COL_AK), kv(COL_AV), kv(COL_WK), kv(COL_WV), s_fin))
            x = _merge_call(x, y, gates, o_att, o_win, o_f, o_b, modv, normv, hg_norm_g, w_branch_b, w_out_b,
                            w_ffn_in_b, w_ffn_out_b, l, rows_of(tm), tm,
                            final_g=final_norm_g.astype(F32) if l == depth - 1 else None)
        return x.reshape(nb, s, D_MODEL), ctx_out

    y_prompt, ctx_out = run(x_prompt, batch, seq, False)
    y_sample, _ = run(x_sample, dec_batch, dec_seq, True)
    stack = lambda k: jnp.stack([cx[k] for cx in ctx_out], axis=1)
    return (y_prompt, y_sample, stack(0), stack(1), stack(2), stack(3), stack(4))
```

```python
import functools

import jax
import jax.numpy as jnp
import numpy as np
from jax import lax
from jax.experimental import pallas as pl
from jax.experimental.pallas import tpu as pltpu

F32 = jnp.float32
BF16 = jnp.bfloat16

D_MODEL = 1024
GRID_W = 64
HEAD_DIM = 64
ATT_KV = 2
ATT_HEADS = 8
GROUP_W = ATT_HEADS // ATT_KV * HEAD_DIM
WINDOW = 128
HG_HEADS = 4
HG_DK = 128
HG_W = HG_HEADS * HG_DK
BRANCH_W = 512
D_FF = 2816
ROPE_THETA = 10000.0
EPS = 1e-6
NEG_INF = -1e30
TINY = 1e-30
LOG2E = 1.4426950408889634
N_MOD = 9
IN_W = 7168

LANES = 128
SUBLANES = 8

COND_ROWS = 16
DENSE_TM = 512
FFN_TM = 1024
FFN_TF = 256
INPROJ_TN = 256
ATT_TQ = 512
ATT_SUB = 128
HG_CHUNK = 128
HG_STEP_CHUNKS = 4
HG_SUB = SUBLANES
PAIR_SUB = 16
VMEM_LIMIT = 56 * 1024 * 1024

COL_AQ, COL_AK, COL_AV = 0, 512, 640
COL_WQ, COL_WK, COL_WV = 768, 1280, 1408
COL_HQ, COL_HF, COL_HI, COL_HG = 1536, 2048, 3072, 3584
COL_GATES = 4096
Y_W = COL_GATES
KV_W = 6 * LANES
GATES_W = IN_W - COL_GATES


def _params(sem):
    return pltpu.CompilerParams(dimension_semantics=sem, vmem_limit_bytes=VMEM_LIMIT)


def _resident(block_shape, index_map):
    return pl.BlockSpec(block_shape, index_map, pipeline_mode=pl.Buffered(1))


def _sigmoid(x):
    return 1.0 / (1.0 + jnp.exp(-x))


def _dot(a, b):
    return jnp.dot(a, b, preferred_element_type=F32)


def _dot_nt(a, b):
    return lax.dot_general(a, b, (((1,), (1,)), ((), ())), preferred_element_type=F32)


def _dot_tn(a, b):
    return lax.dot_general(a, b, (((0,), (0,)), ((), ())), preferred_element_type=F32)


def _rms_rows(x, gain):
    return x * lax.rsqrt(jnp.mean(x * x, axis=-1, keepdims=True) + EPS) * gain


def _mod_kernel(c_ref, w_ref, b_ref, o_ref):
    c = c_ref[...]
    h = (c * _sigmoid(c)).astype(BF16)
    o_ref[...] = _dot(h, w_ref[...].astype(BF16)) + b_ref[...]


def _mod_call(cond, w_mod, b_mod):
    depth, d, nm = w_mod.shape
    tn = nm // 8
    return pl.pallas_call(
        _mod_kernel,
        grid=(depth, nm // tn),
        in_specs=[pl.BlockSpec((COND_ROWS, d), lambda l, j: (0, 0)),
                  pl.BlockSpec((None, d, tn), lambda l, j: (l, 0, j)),
                  pl.BlockSpec((None, 1, tn), lambda l, j: (l, 0, j))],
        out_specs=pl.BlockSpec((None, COND_ROWS, tn), lambda l, j: (l, 0, j)),
        out_shape=jax.ShapeDtypeStruct((depth, COND_ROWS, nm), F32),
        compiler_params=_params(("parallel", "parallel")),
        name="mod",
    )(cond, w_mod, b_mod.reshape(depth, 1, nm))


def _lb_kernel(x_ref, o_ref):
    x = x_ref[...]
    m = jnp.max(x, axis=0, keepdims=True)
    e = jnp.exp(x - m)
    s = e / jnp.sum(e, axis=0, keepdims=True)
    acc = jnp.zeros_like(s[0:1])
    for l in range(x.shape[0]):
        acc = acc + s[l:l + 1]
        o_ref[l:l + 1, :] = acc - s[0:1]


def _lb_call(lower_bounds):
    depth = lower_bounds.shape[0]
    x = lower_bounds.reshape(depth, -1).astype(F32)
    return pl.pallas_call(_lb_kernel, out_shape=jax.ShapeDtypeStruct(x.shape, F32), name="lb")(x)


def _ffn_block(x, shift, scale, gate, g, wgu_ref, wd_ref, o_ref, final_g):
    h = (_rms_rows(x, g) * (1.0 + scale) + shift).astype(BF16)
    for c in range(D_FF // FFN_TF):
        a = _dot(h, wgu_ref[:, c * FFN_TF:(c + 1) * FFN_TF])
        u = _dot(h, wgu_ref[:, D_FF + c * FFN_TF:D_FF + (c + 1) * FFN_TF])
        act = (a * _sigmoid(a) * u).astype(BF16)
        part = _dot(act, wd_ref[c * FFN_TF:(c + 1) * FFN_TF, :])
        if c == 0:
            o_ref[...] = part
        else:
            o_ref[...] += part
    out = x + 0.5 * gate * o_ref[...]
    if final_g is not None:
        out = _rms_rows(out, final_g)
    o_ref[...] = out


def _ffn_kernel(x_ref, shift_ref, scale_ref, gate_ref, g_ref, wgu_ref, wd_ref, o_ref):
    _ffn_block(x_ref[...], shift_ref[...], scale_ref[...], gate_ref[...], g_ref[...], wgu_ref, wd_ref, o_ref, None)


def _mod_spec(layer, k, row_fn):
    return pl.BlockSpec((None, 1, D_MODEL), lambda i: ((layer * COND_ROWS + row_fn(i)) * N_MOD + k, 0, 0))


def _ffn_specs(layer, which, row_fn):
    k0 = 3 * (2 * which)
    return [
        _mod_spec(layer, k0, row_fn), _mod_spec(layer, k0 + 1, row_fn), _mod_spec(layer, k0 + 2, row_fn),
        pl.BlockSpec((None, 1, D_MODEL), lambda i: (layer * 3 + 2 * which, 0, 0)),
        _resident((None, None, D_MODEL, 2 * D_FF), lambda i: (layer, which, 0, 0)),
        _resident((None, None, D_FF, D_MODEL), lambda i: (layer, which, 0, 0)),
    ]


def _ffn_call(x, modv, normv, w_in_b, w_out_b, layer, row_fn, tm):
    n = x.shape[0]
    return pl.pallas_call(
        _ffn_kernel,
        grid=(n // tm,),
        in_specs=[pl.BlockSpec((tm, D_MODEL), lambda i: (i, 0))] + _ffn_specs(layer, 0, row_fn),
        out_specs=pl.BlockSpec((tm, D_MODEL), lambda i: (i, 0)),
        out_shape=jax.ShapeDtypeStruct((n, D_MODEL), F32),
        compiler_params=_params(("parallel",)),
        name="ffn",
    )(x, modv, modv, modv, normv, w_in_b, w_out_b)


def _head_rms(y, gain):
    lane = lax.broadcasted_iota(jnp.int32, (1, LANES), 1)
    lo = lane < HEAD_DIM
    outs = []
    for c in range(y.shape[1] // LANES):
        blk = y[:, c * LANES:(c + 1) * LANES]
        sq = blk * blk
        s_lo = jnp.sum(jnp.where(lo, sq, 0.0), axis=-1, keepdims=True)
        s_hi = jnp.sum(jnp.where(lo, 0.0, sq), axis=-1, keepdims=True)
        ms = jnp.where(lo, s_lo, s_hi) * (1.0 / HEAD_DIM)
        outs.append(blk * lax.rsqrt(ms + EPS) * gain[:, c * LANES:(c + 1) * LANES])
    return outs[0] if len(outs) == 1 else jnp.concatenate(outs, axis=-1)


def _rope(x, cos, sin_signed):
    lane = lax.broadcasted_iota(jnp.int32, (1, LANES), 1)
    first = (lane & 31) < 16
    swapped = jnp.where(first, pltpu.roll(x, LANES - 16, 1), pltpu.roll(x, 16, 1))
    return x * cos + swapped * sin_signed


def _kv_operands(k, v):
    lane = lax.broadcasted_iota(jnp.int32, (1, LANES), 1)
    lo = lane < HEAD_DIM
    one_lo = jnp.where(lane == 0, 1.0, 0.0)
    one_hi = jnp.where(lane == HEAD_DIM, 1.0, 0.0)
    kr = pltpu.roll(k, HEAD_DIM, 1)
    vr = pltpu.roll(v, HEAD_DIM, 1)
    parts = [jnp.where(lo, k, kr), jnp.where(lo, kr, k),
             jnp.where(lo, v, one_hi), jnp.where(lo, one_lo, vr), jnp.where(lo, vr, one_hi), jnp.where(lo, one_lo, v)]
    return jnp.concatenate(parts, axis=-1).astype(BF16)


def _inproj_kernel(*refs, use_rope, emit_kv):
    x_ref, shift_ref, scale_ref, g_ref, w_ref, qg_ref, kg_ref, lb_ref = refs[:8]
    cos_ref, sin_ref = (refs[8], refs[9]) if use_rope else (None, None)
    outs = refs[10:] if use_rope else refs[8:]
    y_ref, gates_ref, kva_ref, kvw_ref = outs[:4]
    leaf_refs = outs[4:]
    h = (_rms_rows(x_ref[...], g_ref[...]) * (1.0 + scale_ref[...]) + shift_ref[...]).astype(BF16)

    def proj(c0, c1):
        return _dot(h, w_ref[:, c0:c1])

    tn = INPROJ_TN
    for c in range(COL_GATES, IN_W, tn):
        gates_ref[:, c - COL_GATES:c - COL_GATES + tn] = _sigmoid(proj(c, c + tn)).astype(BF16)
    for c in range(COL_HQ, COL_HF, tn):
        t = proj(c, c + tn)
        y_ref[:, c:c + tn] = t * _sigmoid(t)
    for c in range(COL_HF, COL_HI, tn):
        lb = lb_ref[:, c - COL_HF:c - COL_HF + tn]
        y_ref[:, c:c + tn] = lb + (1.0 - lb) * _sigmoid(proj(c, c + tn))
    for c in range(COL_AQ, COL_AK, tn):
        y_ref[:, c:c + tn] = _head_rms(proj(c, c + tn), qg_ref[:, c:c + tn])
    for i, (ck, out_ref) in enumerate(((COL_AK, kva_ref), (COL_WK, kvw_ref))):
        t = proj(ck, ck + 2 * LANES)
        k = t[:, :LANES]
        if ck == COL_AK:
            k = _head_rms(k, kg_ref[...])
        y_ref[:, ck:ck + LANES] = k
        y_ref[:, ck + LANES:ck + 2 * LANES] = t[:, LANES:]
        if emit_kv:
            leaf_refs[2 * i][...] = k
            leaf_refs[2 * i + 1][...] = t[:, LANES:]
        if use_rope:
            k = _rope(k, cos_ref[...], sin_ref[...])
        out_ref[...] = _kv_operands(k, t[:, LANES:])
    for c in list(range(COL_WQ, COL_WK, tn)) + list(range(COL_HI, COL_GATES, tn)):
        y_ref[:, c:c + tn] = proj(c, c + tn)


def _inproj_call(x, modv, normv, w_in_b, qk_norm_g, lb_all, layer, row_fn, tm, rope=None, seq=None, emit_kv=False):
    n = x.shape[0]
    use_rope = rope is not None
    qg = jnp.tile(qk_norm_g[layer, 0].astype(F32), (COL_AK - COL_AQ) // HEAD_DIM).reshape(1, COL_AK - COL_AQ)
    kg = jnp.tile(qk_norm_g[layer, 1].astype(F32), LANES // HEAD_DIM).reshape(1, LANES)
    in_specs = [
        pl.BlockSpec((tm, D_MODEL), lambda i: (i, 0)),
        _mod_spec(layer, 3, row_fn), _mod_spec(layer, 4, row_fn),
        pl.BlockSpec((None, 1, D_MODEL), lambda i: (layer * 3 + 1, 0, 0)),
        _resident((None, D_MODEL, IN_W), lambda i: (layer, 0, 0)),
        pl.BlockSpec((1, COL_AK - COL_AQ), lambda i: (0, 0)),
        pl.BlockSpec((1, LANES), lambda i: (0, 0)),
        pl.BlockSpec((None, 1, 2 * HG_W), lambda i: (layer, 0, 0)),
    ]
    args = [x, modv, modv, normv, w_in_b, qg, kg, lb_all]
    if use_rope:
        in_specs += [pl.BlockSpec((tm, LANES), lambda i: (i % (seq // tm), 0))] * 2
        args += list(rope)
    widths = (Y_W, GATES_W, KV_W, KV_W) + (LANES,) * (4 if emit_kv else 0)
    dtypes = (F32, BF16, BF16, BF16) + (F32,) * (4 if emit_kv else 0)
    return pl.pallas_call(
        functools.partial(_inproj_kernel, use_rope=use_rope, emit_kv=emit_kv),
        grid=(n // tm,),
        in_specs=in_specs,
        out_specs=[pl.BlockSpec((tm, w), lambda i: (i, 0)) for w in widths],
        out_shape=[jax.ShapeDtypeStruct((n, w), dt) for w, dt in zip(widths, dtypes)],
        compiler_params=_params(("parallel",)),
        name="inproj",
    )(*args)


def _attn_kernel(*refs, tq, ts, seq, past, use_rope, use_sink, window, sink_off):
    it = iter(refs)
    sink_ref = next(it) if use_sink else None
    q_refs = (next(it), next(it))
    kv_ref = next(it)
    ck_ref = cv_ref = cos_ref = sin_ref = None
    if past:
        ck_ref, cv_ref = next(it), next(it)
    if use_rope:
        cos_ref, sin_ref = next(it), next(it)
    o_ref = next(it)
    ctx_scr = next(it) if past else None

    qi = pl.program_id(1)
    lane = lax.broadcasted_iota(jnp.int32, (1, LANES), 1)
    lo = lane < HEAD_DIM

    if past:
        @pl.when(qi == 0)
        def _build():
            ctx_scr[...] = _kv_operands(ck_ref[...], cv_ref[...])

    q0 = pl.multiple_of(qi * tq, tq)
    q_scale = (HEAD_DIM ** -0.5) * LOG2E

    def unit_segments(h):
        ctx = [(ctx_scr, slice(0, past), None)] if past else []
        if not window:
            return ctx + [(kv_ref, slice(0, seq), None)]
        span = ts + 2 * WINDOW
        t0 = q0 + h * ts
        start = pl.multiple_of(jnp.clip(t0 - WINDOW, 0, seq - span), WINDOW)
        t_pos = t0 + lax.broadcasted_iota(jnp.int32, (ts, 1), 0)
        s_pos = start + lax.broadcasted_iota(jnp.int32, (1, span), 1)
        band_ok = jnp.abs(t_pos - s_pos) <= WINDOW
        return ctx + [(kv_ref, pl.ds(start, span), band_ok)]

    def keys(g, seg):
        ref, rsl, _ = seg
        return ref[rsl, g * LANES:(g + 1) * LANES]

    def values(g, par, seg):
        ref, rsl, _ = seg
        c0 = (2 + 2 * g + par) * LANES
        return ref[rsl, c0:c0 + LANES]

    def score_phase(g, h, segments):
        xs = []
        for p in range(2):
            q = q_refs[g][h * ts:(h + 1) * ts, p * LANES:(p + 1) * LANES]
            if use_rope:
                tsl = pl.ds(q0 + h * ts, ts)
                q = _rope(q, cos_ref[tsl, :], sin_ref[tsl, :])
            xs.append(q * q_scale)
        q4 = jnp.concatenate([jnp.where(lo, xs[0], 0.0), jnp.where(lo, xs[1], 0.0),
                              jnp.where(lo, 0.0, xs[0]), jnp.where(lo, 0.0, xs[1])], axis=0).astype(BF16)
        return [_dot_nt(q4, keys(g, seg)) for seg in segments]

    def softmax_phase(g, scores, segments):
        probs = [[] for _ in scores]
        extra = []
        for blk, head in enumerate((0, 2, 1, 3)):
            rsl = slice(blk * ts, (blk + 1) * ts)
            rows = [s[rsl] if seg[2] is None else jnp.where(seg[2], s[rsl], NEG_INF) for s, seg in zip(scores, segments)]
            m = None
            for s in rows:
                ms = jnp.max(s, axis=-1, keepdims=True)
                m = ms if m is None else jnp.maximum(m, ms)
            if use_sink:
                sk = sink_ref[sink_off + g * (ATT_HEADS // ATT_KV) + head] * LOG2E
                m = jnp.maximum(m, sk)
                extra.append(jnp.exp2(sk - m))
            for i, s in enumerate(rows):
                probs[i].append(jnp.exp2(s - m).astype(BF16))
        return probs, extra

    def value_phase(g, probs, segments):
        acc = [None, None]
        for i, seg in enumerate(segments):
            for par in range(2):
                e = jnp.concatenate(probs[i][2 * par:2 * par + 2], axis=0)
                pv = _dot(e, values(g, par, seg))
                acc[par] = pv if acc[par] is None else acc[par] + pv
        return acc

    def output_phase(g, h, acc, extra):
        outs = []
        for blk in range(4):
            par = blk // 2
            a = acc[par][(blk % 2) * ts:(blk % 2 + 1) * ts]
            den = jnp.sum(jnp.where(lane == (HEAD_DIM if par == 0 else 0), a, 0.0), axis=-1, keepdims=True)
            if use_sink:
                den = den + extra[blk]
            outs.append(a * (1.0 / den))
        for p in range(2):
            o_pair = jnp.where(lo, outs[p], outs[2 + p])
            c0 = g * GROUP_W + p * LANES
            o_ref[h * ts:(h + 1) * ts, c0:c0 + LANES] = o_pair.astype(o_ref.dtype)

    units = [(g, h) for h in range(tq // ts) for g in range(ATT_KV)]
    segs = {h: unit_segments(h) for h in range(tq // ts)}
    scores = [score_phase(g, h, segs[h]) for g, h in units]
    accs, extras = [], []
    for (g, h), sc in zip(units, scores):
        probs, extra = softmax_phase(g, sc, segs[h])
        accs.append(value_phase(g, probs, segs[h]))
        extras.append(extra)
    for (g, h), acc, extra in zip(units, accs, extras):
        output_phase(g, h, acc, extra)


def _attn_call(y, kv, nb, seq, qcol, *, cache_k=None, cache_v=None, layer=0, rope=None, sink=None, window=False):
    tq = min(ATT_TQ, seq)
    ts = min(ATT_SUB, tq)
    nq = seq // tq
    past = 0 if cache_k is None else cache_k.shape[2]
    use_rope = rope is not None
    use_sink = sink is not None
    in_specs, args = [], []
    if use_sink:
        in_specs.append(pl.BlockSpec(memory_space=pltpu.SMEM))
        args.append(sink.reshape(-1).astype(F32))
    for g in range(ATT_KV):
        in_specs.append(pl.BlockSpec((tq, GROUP_W), lambda b, qi, g=g: (b * nq + qi, qcol // GROUP_W + g)))
        args.append(y)
    in_specs.append(pl.BlockSpec((seq, KV_W), lambda b, qi: (b, 0)))
    args.append(kv)
    if past:
        cshape = cache_k.shape[:3] + (LANES,)
        in_specs += [pl.BlockSpec((None, None, past, LANES), lambda b, qi: (b, layer, 0, 0))] * 2
        args += [cache_k.reshape(cshape), cache_v.reshape(cshape)]
    if use_rope:
        in_specs += [pl.BlockSpec((seq, LANES), lambda b, qi: (0, 0))] * 2
        args += list(rope)
    kern = functools.partial(_attn_kernel, tq=tq, ts=ts, seq=seq, past=past, use_rope=use_rope,
                             use_sink=use_sink, window=window, sink_off=layer * ATT_HEADS)
    return pl.pallas_call(
        kern,
        grid=(nb, nq),
        in_specs=in_specs,
        out_specs=pl.BlockSpec((tq, BRANCH_W), lambda b, qi: (b * nq + qi, 0)),
        out_shape=jax.ShapeDtypeStruct((nb * seq, BRANCH_W), BF16),
        scratch_shapes=[pltpu.VMEM((past, KV_W), BF16)] if past else [],
        compiler_params=_params(("parallel", "arbitrary")),
        name="win" if window or use_sink else "att",
    )(*args)


def _pair_table(fwd):
    C = HG_CHUNK
    t = np.arange(C)[:, None]
    s = np.arange(C)[None, :]
    seen = (s <= t) if fwd else (s >= t)
    tab = np.full((C, C), -1, np.int32)
    hh, idx = C // 2, (C // HG_SUB).bit_length() - 2
    while hh >= HG_SUB:
        tab = np.where((t // (2 * hh) == s // (2 * hh)) & seen, idx, tab)
        hh, idx = hh // 2, idx - 1
    tab = np.where((t // HG_SUB == s // HG_SUB), np.where(seen, PAIR_SUB + s % HG_SUB, -1), tab)
    return tab.astype(np.int32)


def _hgrn_unit(q, k, b, v, st, sub_masks, level_masks, fwd):
    C = HG_CHUNK
    G = C // HG_SUB
    vb = v.astype(BF16)

    b3 = b.reshape(G, HG_SUB, HG_DK)
    q3 = q.reshape(G, HG_SUB, HG_DK)
    c3 = (jnp.log2(k) - b).reshape(G, HG_SUB, HG_DK)
    a3 = jnp.zeros((G, HG_SUB, C), F32)
    for jj in range(HG_SUB):
        kdec = jnp.exp2(b3 + c3[:, jj:jj + 1, :])
        score = jnp.sum(q3 * kdec, axis=-1, keepdims=True)
        a3 = jnp.where(sub_masks[jj], score, a3)
    a_mat = a3.reshape(C, C)

    tcol = lax.broadcasted_iota(jnp.int32, (C, 1), 0)
    hh, idx = HG_SUB, 0
    while hh < C:
        grp = 2 * hh
        later = (tcol & (grp - 1)) >= hh
        is_q = later if fwd else jnp.logical_not(later)
        bg = b.reshape(C // grp, grp, HG_DK)
        bnd = bg[:, hh - 1:hh, :] if fwd else bg[:, hh:hh + 1, :]
        bnd = jnp.broadcast_to(bnd, (C // grp, grp, HG_DK)).reshape(C, HG_DK)
        w = (jnp.where(is_q, q, k) * jnp.exp2(jnp.where(is_q, b - bnd, bnd - b))).astype(BF16)
        a_mat = jnp.where(level_masks[idx], _dot_nt(w, w), a_mat)
        hh, idx = grp, idx + 1
    out = _dot(a_mat.astype(BF16), vb)

    out = out + _dot_nt((q * jnp.exp2(b)).astype(BF16), st.astype(BF16))
    blast = b[C - 1:C, :] if fwd else b[0:1, :]
    kh = (k * jnp.exp2(blast - b)).astype(BF16)
    st_new = st * jnp.exp2(blast) + _dot_tn(vb, kh)
    return out, st_new


def _hgrn_kernel(*refs, nsteps, cps, has_init, emit_state):
    it = iter(refs)
    io = [(next(it), next(it), next(it)) for _ in range(2)]
    pair_ref = next(it)
    s0_ref = next(it) if has_init else None
    o_refs = (next(it), next(it))
    sfin_ref = next(it) if emit_state else None
    st_scr = next(it)

    j = pl.program_id(1)
    C = HG_CHUNK

    @pl.when(j == 0)
    def _():
        for dirn in range(2):
            for h in range(HG_HEADS):
                if has_init:
                    st_scr[dirn, h] = s0_ref[dirn, h].T
                else:
                    st_scr[dirn, h] = jnp.zeros((HG_DK, HG_DK), F32)

    ti = lax.broadcasted_iota(jnp.int32, (C, C), 0)
    si = lax.broadcasted_iota(jnp.int32, (C, C), 1)
    tris, masks = [], []
    for dirn in range(2):
        tris.append(jnp.where((ti >= si) if dirn == 0 else (ti <= si), 1.0, 0.0).astype(BF16))
        pair = pair_ref[dirn]
        pair3 = pair.reshape(C // HG_SUB, HG_SUB, C)
        masks.append(([pair3 == PAIR_SUB + jj for jj in range(HG_SUB)],
                      [pair == idx for idx in range((C // HG_SUB).bit_length() - 1)]))
    states = [[st_scr[dirn, h] for h in range(HG_HEADS)] for dirn in range(2)]
    for cc in range(cps):
        for dirn in range(2):
            fwd = dirn == 0
            q_ref, f_ref, v_ref = io[dirn]
            r0 = (cc if fwd else cps - 1 - cc) * C
            rows = slice(r0, r0 + C)
            f = f_ref[rows, :]
            l2 = jnp.log2(jnp.maximum(f, TINY))
            l_hi = l2.astype(BF16)
            r1 = l2 - l_hi.astype(F32)
            l_mid = r1.astype(BF16)
            l_lo = (r1 - l_mid.astype(F32)).astype(BF16)
            b_all = _dot(tris[dirn], l_hi) + _dot(tris[dirn], l_mid) + _dot(tris[dirn], l_lo)
            for h in range(HG_HEADS):
                sl = slice(h * HG_DK, (h + 1) * HG_DK)
                out, st_new = _hgrn_unit(q_ref[rows, sl], jnp.maximum(1.0 - f[:, sl], 0.0), b_all[:, sl],
                                         v_ref[rows, sl], states[dirn][h], masks[dirn][0], masks[dirn][1], fwd)
                o_refs[dirn][rows, sl] = out
                states[dirn][h] = st_new
    for dirn in range(2):
        for h in range(HG_HEADS):
            st_scr[dirn, h] = states[dirn][h]
            if emit_state:
                @pl.when(j == nsteps - 1)
                def _():
                    sfin_ref[dirn, h] = states[dirn][h].T


def _hgrn_call(y, nb, seq, layer, state0=None, emit_state=False):
    cps = min(HG_STEP_CHUNKS, seq // HG_CHUNK)
    rows = cps * HG_CHUNK
    nsteps = seq // rows
    has_init = state0 is not None
    in_specs, args = [], []
    for dirn in range(2):
        rowblk = (lambda n, j: n * nsteps + j) if dirn == 0 else (lambda n, j: n * nsteps + nsteps - 1 - j)
        for col in (COL_HQ, COL_HF + dirn * HG_W, COL_HI):
            in_specs.append(pl.BlockSpec((rows, HG_W), lambda n, j, rowblk=rowblk, col=col: (rowblk(n, j), col // HG_W)))
            args.append(y)
    in_specs.append(pl.BlockSpec((2, HG_CHUNK, HG_CHUNK), lambda n, j: (0, 0, 0)))
    args.append(jnp.asarray([_pair_table(True), _pair_table(False)]))
    if has_init:
        in_specs.append(pl.BlockSpec((None, None, 2, HG_HEADS, HG_DK, HG_DK), lambda n, j: (n, layer, 0, 0, 0, 0)))
        args.append(state0)
    out_specs = [pl.BlockSpec((rows, HG_W), lambda n, j: (n * nsteps + j, 0)),
                 pl.BlockSpec((rows, HG_W), lambda n, j: (n * nsteps + nsteps - 1 - j, 0))]
    out_shape = [jax.ShapeDtypeStruct((nb * seq, HG_W), F32)] * 2
    if emit_state:
        out_specs.append(pl.BlockSpec((None, 2, HG_HEADS, HG_DK, HG_DK), lambda n, j: (n, 0, 0, 0, 0)))
        out_shape.append(jax.ShapeDtypeStruct((nb, 2, HG_HEADS, HG_DK, HG_DK), F32))
    res = pl.pallas_call(
        functools.partial(_hgrn_kernel, nsteps=nsteps, cps=cps, has_init=has_init, emit_state=emit_state),
        grid=(nb, nsteps),
        in_specs=in_specs,
        out_specs=out_specs,
        out_shape=out_shape,
        scratch_shapes=[pltpu.VMEM((2, HG_HEADS, HG_DK, HG_DK), F32)],
        compiler_params=_params(("parallel", "arbitrary")),
        name="hgrn",
    )(*args)
    return (res[0], res[1], res[2]) if emit_state else (res[0], res[1], None)


def _merge_kernel(*refs, final):
    (oa_ref, ow_ref, of_ref, ob_ref, hg_ref, gates_ref, x_ref, gate_ref, hgn_ref, wb_ref, wo_ref,
     shift2_ref, scale2_ref, gate2_ref, g2_ref, wgu_ref, wd_ref) = refs[:17]
    fg_ref = refs[17] if final else None
    o_ref = refs[-1]
    o = of_ref[...] + ob_ref[...]
    hg = hg_ref[...]
    hgn = hgn_ref[...]
    parts = []
    for h in range(HG_HEADS):
        sl = slice(h * HG_DK, (h + 1) * HG_DK)
        g = hg[:, sl]
        parts.append(_rms_rows(o[:, sl], hgn) * (g * _sigmoid(g)))
    o_hg = jnp.concatenate(parts, axis=-1).astype(BF16)
    branch = (oa_ref[...], o_hg, ow_ref[...])
    merged = None
    for k in range(3):
        term = gates_ref[:, k * D_MODEL:(k + 1) * D_MODEL].astype(F32) * _dot(branch[k], wb_ref[k])
        merged = term if merged is None else merged + term
    yv = _dot(merged.astype(BF16), wo_ref[...])
    x_mid = x_ref[...] + gate_ref[...] * yv
    _ffn_block(x_mid, shift2_ref[...], scale2_ref[...], gate2_ref[...], g2_ref[...], wgu_ref, wd_ref, o_ref,
               fg_ref[...] if final else None)


def _merge_call(x, y, gates, o_att, o_win, o_f, o_b, modv, normv, hg_norm_g, w_branch_b, w_out_b,
                w_ffn_in_b, w_ffn_out_b, layer, row_fn, tm, final_g=None):
    n = x.shape[0]
    final = final_g is not None
    row = lambda i: (i, 0)
    in_specs = [
        pl.BlockSpec((tm, BRANCH_W), row),
        pl.BlockSpec((tm, BRANCH_W), row),
        pl.BlockSpec((tm, HG_W), row),
        pl.BlockSpec((tm, HG_W), row),
        pl.BlockSpec((tm, HG_W), lambda i: (i, COL_HG // HG_W)),
        pl.BlockSpec((tm, GATES_W), row),
        pl.BlockSpec((tm, D_MODEL), row),
        _mod_spec(layer, 5, row_fn),
        pl.BlockSpec((1, HG_DK), lambda i: (0, 0)),
        _resident((None, 3, BRANCH_W, D_MODEL), lambda i: (layer, 0, 0, 0)),
        _resident((None, D_MODEL, D_MODEL), lambda i: (layer, 0, 0)),
    ] + _ffn_specs(layer, 1, row_fn)
    args = [o_att, o_win, o_f, o_b, y, gates, x, modv, hg_norm_g[layer].reshape(1, HG_DK).astype(F32),
            w_branch_b, w_out_b, modv, modv, modv, normv, w_ffn_in_b, w_ffn_out_b]
    if final:
        in_specs.append(pl.BlockSpec((1, D_MODEL), lambda i: (0, 0)))
        args.append(final_g.reshape(1, D_MODEL))
    return pl.pallas_call(
        functools.partial(_merge_kernel, final=final),
        grid=(n // tm,),
        in_specs=in_specs,
        out_specs=pl.BlockSpec((tm, D_MODEL), row),
        out_shape=jax.ShapeDtypeStruct((n, D_MODEL), F32),
        compiler_params=_params(("parallel",)),
        name="merge_ffn",
    )(*args)


def _rope_tables(n_lat):
    t = jnp.arange(n_lat, dtype=jnp.int32)
    row = (t // GRID_W).astype(F32)
    col = (t % GRID_W).astype(F32)
    axis_dim = HEAD_DIM // 2
    inv = ROPE_THETA ** (-jnp.arange(0, axis_dim, 2, dtype=F32) / axis_dim)
    ang_r = row[:, None] * inv
    ang_c = col[:, None] * inv
    cos64 = jnp.concatenate([jnp.cos(ang_r), jnp.cos(ang_r), jnp.cos(ang_c), jnp.cos(ang_c)], axis=-1)
    sin64 = jnp.concatenate([-jnp.sin(ang_r), jnp.sin(ang_r), -jnp.sin(ang_c), jnp.sin(ang_c)], axis=-1)
    return jnp.tile(cos64, (1, 2)), jnp.tile(sin64, (1, 2))


def kernel(x_prompt, x_sample, cache_k_attn, cache_v_attn, cache_k_win, cache_v_win, state_hgrn, c, c_ctx,
           w_mod, b_mod, norm_g, w_ffn_in, w_ffn_out, w_in, qk_norm_g, lower_bounds, hg_norm_g, sink_logit,
           w_branch, w_out, final_norm_g):
    batch, seq, _ = x_prompt.shape
    dec_batch, dec_seq, _ = x_sample.shape
    depth = w_mod.shape[0]

    cond = jnp.zeros((COND_ROWS, D_MODEL), F32).at[0].set(c_ctx).at[1:1 + dec_batch].set(c)
    modv = _mod_call(cond, w_mod, b_mod).reshape(depth * COND_ROWS * N_MOD, 1, D_MODEL)
    lb_all = _lb_call(lower_bounds).reshape(depth, 1, 2 * HG_W)
    normv = norm_g.astype(F32).reshape(depth * 3, 1, D_MODEL)
    w_ffn_in_b = w_ffn_in.astype(BF16)
    w_ffn_out_b = w_ffn_out.astype(BF16)
    w_in_b = w_in.astype(BF16)
    w_branch_b = w_branch.astype(BF16)
    w_out_b = w_out.astype(BF16)
    rope = _rope_tables(dec_seq)

    def run(x, nb, s, latent):
        n = nb * s

        def rows_of(tm):
            return (lambda i: 1 + i // (s // tm)) if latent else (lambda i: 0)

        tm = min(DENSE_TM, s if latent else n)
        tm_ffn = min(FFN_TM, s if latent else n)
        x = x.reshape(n, D_MODEL)
        ctx_out = []
        for l in range(depth):
            x = _ffn_call(x, modv, normv, w_ffn_in_b, w_ffn_out_b, l, rows_of(tm_ffn), tm_ffn)
            y, gates, kv_att, kv_win, *leaves = _inproj_call(x, modv, normv, w_in_b, qk_norm_g, lb_all, l, rows_of(tm), tm,
                                                             rope=rope if latent else None, seq=s, emit_kv=not latent)
            if latent:
                o_att = _attn_call(y, kv_att, nb, s, COL_AQ, cache_k=cache_k_attn, cache_v=cache_v_attn,
                                   layer=l, rope=rope)
                o_win = _attn_call(y, kv_win, nb, s, COL_WQ, cache_k=cache_k_win, cache_v=cache_v_win,
                                   layer=l, rope=rope, sink=sink_logit, window=True)
                o_f, o_b, _ = _hgrn_call(y, nb, s, l, state0=state_hgrn)
            else:
                o_att = _attn_call(y, kv_att, nb, s, COL_AQ)
                o_win = _attn_call(y, kv_win, nb, s, COL_WQ, layer=l, sink=sink_logit)
                o_f, o_b, s_fin = _hgrn_call(y, nb, s, l, emit_state=True)
                ctx_out.append(tuple(t.reshape(nb, s, ATT_KV, HEAD_DIM) for t in leaves) + (s_fin,))
            x = _merge_call(x, y, gates, o_att, o_win, o_f, o_b, modv, normv, hg_norm_g, w_branch_b, w_out_b,
                            w_ffn_in_b, w_ffn_out_b, l, rows_of(tm), tm,
                            final_g=final_norm_g.astype(F32) if l == depth - 1 else None)
        return x.reshape(nb, s, D_MODEL), ctx_out

    y_prompt, ctx_out = run(x_prompt, batch, seq, False)
    y_sample, _ = run(x_sample, dec_batch, dec_seq, True)
    stack = lambda k: jnp.stack([cx[k] for cx in ctx_out], axis=1)
    return (y_prompt, y_sample, stack(0), stack(1), stack(2), stack(3), stack(4))
```

```python
import functools

import jax
import jax.numpy as jnp
import numpy as np
from jax import lax
from jax.experimental import pallas as pl
from jax.experimental.pallas import tpu as pltpu

F32 = jnp.float32
BF16 = jnp.bfloat16

D_MODEL = 1024
GRID_W = 64
HEAD_DIM = 64
ATT_KV = 2
ATT_HEADS = 8
GROUP_W = ATT_HEADS // ATT_KV * HEAD_DIM
WINDOW = 128
HG_HEADS = 4
HG_DK = 128
HG_W = HG_HEADS * HG_DK
BRANCH_W = 512
D_FF = 2816
ROPE_THETA = 10000.0
EPS = 1e-6
NEG_INF = -1e30
TINY = 1e-30
LOG2E = 1.4426950408889634
N_MOD = 9
IN_W = 7168

LANES = 128
SUBLANES = 8

COND_ROWS = 16
DENSE_TM = 512
FFN_TM = 1024
FFN_TF = 256
INPROJ_TN = 256
ATT_TQ = 512
ATT_SUB = 128
HG_CHUNK = 128
HG_STEP_CHUNKS = 4
HG_SUB = SUBLANES
PAIR_SUB = 16
VMEM_LIMIT = 56 * 1024 * 1024

COL_AQ, COL_AK, COL_AV = 0, 512, 640
COL_WQ, COL_WK, COL_WV = 768, 1280, 1408
COL_HQ, COL_HF, COL_HI, COL_HG = 1536, 2048, 3072, 3584
COL_GATES = 4096
Y_W = COL_GATES
KV_W = 6 * LANES
GATES_W = IN_W - COL_GATES


def _params(sem):
    return pltpu.CompilerParams(dimension_semantics=sem, vmem_limit_bytes=VMEM_LIMIT)


def _resident(block_shape, index_map):
    return pl.BlockSpec(block_shape, index_map, pipeline_mode=pl.Buffered(1))


def _sigmoid(x):
    return 1.0 / (1.0 + jnp.exp(-x))


def _dot(a, b):
    return jnp.dot(a, b, preferred_element_type=F32)


def _dot_nt(a, b):
    return lax.dot_general(a, b, (((1,), (1,)), ((), ())), preferred_element_type=F32)


def _dot_tn(a, b):
    return lax.dot_general(a, b, (((0,), (0,)), ((), ())), preferred_element_type=F32)


def _rms_rows(x, gain):
    return x * lax.rsqrt(jnp.mean(x * x, axis=-1, keepdims=True) + EPS) * gain


def _mod_kernel(c_ref, w_ref, b_ref, o_ref):
    c = c_ref[...]
    h = (c * _sigmoid(c)).astype(BF16)
    o_ref[...] = _dot(h, w_ref[...].astype(BF16)) + b_ref[...]


def _mod_call(cond, w_mod, b_mod):
    depth, d, nm = w_mod.shape
    tn = nm // 8
    return pl.pallas_call(
        _mod_kernel,
        grid=(depth, nm // tn),
        in_specs=[pl.BlockSpec((COND_ROWS, d), lambda l, j: (0, 0)),
                  pl.BlockSpec((None, d, tn), lambda l, j: (l, 0, j)),
                  pl.BlockSpec((None, 1, tn), lambda l, j: (l, 0, j))],
        out_specs=pl.BlockSpec((None, COND_ROWS, tn), lambda l, j: (l, 0, j)),
        out_shape=jax.ShapeDtypeStruct((depth, COND_ROWS, nm), F32),
        compiler_params=_params(("parallel", "parallel")),
        name="mod",
    )(cond, w_mod, b_mod.reshape(depth, 1, nm))


def _lb_kernel(x_ref, o_ref):
    x = x_ref[...]
    m = jnp.max(x, axis=0, keepdims=True)
    e = jnp.exp(x - m)
    s = e / jnp.sum(e, axis=0, keepdims=True)
    acc = jnp.zeros_like(s[0:1])
    for l in range(x.shape[0]):
        acc = acc + s[l:l + 1]
        o_ref[l:l + 1, :] = acc - s[0:1]


def _lb_call(lower_bounds):
    depth = lower_bounds.shape[0]
    x = lower_bounds.reshape(depth, -1).astype(F32)
    return pl.pallas_call(_lb_kernel, out_shape=jax.ShapeDtypeStruct(x.shape, F32), name="lb")(x)


def _ffn_block(x, shift, scale, gate, g, wgu_ref, wd_ref, o_ref, final_g):
    h = (_rms_rows(x, g) * (1.0 + scale) + shift).astype(BF16)
    for c in range(D_FF // FFN_TF):
        gu = _dot(h, wgu_ref[:, 2 * c * FFN_TF:2 * (c + 1) * FFN_TF])
        a, u = gu[:, :FFN_TF], gu[:, FFN_TF:]
        act = (a * _sigmoid(a) * u).astype(BF16)
        part = _dot(act, wd_ref[c * FFN_TF:(c + 1) * FFN_TF, :])
        if c == 0:
            o_ref[...] = part
        else:
            o_ref[...] += part
    out = x + 0.5 * gate * o_ref[...]
    if final_g is not None:
        out = _rms_rows(out, final_g)
    o_ref[...] = out


def _ffn_kernel(x_ref, shift_ref, scale_ref, gate_ref, g_ref, wgu_ref, wd_ref, o_ref):
    _ffn_block(x_ref[...], shift_ref[...], scale_ref[...], gate_ref[...], g_ref[...], wgu_ref, wd_ref, o_ref, None)


def _mod_spec(layer, k, row_fn):
    return pl.BlockSpec((None, 1, D_MODEL), lambda i: ((layer * COND_ROWS + row_fn(i)) * N_MOD + k, 0, 0))


def _ffn_specs(layer, which, row_fn):
    k0 = 3 * (2 * which)
    return [
        _mod_spec(layer, k0, row_fn), _mod_spec(layer, k0 + 1, row_fn), _mod_spec(layer, k0 + 2, row_fn),
        pl.BlockSpec((None, 1, D_MODEL), lambda i: (layer * 3 + 2 * which, 0, 0)),
        _resident((None, None, D_MODEL, 2 * D_FF), lambda i: (layer, which, 0, 0)),
        _resident((None, None, D_FF, D_MODEL), lambda i: (layer, which, 0, 0)),
    ]


def _ffn_call(x, modv, normv, w_in_b, w_out_b, layer, row_fn, tm):
    n = x.shape[0]
    return pl.pallas_call(
        _ffn_kernel,
        grid=(n // tm,),
        in_specs=[pl.BlockSpec((tm, D_MODEL), lambda i: (i, 0))] + _ffn_specs(layer, 0, row_fn),
        out_specs=pl.BlockSpec((tm, D_MODEL), lambda i: (i, 0)),
        out_shape=jax.ShapeDtypeStruct((n, D_MODEL), F32),
        compiler_params=_params(("parallel",)),
        name="ffn",
    )(x, modv, modv, modv, normv, w_in_b, w_out_b)


def _head_rms(y, gain):
    lane = lax.broadcasted_iota(jnp.int32, (1, LANES), 1)
    lo = lane < HEAD_DIM
    outs = []
    for c in range(y.shape[1] // LANES):
        blk = y[:, c * LANES:(c + 1) * LANES]
        sq = blk * blk
        s_lo = jnp.sum(jnp.where(lo, sq, 0.0), axis=-1, keepdims=True)
        s_hi = jnp.sum(jnp.where(lo, 0.0, sq), axis=-1, keepdims=True)
        ms = jnp.where(lo, s_lo, s_hi) * (1.0 / HEAD_DIM)
        outs.append(blk * lax.rsqrt(ms + EPS) * gain[:, c * LANES:(c + 1) * LANES])
    return outs[0] if len(outs) == 1 else jnp.concatenate(outs, axis=-1)


def _rope(x, cos, sin_signed):
    lane = lax.broadcasted_iota(jnp.int32, (1, LANES), 1)
    first = (lane & 31) < 16
    swapped = jnp.where(first, pltpu.roll(x, LANES - 16, 1), pltpu.roll(x, 16, 1))
    return x * cos + swapped * sin_signed


def _kv_operands(k, v):
    lane = lax.broadcasted_iota(jnp.int32, (1, LANES), 1)
    lo = lane < HEAD_DIM
    one_lo = jnp.where(lane == 0, 1.0, 0.0)
    one_hi = jnp.where(lane == HEAD_DIM, 1.0, 0.0)
    kr = pltpu.roll(k, HEAD_DIM, 1)
    vr = pltpu.roll(v, HEAD_DIM, 1)
    parts = [jnp.where(lo, k, kr), jnp.where(lo, kr, k),
             jnp.where(lo, v, one_hi), jnp.where(lo, one_lo, vr), jnp.where(lo, vr, one_hi), jnp.where(lo, one_lo, v)]
    return jnp.concatenate(parts, axis=-1).astype(BF16)


def _inproj_kernel(*refs, use_rope, emit_kv):
    x_ref, shift_ref, scale_ref, g_ref, w_ref, qg_ref, kg_ref, lb_ref = refs[:8]
    cos_ref, sin_ref = (refs[8], refs[9]) if use_rope else (None, None)
    outs = refs[10:] if use_rope else refs[8:]
    y_ref, gates_ref, kva_ref, kvw_ref = outs[:4]
    leaf_refs = outs[4:]
    h = (_rms_rows(x_ref[...], g_ref[...]) * (1.0 + scale_ref[...]) + shift_ref[...]).astype(BF16)

    def proj(c0, c1):
        return _dot(h, w_ref[:, c0:c1])

    tn = INPROJ_TN
    for c in range(COL_GATES, IN_W, tn):
        gates_ref[:, c - COL_GATES:c - COL_GATES + tn] = _sigmoid(proj(c, c + tn)).astype(BF16)
    for c in range(COL_HQ, COL_HF, tn):
        t = proj(c, c + tn)
        y_ref[:, c:c + tn] = t * _sigmoid(t)
    for c in range(COL_HF, COL_HI, tn):
        lb = lb_ref[:, c - COL_HF:c - COL_HF + tn]
        y_ref[:, c:c + tn] = lb + (1.0 - lb) * _sigmoid(proj(c, c + tn))
    for c in range(COL_AQ, COL_AK, tn):
        y_ref[:, c:c + tn] = _head_rms(proj(c, c + tn), qg_ref[:, c:c + tn])
    for i, (ck, out_ref) in enumerate(((COL_AK, kva_ref), (COL_WK, kvw_ref))):
        t = proj(ck, ck + 2 * LANES)
        k = t[:, :LANES]
        if ck == COL_AK:
            k = _head_rms(k, kg_ref[...])
        y_ref[:, ck:ck + LANES] = k
        y_ref[:, ck + LANES:ck + 2 * LANES] = t[:, LANES:]
        if emit_kv:
            leaf_refs[2 * i][...] = k
            leaf_refs[2 * i + 1][...] = t[:, LANES:]
        if use_rope:
            k = _rope(k, cos_ref[...], sin_ref[...])
        out_ref[...] = _kv_operands(k, t[:, LANES:])
    for c in list(range(COL_WQ, COL_WK, tn)) + list(range(COL_HI, COL_GATES, tn)):
        y_ref[:, c:c + tn] = proj(c, c + tn)


def _inproj_call(x, modv, normv, w_in_b, qk_norm_g, lb_all, layer, row_fn, tm, rope=None, seq=None, emit_kv=False):
    n = x.shape[0]
    use_rope = rope is not None
    qg = jnp.tile(qk_norm_g[layer, 0].astype(F32), (COL_AK - COL_AQ) // HEAD_DIM).reshape(1, COL_AK - COL_AQ)
    kg = jnp.tile(qk_norm_g[layer, 1].astype(F32), LANES // HEAD_DIM).reshape(1, LANES)
    in_specs = [
        pl.BlockSpec((tm, D_MODEL), lambda i: (i, 0)),
        _mod_spec(layer, 3, row_fn), _mod_spec(layer, 4, row_fn),
        pl.BlockSpec((None, 1, D_MODEL), lambda i: (layer * 3 + 1, 0, 0)),
        _resident((None, D_MODEL, IN_W), lambda i: (layer, 0, 0)),
        pl.BlockSpec((1, COL_AK - COL_AQ), lambda i: (0, 0)),
        pl.BlockSpec((1, LANES), lambda i: (0, 0)),
        pl.BlockSpec((None, 1, 2 * HG_W), lambda i: (layer, 0, 0)),
    ]
    args = [x, modv, modv, normv, w_in_b, qg, kg, lb_all]
    if use_rope:
        in_specs += [pl.BlockSpec((tm, LANES), lambda i: (i % (seq // tm), 0))] * 2
        args += list(rope)
    widths = (Y_W, GATES_W, KV_W, KV_W) + (LANES,) * (4 if emit_kv else 0)
    dtypes = (F32, BF16, BF16, BF16) + (F32,) * (4 if emit_kv else 0)
    return pl.pallas_call(
        functools.partial(_inproj_kernel, use_rope=use_rope, emit_kv=emit_kv),
        grid=(n // tm,),
        in_specs=in_specs,
        out_specs=[pl.BlockSpec((tm, w), lambda i: (i, 0)) for w in widths],
        out_shape=[jax.ShapeDtypeStruct((n, w), dt) for w, dt in zip(widths, dtypes)],
        compiler_params=_params(("parallel",)),
        name="inproj",
    )(*args)


def _attn_kernel(*refs, tq, ts, seq, past, use_rope, use_sink, window, sink_off):
    it = iter(refs)
    sink_ref = next(it) if use_sink else None
    q_refs = (next(it), next(it))
    kv_ref = next(it)
    ck_ref = cv_ref = cos_ref = sin_ref = None
    if past:
        ck_ref, cv_ref = next(it), next(it)
    if use_rope:
        cos_ref, sin_ref = next(it), next(it)
    o_ref = next(it)
    ctx_scr = next(it) if past else None

    qi = pl.program_id(1)
    lane = lax.broadcasted_iota(jnp.int32, (1, LANES), 1)
    lo = lane < HEAD_DIM

    if past:
        @pl.when(qi == 0)
        def _build():
            ctx_scr[...] = _kv_operands(ck_ref[...], cv_ref[...])

    q0 = pl.multiple_of(qi * tq, tq)
    q_scale = (HEAD_DIM ** -0.5) * LOG2E

    def unit_segments(h):
        ctx = [(ctx_scr, slice(0, past), None)] if past else []
        if not window:
            return ctx + [(kv_ref, slice(0, seq), None)]
        span = ts + 2 * WINDOW
        t0 = q0 + h * ts
        start = pl.multiple_of(jnp.clip(t0 - WINDOW, 0, seq - span), WINDOW)
        t_pos = t0 + lax.broadcasted_iota(jnp.int32, (ts, 1), 0)
        s_pos = start + lax.broadcasted_iota(jnp.int32, (1, span), 1)
        band_ok = jnp.abs(t_pos - s_pos) <= WINDOW
        return ctx + [(kv_ref, pl.ds(start, span), band_ok)]

    def keys(g, seg):
        ref, rsl, _ = seg
        return ref[rsl, g * LANES:(g + 1) * LANES]

    def values(g, par, seg):
        ref, rsl, _ = seg
        c0 = (2 + 2 * g + par) * LANES
        return ref[rsl, c0:c0 + LANES]

    def score_phase(g, h, segments):
        xs = []
        for p in range(2):
            q = q_refs[g][h * ts:(h + 1) * ts, p * LANES:(p + 1) * LANES]
            if use_rope:
                tsl = pl.ds(q0 + h * ts, ts)
                q = _rope(q, cos_ref[tsl, :], sin_ref[tsl, :])
            xs.append(q * q_scale)
        q4 = jnp.concatenate([jnp.where(lo, xs[0], 0.0), jnp.where(lo, xs[1], 0.0),
                              jnp.where(lo, 0.0, xs[0]), jnp.where(lo, 0.0, xs[1])], axis=0).astype(BF16)
        return [_dot_nt(q4, keys(g, seg)) for seg in segments]

    def softmax_phase(g, scores, segments):
        probs = [[] for _ in scores]
        extra = []
        for blk, head in enumerate((0, 2, 1, 3)):
            rsl = slice(blk * ts, (blk + 1) * ts)
            rows = [s[rsl] if seg[2] is None else jnp.where(seg[2], s[rsl], NEG_INF) for s, seg in zip(scores, segments)]
            m = None
            for s in rows:
                ms = jnp.max(s, axis=-1, keepdims=True)
                m = ms if m is None else jnp.maximum(m, ms)
            if use_sink:
                sk = sink_ref[sink_off + g * (ATT_HEADS // ATT_KV) + head] * LOG2E
                m = jnp.maximum(m, sk)
                extra.append(jnp.exp2(sk - m))
            for i, s in enumerate(rows):
                probs[i].append(jnp.exp2(s - m).astype(BF16))
        return probs, extra

    def value_phase(g, probs, segments):
        acc = [None, None]
        for i, seg in enumerate(segments):
            for par in range(2):
                e = jnp.concatenate(probs[i][2 * par:2 * par + 2], axis=0)
                pv = _dot(e, values(g, par, seg))
                acc[par] = pv if acc[par] is None else acc[par] + pv
        return acc

    def output_phase(g, h, acc, extra):
        outs = []
        for blk in range(4):
            par = blk // 2
            a = acc[par][(blk % 2) * ts:(blk % 2 + 1) * ts]
            den = jnp.sum(jnp.where(lane == (HEAD_DIM if par == 0 else 0), a, 0.0), axis=-1, keepdims=True)
            if use_sink:
                den = den + extra[blk]
            outs.append(a * (1.0 / den))
        for p in range(2):
            o_pair = jnp.where(lo, outs[p], outs[2 + p])
            c0 = g * GROUP_W + p * LANES
            o_ref[h * ts:(h + 1) * ts, c0:c0 + LANES] = o_pair.astype(o_ref.dtype)

    units = [(g, h) for h in range(tq // ts) for g in range(ATT_KV)]
    segs = {h: unit_segments(h) for h in range(tq // ts)}
    scores = [score_phase(g, h, segs[h]) for g, h in units]
    accs, extras = [], []
    for (g, h), sc in zip(units, scores):
        probs, extra = softmax_phase(g, sc, segs[h])
        accs.append(value_phase(g, probs, segs[h]))
        extras.append(extra)
    for (g, h), acc, extra in zip(units, accs, extras):
        output_phase(g, h, acc, extra)


def _attn_call(y, kv, nb, seq, qcol, *, cache_k=None, cache_v=None, layer=0, rope=None, sink=None, window=False):
    tq = min(ATT_TQ, seq)
    ts = min(ATT_SUB, tq)
    nq = seq // tq
    past = 0 if cache_k is None else cache_k.shape[2]
    use_rope = rope is not None
    use_sink = sink is not None
    in_specs, args = [], []
    if use_sink:
        in_specs.append(pl.BlockSpec(memory_space=pltpu.SMEM))
        args.append(sink.reshape(-1).astype(F32))
    for g in range(ATT_KV):
        in_specs.append(pl.BlockSpec((tq, GROUP_W), lambda b, qi, g=g: (b * nq + qi, qcol // GROUP_W + g)))
        args.append(y)
    in_specs.append(pl.BlockSpec((seq, KV_W), lambda b, qi: (b, 0)))
    args.append(kv)
    if past:
        cshape = cache_k.shape[:3] + (LANES,)
        in_specs += [pl.BlockSpec((None, None, past, LANES), lambda b, qi: (b, layer, 0, 0))] * 2
        args += [cache_k.reshape(cshape), cache_v.reshape(cshape)]
    if use_rope:
        in_specs += [pl.BlockSpec((seq, LANES), lambda b, qi: (0, 0))] * 2
        args += list(rope)
    kern = functools.partial(_attn_kernel, tq=tq, ts=ts, seq=seq, past=past, use_rope=use_rope,
                             use_sink=use_sink, window=window, sink_off=layer * ATT_HEADS)
    return pl.pallas_call(
        kern,
        grid=(nb, nq),
        in_specs=in_specs,
        out_specs=pl.BlockSpec((tq, BRANCH_W), lambda b, qi: (b * nq + qi, 0)),
        out_shape=jax.ShapeDtypeStruct((nb * seq, BRANCH_W), BF16),
        scratch_shapes=[pltpu.VMEM((past, KV_W), BF16)] if past else [],
        compiler_params=_params(("parallel", "arbitrary")),
        name="win" if window or use_sink else "att",
    )(*args)


def _pair_table(fwd):
    C = HG_CHUNK
    t = np.arange(C)[:, None]
    s = np.arange(C)[None, :]
    seen = (s <= t) if fwd else (s >= t)
    tab = np.full((C, C), -1, np.int32)
    hh, idx = C // 2, (C // HG_SUB).bit_length() - 2
    while hh >= HG_SUB:
        tab = np.where((t // (2 * hh) == s // (2 * hh)) & seen, idx, tab)
        hh, idx = hh // 2, idx - 1
    tab = np.where((t // HG_SUB == s // HG_SUB), np.where(seen, PAIR_SUB + s % HG_SUB, -1), tab)
    return tab.astype(np.int32)


def _hgrn_unit(q, k, b, v, st, sub_masks, level_masks, fwd):
    C = HG_CHUNK
    G = C // HG_SUB
    vb = v.astype(BF16)

    b3 = b.reshape(G, HG_SUB, HG_DK)
    q3 = q.reshape(G, HG_SUB, HG_DK)
    c3 = (jnp.log2(k) - b).reshape(G, HG_SUB, HG_DK)
    a3 = jnp.zeros((G, HG_SUB, C), F32)
    for jj in range(HG_SUB):
        kdec = jnp.exp2(b3 + c3[:, jj:jj + 1, :])
        score = jnp.sum(q3 * kdec, axis=-1, keepdims=True)
        a3 = jnp.where(sub_masks[jj], score, a3)
    a_mat = a3.reshape(C, C)

    tcol = lax.broadcasted_iota(jnp.int32, (C, 1), 0)
    hh, idx = HG_SUB, 0
    while hh < C:
        grp = 2 * hh
        later = (tcol & (grp - 1)) >= hh
        is_q = later if fwd else jnp.logical_not(later)
        bg = b.reshape(C // grp, grp, HG_DK)
        bnd = bg[:, hh - 1:hh, :] if fwd else bg[:, hh:hh + 1, :]
        bnd = jnp.broadcast_to(bnd, (C // grp, grp, HG_DK)).reshape(C, HG_DK)
        w = (jnp.where(is_q, q, k) * jnp.exp2(jnp.where(is_q, b - bnd, bnd - b))).astype(BF16)
        a_mat = jnp.where(level_masks[idx], _dot_nt(w, w), a_mat)
        hh, idx = grp, idx + 1
    out = _dot(a_mat.astype(BF16), vb)

    out = out + _dot_nt((q * jnp.exp2(b)).astype(BF16), st.astype(BF16))
    blast = b[C - 1:C, :] if fwd else b[0:1, :]
    kh = (k * jnp.exp2(blast - b)).astype(BF16)
    st_new = st * jnp.exp2(blast) + _dot_tn(vb, kh)
    return out, st_new


def _hgrn_kernel(*refs, nsteps, cps, has_init, emit_state):
    it = iter(refs)
    io = [(next(it), next(it), next(it)) for _ in range(2)]
    pair_ref = next(it)
    s0_ref = next(it) if has_init else None
    o_refs = (next(it), next(it))
    sfin_ref = next(it) if emit_state else None
    st_scr = next(it)

    j = pl.program_id(1)
    C = HG_CHUNK

    @pl.when(j == 0)
    def _():
        for dirn in range(2):
            for h in range(HG_HEADS):
                if has_init:
                    st_scr[dirn, h] = s0_ref[dirn, h].T
                else:
                    st_scr[dirn, h] = jnp.zeros((HG_DK, HG_DK), F32)

    ti = lax.broadcasted_iota(jnp.int32, (C, C), 0)
    si = lax.broadcasted_iota(jnp.int32, (C, C), 1)
    tris, masks = [], []
    for dirn in range(2):
        tris.append(jnp.where((ti >= si) if dirn == 0 else (ti <= si), 1.0, 0.0).astype(BF16))
        pair = pair_ref[dirn]
        pair3 = pair.reshape(C // HG_SUB, HG_SUB, C)
        masks.append(([pair3 == PAIR_SUB + jj for jj in range(HG_SUB)],
                      [pair == idx for idx in range((C // HG_SUB).bit_length() - 1)]))
    states = [[st_scr[dirn, h] for h in range(HG_HEADS)] for dirn in range(2)]
    for cc in range(cps):
        for dirn in range(2):
            fwd = dirn == 0
            q_ref, f_ref, v_ref = io[dirn]
            r0 = (cc if fwd else cps - 1 - cc) * C
            rows = slice(r0, r0 + C)
            f = f_ref[rows, :]
            l2 = jnp.log2(jnp.maximum(f, TINY))
            l_hi = l2.astype(BF16)
            r1 = l2 - l_hi.astype(F32)
            l_mid = r1.astype(BF16)
            l_lo = (r1 - l_mid.astype(F32)).astype(BF16)
            b_all = _dot(tris[dirn], l_hi) + _dot(tris[dirn], l_mid) + _dot(tris[dirn], l_lo)
            for h in range(HG_HEADS):
                sl = slice(h * HG_DK, (h + 1) * HG_DK)
                out, st_new = _hgrn_unit(q_ref[rows, sl], jnp.maximum(1.0 - f[:, sl], 0.0), b_all[:, sl],
                                         v_ref[rows, sl], states[dirn][h], masks[dirn][0], masks[dirn][1], fwd)
                o_refs[dirn][rows, sl] = out
                states[dirn][h] = st_new
    for dirn in range(2):
        for h in range(HG_HEADS):
            st_scr[dirn, h] = states[dirn][h]
            if emit_state:
                @pl.when(j == nsteps - 1)
                def _():
                    sfin_ref[dirn, h] = states[dirn][h].T


def _hgrn_call(y, nb, seq, layer, state0=None, emit_state=False):
    cps = min(HG_STEP_CHUNKS, seq // HG_CHUNK)
    rows = cps * HG_CHUNK
    nsteps = seq // rows
    has_init = state0 is not None
    in_specs, args = [], []
    for dirn in range(2):
        rowblk = (lambda n, j: n * nsteps + j) if dirn == 0 else (lambda n, j: n * nsteps + nsteps - 1 - j)
        for col in (COL_HQ, COL_HF + dirn * HG_W, COL_HI):
            in_specs.append(pl.BlockSpec((rows, HG_W), lambda n, j, rowblk=rowblk, col=col: (rowblk(n, j), col // HG_W)))
            args.append(y)
    in_specs.append(pl.BlockSpec((2, HG_CHUNK, HG_CHUNK), lambda n, j: (0, 0, 0)))
    args.append(jnp.asarray([_pair_table(True), _pair_table(False)]))
    if has_init:
        in_specs.append(pl.BlockSpec((None, None, 2, HG_HEADS, HG_DK, HG_DK), lambda n, j: (n, layer, 0, 0, 0, 0)))
        args.append(state0)
    out_specs = [pl.BlockSpec((rows, HG_W), lambda n, j: (n * nsteps + j, 0)),
                 pl.BlockSpec((rows, HG_W), lambda n, j: (n * nsteps + nsteps - 1 - j, 0))]
    out_shape = [jax.ShapeDtypeStruct((nb * seq, HG_W), F32)] * 2
    if emit_state:
        out_specs.append(pl.BlockSpec((None, 2, HG_HEADS, HG_DK, HG_DK), lambda n, j: (n, 0, 0, 0, 0)))
        out_shape.append(jax.ShapeDtypeStruct((nb, 2, HG_HEADS, HG_DK, HG_DK), F32))
    res = pl.pallas_call(
        functools.partial(_hgrn_kernel, nsteps=nsteps, cps=cps, has_init=has_init, emit_state=emit_state),
        grid=(nb, nsteps),
        in_specs=in_specs,
        out_specs=out_specs,
        out_shape=out_shape,
        scratch_shapes=[pltpu.VMEM((2, HG_HEADS, HG_DK, HG_DK), F32)],
        compiler_params=_params(("parallel", "arbitrary")),
        name="hgrn",
    )(*args)
    return (res[0], res[1], res[2]) if emit_state else (res[0], res[1], None)


def _merge_kernel(*refs, final):
    (oa_ref, ow_ref, of_ref, ob_ref, hg_ref, gates_ref, x_ref, gate_ref, hgn_ref, wb_ref, wo_ref,
     shift2_ref, scale2_ref, gate2_ref, g2_ref, wgu_ref, wd_ref) = refs[:17]
    fg_ref = refs[17] if final else None
    o_ref = refs[-1]
    o = of_ref[...] + ob_ref[...]
    hg = hg_ref[...]
    hgn = hgn_ref[...]
    parts = []
    for h in range(HG_HEADS):
        sl = slice(h * HG_DK, (h + 1) * HG_DK)
        g = hg[:, sl]
        parts.append(_rms_rows(o[:, sl], hgn) * (g * _sigmoid(g)))
    o_hg = jnp.concatenate(parts, axis=-1).astype(BF16)
    branch = (oa_ref[...], o_hg, ow_ref[...])
    merged = None
    for k in range(3):
        term = gates_ref[:, k * D_MODEL:(k + 1) * D_MODEL].astype(F32) * _dot(branch[k], wb_ref[k])
        merged = term if merged is None else merged + term
    yv = _dot(merged.astype(BF16), wo_ref[...])
    x_mid = x_ref[...] + gate_ref[...] * yv
    _ffn_block(x_mid, shift2_ref[...], scale2_ref[...], gate2_ref[...], g2_ref[...], wgu_ref, wd_ref, o_ref,
               fg_ref[...] if final else None)


def _merge_call(x, y, gates, o_att, o_win, o_f, o_b, modv, normv, hg_norm_g, w_branch_b, w_out_b,
                w_ffn_in_b, w_ffn_out_b, layer, row_fn, tm, final_g=None):
    n = x.shape[0]
    final = final_g is not None
    row = lambda i: (i, 0)
    in_specs = [
        pl.BlockSpec((tm, BRANCH_W), row),
        pl.BlockSpec((tm, BRANCH_W), row),
        pl.BlockSpec((tm, HG_W), row),
        pl.BlockSpec((tm, HG_W), row),
        pl.BlockSpec((tm, HG_W), lambda i: (i, COL_HG // HG_W)),
        pl.BlockSpec((tm, GATES_W), row),
        pl.BlockSpec((tm, D_MODEL), row),
        _mod_spec(layer, 5, row_fn),
        pl.BlockSpec((1, HG_DK), lambda i: (0, 0)),
        _resident((None, 3, BRANCH_W, D_MODEL), lambda i: (layer, 0, 0, 0)),
        _resident((None, D_MODEL, D_MODEL), lambda i: (layer, 0, 0)),
    ] + _ffn_specs(layer, 1, row_fn)
    args = [o_att, o_win, o_f, o_b, y, gates, x, modv, hg_norm_g[layer].reshape(1, HG_DK).astype(F32),
            w_branch_b, w_out_b, modv, modv, modv, normv, w_ffn_in_b, w_ffn_out_b]
    if final:
        in_specs.append(pl.BlockSpec((1, D_MODEL), lambda i: (0, 0)))
        args.append(final_g.reshape(1, D_MODEL))
    return pl.pallas_call(
        functools.partial(_merge_kernel, final=final),
        grid=(n // tm,),
        in_specs=in_specs,
        out_specs=pl.BlockSpec((tm, D_MODEL), row),
        out_shape=jax.ShapeDtypeStruct((n, D_MODEL), F32),
        compiler_params=_params(("parallel",)),
        name="merge_ffn",
    )(*args)


def _rope_tables(n_lat):
    t = jnp.arange(n_lat, dtype=jnp.int32)
    row = (t // GRID_W).astype(F32)
    col = (t % GRID_W).astype(F32)
    axis_dim = HEAD_DIM // 2
    inv = ROPE_THETA ** (-jnp.arange(0, axis_dim, 2, dtype=F32) / axis_dim)
    ang_r = row[:, None] * inv
    ang_c = col[:, None] * inv
    cos64 = jnp.concatenate([jnp.cos(ang_r), jnp.cos(ang_r), jnp.cos(ang_c), jnp.cos(ang_c)], axis=-1)
    sin64 = jnp.concatenate([-jnp.sin(ang_r), jnp.sin(ang_r), -jnp.sin(ang_c), jnp.sin(ang_c)], axis=-1)
    return jnp.tile(cos64, (1, 2)), jnp.tile(sin64, (1, 2))


def kernel(x_prompt, x_sample, cache_k_attn, cache_v_attn, cache_k_win, cache_v_win, state_hgrn, c, c_ctx,
           w_mod, b_mod, norm_g, w_ffn_in, w_ffn_out, w_in, qk_norm_g, lower_bounds, hg_norm_g, sink_logit,
           w_branch, w_out, final_norm_g):
    batch, seq, _ = x_prompt.shape
    dec_batch, dec_seq, _ = x_sample.shape
    depth = w_mod.shape[0]

    cond = jnp.zeros((COND_ROWS, D_MODEL), F32).at[0].set(c_ctx).at[1:1 + dec_batch].set(c)
    modv = _mod_call(cond, w_mod, b_mod).reshape(depth * COND_ROWS * N_MOD, 1, D_MODEL)
    lb_all = _lb_call(lower_bounds).reshape(depth, 1, 2 * HG_W)
    normv = norm_g.astype(F32).reshape(depth * 3, 1, D_MODEL)
    nchunk = D_FF // FFN_TF
    w_ffn_in_b = (w_ffn_in.astype(BF16).reshape(depth, 2, D_MODEL, 2, nchunk, FFN_TF)
                  .transpose(0, 1, 2, 4, 3, 5).reshape(depth, 2, D_MODEL, 2 * D_FF))
    w_ffn_out_b = w_ffn_out.astype(BF16)
    w_in_b = w_in.astype(BF16)
    w_branch_b = w_branch.astype(BF16)
    w_out_b = w_out.astype(BF16)
    rope = _rope_tables(dec_seq)

    def run(x, nb, s, latent):
        n = nb * s

        def rows_of(tm):
            return (lambda i: 1 + i // (s // tm)) if latent else (lambda i: 0)

        tm = min(DENSE_TM, s if latent else n)
        tm_ffn = min(FFN_TM, s if latent else n)
        x = x.reshape(n, D_MODEL)
        ctx_out = []
        for l in range(depth):
            x = _ffn_call(x, modv, normv, w_ffn_in_b, w_ffn_out_b, l, rows_of(tm_ffn), tm_ffn)
            y, gates, kv_att, kv_win, *leaves = _inproj_call(x, modv, normv, w_in_b, qk_norm_g, lb_all, l, rows_of(tm), tm,
                                                             rope=rope if latent else None, seq=s, emit_kv=not latent)
            if latent:
                o_att = _attn_call(y, kv_att, nb, s, COL_AQ, cache_k=cache_k_attn, cache_v=cache_v_attn,
                                   layer=l, rope=rope)
                o_win = _attn_call(y, kv_win, nb, s, COL_WQ, cache_k=cache_k_win, cache_v=cache_v_win,
                                   layer=l, rope=rope, sink=sink_logit, window=True)
                o_f, o_b, _ = _hgrn_call(y, nb, s, l, state0=state_hgrn)
            else:
                o_att = _attn_call(y, kv_att, nb, s, COL_AQ)
                o_win = _attn_call(y, kv_win, nb, s, COL_WQ, layer=l, sink=sink_logit)
                o_f, o_b, s_fin = _hgrn_call(y, nb, s, l, emit_state=True)
                ctx_out.append(tuple(t.reshape(nb, s, ATT_KV, HEAD_DIM) for t in leaves) + (s_fin,))
            x = _merge_call(x, y, gates, o_att, o_win, o_f, o_b, modv, normv, hg_norm_g, w_branch_b, w_out_b,
                            w_ffn_in_b, w_ffn_out_b, l, rows_of(tm), tm,
                            final_g=final_norm_g.astype(F32) if l == depth - 1 else None)
        return x.reshape(nb, s, D_MODEL), ctx_out

    y_prompt, ctx_out = run(x_prompt, batch, seq, False)
    y_sample, _ = run(x_sample, dec_batch, dec_seq, True)
    stack = lambda k: jnp.stack([cx[k] for cx in ctx_out], axis=1)
    return (y_prompt, y_sample, stack(0), stack(1), stack(2), stack(3), stack(4))
```
